```python
import math
import jax, jax.numpy as jnp
from jax import lax
import numpy as np

D_MODEL = 1024
BATCH = 8
SEQ = 8192
DEPTH = 2

D_FF = 2816
N_EVEN = (DEPTH + 1) // 2
N_ODD = DEPTH // 2
S5_WIDTH = D_MODEL // 2
S5_GROUP = 16
S5_GROUPS = S5_WIDTH // S5_GROUP
S5_STATE = 64
GLA_HEADS = 4
GLA_DK = D_MODEL // 4 // GLA_HEADS
GLA_DV = D_MODEL // 2 // GLA_HEADS
GLA_RANK = 16
GLA_GATE_NORM = 16.0
RET_HEADS = 8
RET_DK = D_MODEL // RET_HEADS
RET_DV = 2 * D_MODEL // RET_HEADS
ROPE_BASE = 10000.0
CHUNK = 64
EPS = 1e-6

AB_IN = S5_WIDTH + 2 * GLA_HEADS * GLA_DK + 2 * GLA_HEADS * GLA_DV + 2 * GLA_RANK
AB_OUT = S5_WIDTH + GLA_HEADS * GLA_DV
RET_IN = 2 * RET_HEADS * RET_DK + 2 * RET_HEADS * RET_DV
RET_OUT = RET_HEADS * RET_DV

kernel_name = 'hybrid_s5_gla_retention_macaron_encoder'


def rmsnorm(x, g):
    xf = x.astype(jnp.float32)
    y = xf * lax.rsqrt(jnp.mean(xf * xf, axis=-1, keepdims=True) + EPS)
    return (y * g.astype(jnp.float32)).astype(x.dtype)


def head_rmsnorm(o, g):
    y = o * lax.rsqrt(jnp.mean(o * o, axis=-1, keepdims=True) + EPS)
    return y * g.astype(jnp.float32).reshape(o.shape[-2], o.shape[-1])


def swiglu_ffn(h, w1, w2):
    gate, up = jnp.split(h @ w1, 2, axis=-1)
    return (jax.nn.silu(gate) * up) @ w2


def to_heads(t, n_heads):
    b, l, _ = t.shape
    return t.reshape(b, l, n_heads, -1).transpose(0, 2, 1, 3)


def flip_seq(t):
    return jnp.flip(t, axis=2)


def _complex_affine_combine(e1, e2):
    a1r, a1i, b1r, b1i = e1
    a2r, a2i, b2r, b2i = e2
    return (a2r * a1r - a2i * a1i,
            a2r * a1i + a2i * a1r,
            a2r * b1r - a2i * b1i + b2r,
            a2r * b1i + a2i * b1r + b2i)


def s5_bidirectional(u, lam_re, lam_im, b_re, b_im, c_re, c_im, log_dt, d_skip):
    f32 = jnp.float32
    uf = u.astype(f32)
    seq_len = u.shape[1]
    y = uf * d_skip.astype(f32).reshape(S5_GROUPS, S5_GROUP)
    for direction in range(2):
        lr = jnp.minimum(lam_re[direction].astype(f32), -1e-4)
        li = lam_im[direction].astype(f32)
        dt = jnp.exp(log_dt[direction].astype(f32))[:, None]
        mag = jnp.exp(lr * dt)
        ar = mag * jnp.cos(li * dt)
        ai = mag * jnp.sin(li * dt)
        den = lr * lr + li * li
        cr = ((ar - 1.0) * lr + ai * li) / den
        ci = (ai * lr - (ar - 1.0) * li) / den
        br = b_re[direction].astype(f32)
        bi = b_im[direction].astype(f32)
        bbr = cr[..., None] * br - ci[..., None] * bi
        bbi = cr[..., None] * bi + ci[..., None] * br
        bu_r = jnp.einsum('blgp,gnp->blgn', uf, bbr)
        bu_i = jnp.einsum('blgp,gnp->blgn', uf, bbi)
        a_r = jnp.broadcast_to(ar[None, None], (1, seq_len) + ar.shape)
        a_i = jnp.broadcast_to(ai[None, None], (1, seq_len) + ai.shape)
        _, _, xr, xi = lax.associative_scan(_complex_affine_combine, (a_r, a_i, bu_r, bu_i),
                                            reverse=(direction == 1), axis=1)
        y = y + jnp.einsum('blgn,gpn->blgp', xr, c_re[direction].astype(f32)) \
              - jnp.einsum('blgn,gpn->blgp', xi, c_im[direction].astype(f32))
    return y


def gla_chunked(q, k, v, g, strict):
    b_, h_, l_, dk = q.shape
    dv = v.shape[-1]
    n = l_ // CHUNK
    q = q.reshape(b_, h_, n, CHUNK, dk)
    k = k.reshape(b_, h_, n, CHUNK, dk)
    v = v.reshape(b_, h_, n, CHUNK, dv)
    g = g.reshape(b_, h_, n, CHUNK, dk)
    cum = jnp.cumsum(g, axis=3)
    q_dec = q * jnp.exp(cum)
    k_inv = k * jnp.exp(-cum)
    mask = jnp.tril(jnp.ones((CHUNK, CHUNK), dtype=bool), k=-1 if strict else 0)
    scores = jnp.where(mask, jnp.einsum('bhnid,bhnjd->bhnij', q_dec, k_inv), 0.0)
    o_intra = jnp.einsum('bhnij,bhnjv->bhniv', scores, v)
    last = cum[:, :, :, -1:, :]
    kv = jnp.einsum('bhnjd,bhnjv->bhndv', k * jnp.exp(last - cum), v)
    chunk_decay = jnp.exp(last[:, :, :, 0, :])

    def step(state, inp):
        dec, kv_c = inp
        return dec[..., None] * state + kv_c, state

    _, s_before = lax.scan(step, jnp.zeros((b_, h_, dk, dv), jnp.float32),
                           (jnp.moveaxis(chunk_decay, 2, 0), jnp.moveaxis(kv, 2, 0)))
    s_before = jnp.moveaxis(s_before, 0, 2)
    o_inter = jnp.einsum('bhnid,bhndv->bhniv', q_dec, s_before)
    return (o_intra + o_inter).reshape(b_, h_, l_, dv)


def retention_chunkwise(q, k, v, log_gamma, strict):
    b_, h_, l_, dk = q.shape
    dv = v.shape[-1]
    n = l_ // CHUNK
    q = q.reshape(b_, h_, n, CHUNK, dk)
    k = k.reshape(b_, h_, n, CHUNK, dk)
    v = v.reshape(b_, h_, n, CHUNK, dv)
    idx_i = jnp.arange(CHUNK)
    diff_i = idx_i[:, None] - idx_i[None, :]
    mask = diff_i >= (1 if strict else 0)
    idx = idx_i.astype(jnp.float32)
    diff = jnp.maximum(diff_i, 0).astype(jnp.float32)
    decay_mat = jnp.where(mask[None], jnp.exp(diff[None] * log_gamma[:, None, None]), 0.0)
    scores = jnp.einsum('bhnid,bhnjd->bhnij', q, k) * decay_mat[None, :, None]
    o_intra = jnp.einsum('bhnij,bhnjv->bhniv', scores, v)
    zeta = jnp.exp((CHUNK - 1.0 - idx)[None, :] * log_gamma[:, None])
    xi = jnp.exp((idx + 1.0)[None, :] * log_gamma[:, None])
    kv = jnp.einsum('bhnjd,bhnjv,hj->bhndv', k, v, zeta)
    chunk_decay = jnp.exp(CHUNK * log_gamma)

    def step(state, kv_c):
        return chunk_decay[None, :, None, None] * state + kv_c, state

    _, r_before = lax.scan(step, jnp.zeros((b_, h_, dk, dv), jnp.float32), jnp.moveaxis(kv, 2, 0))
    r_before = jnp.moveaxis(r_before, 0, 2)
    o_inter = jnp.einsum('bhnid,bhndv,hi->bhniv', q, r_before, xi)
    return (o_intra + o_inter).reshape(b_, h_, l_, dv)


def rotary(t):
    dk = t.shape[-1]
    half = dk // 2
    pos = jnp.arange(t.shape[2], dtype=jnp.float32)
    inv = jnp.exp(-math.log(ROPE_BASE) * jnp.arange(half, dtype=jnp.float32) / half)
    ang = pos[:, None] * inv[None, :]
    cos, sin = jnp.cos(ang), jnp.sin(ang)
    t1, t2 = t[..., :half], t[..., half:]
    return jnp.concatenate([t1 * cos - t2 * sin, t1 * sin + t2 * cos], axis=-1)


def s5_gla_mixer(h, w_in, lam_re, lam_im, b_re, b_im, c_re, c_im, log_dt, d_skip, w_glu,
                 w_gk, b_gk, gla_norm, w_out):
    b_, l_, _ = h.shape
    hk = GLA_HEADS * GLA_DK
    hv = GLA_HEADS * GLA_DV
    proj = h @ w_in
    cuts = [S5_WIDTH, S5_WIDTH + hk, S5_WIDTH + 2 * hk, S5_WIDTH + 2 * hk + hv, S5_WIDTH + 2 * hk + 2 * hv]
    u, q, k, v, og, glo = jnp.split(proj, cuts, axis=-1)
    y = s5_bidirectional(u.reshape(b_, l_, S5_GROUPS, S5_GROUP), lam_re, lam_im, b_re, b_im,
                         c_re, c_im, log_dt, d_skip).reshape(b_, l_, S5_WIDTH)
    gy = jax.nn.gelu(y).astype(h.dtype)
    s5_out = gy * jax.nn.sigmoid(gy @ w_glu)
    glo = glo.reshape(b_, l_, 2, GLA_RANK)
    gk = jnp.einsum('blsr,srk->blsk', glo, w_gk) + b_gk
    gk = jax.nn.log_sigmoid(gk.astype(jnp.float32)) / GLA_GATE_NORM
    qh = to_heads(q, GLA_HEADS).astype(jnp.float32) * GLA_DK ** -0.5
    kh = to_heads(k, GLA_HEADS).astype(jnp.float32)
    vh = to_heads(v, GLA_HEADS).astype(jnp.float32)
    gf = to_heads(gk[:, :, 0], GLA_HEADS)
    gb = to_heads(gk[:, :, 1], GLA_HEADS)
    o_f = gla_chunked(qh, kh, vh, gf, strict=False)
    o_b = flip_seq(gla_chunked(flip_seq(qh), flip_seq(kh), flip_seq(vh), flip_seq(gb), strict=True))
    o = head_rmsnorm((o_f + o_b).transpose(0, 2, 1, 3), gla_norm).reshape(b_, l_, hv)
    gla_out = o.astype(h.dtype) * jax.nn.silu(og)
    return jnp.concatenate([s5_out, gla_out], axis=-1) @ w_out


def retention_mixer(h, w_in, ret_norm, w_out):
    b_, l_, _ = h.shape
    hk = RET_HEADS * RET_DK
    hv = RET_HEADS * RET_DV
    q, k, v, og = jnp.split(h @ w_in, [hk, 2 * hk, 2 * hk + hv], axis=-1)
    qh = rotary(to_heads(q, RET_HEADS).astype(jnp.float32))
    kh = rotary(to_heads(k, RET_HEADS).astype(jnp.float32)) * RET_DK ** -0.5
    vh = to_heads(v, RET_HEADS).astype(jnp.float32)
    lg_f = jnp.log1p(-jnp.exp2(-5.0 - jnp.arange(RET_HEADS, dtype=jnp.float32)))
    lg_b = lg_f[::-1]
    o_f = retention_chunkwise(qh, kh, vh, lg_f, strict=False)
    o_b = flip_seq(retention_chunkwise(flip_seq(qh), flip_seq(kh), flip_seq(vh), lg_b, strict=True))
    o = head_rmsnorm((o_f + o_b).transpose(0, 2, 1, 3), ret_norm).reshape(b_, l_, hv)
    return (o.astype(h.dtype) * jax.nn.silu(og)) @ w_out


def setup_inputs(seed: int = 0) -> dict:
    key = jax.random.key(seed)
    ks = jax.random.split(key, 32)
    f32 = jnp.float32

    def nrm(k, shape, scale):
        return jax.random.normal(k, shape, f32) * scale

    def gain(k, shape):
        return 1.0 + 0.02 * jax.random.normal(k, shape, f32)

    n_idx = jnp.arange(S5_STATE, dtype=f32)
    return {
        'x': nrm(ks[0], (BATCH, SEQ, D_MODEL), 1.0),
        'ffn1_norm': gain(ks[1], (DEPTH, D_MODEL)),
        'ffn1_w1': nrm(ks[2], (DEPTH, D_MODEL, 2 * D_FF), D_MODEL ** -0.5),
        'ffn1_w2': nrm(ks[3], (DEPTH, D_FF, D_MODEL), D_FF ** -0.5),
        'mix_norm': gain(ks[4], (DEPTH, D_MODEL)),
        'ffn2_norm': gain(ks[5], (DEPTH, D_MODEL)),
        'ffn2_w1': nrm(ks[6], (DEPTH, D_MODEL, 2 * D_FF), D_MODEL ** -0.5),
        'ffn2_w2': nrm(ks[7], (DEPTH, D_FF, D_MODEL), D_FF ** -0.5),
        'ab_w_in': nrm(ks[8], (N_EVEN, D_MODEL, AB_IN), D_MODEL ** -0.5),
        's5_lambda_re': -0.5 + 0.01 * jax.random.normal(ks[9], (N_EVEN, 2, S5_GROUPS, S5_STATE), f32),
        's5_lambda_im': math.pi * n_idx + 0.01 * jax.random.normal(ks[10], (N_EVEN, 2, S5_GROUPS, S5_STATE), f32),
        's5_b_re': nrm(ks[11], (N_EVEN, 2, S5_GROUPS, S5_STATE, S5_GROUP), (2 * S5_GROUP) ** -0.5),
        's5_b_im': nrm(ks[12], (N_EVEN, 2, S5_GROUPS, S5_STATE, S5_GROUP), (2 * S5_GROUP) ** -0.5),
        's5_c_re': nrm(ks[13], (N_EVEN, 2, S5_GROUPS, S5_GROUP, S5_STATE), (2 * S5_STATE) ** -0.5),
        's5_c_im': nrm(ks[14], (N_EVEN, 2, S5_GROUPS, S5_GROUP, S5_STATE), (2 * S5_STATE) ** -0.5),
        's5_log_dt': jax.random.uniform(ks[15], (N_EVEN, 2, S5_GROUPS), f32, math.log(1e-3), math.log(1e-1)),
        's5_d': nrm(ks[16], (N_EVEN, S5_WIDTH), 1.0),
        's5_w_glu': nrm(ks[17], (N_EVEN, S5_WIDTH, S5_WIDTH), S5_WIDTH ** -0.5),
        'gla_w_gk': nrm(ks[18], (N_EVEN, 2, GLA_RANK, GLA_HEADS * GLA_DK), GLA_RANK ** -0.5),
        'gla_b_gk': nrm(ks[19], (N_EVEN, 2, GLA_HEADS * GLA_DK), 0.1),
        'gla_norm': gain(ks[20], (N_EVEN, GLA_HEADS * GLA_DV)),
        'ab_w_out': nrm(ks[21], (N_EVEN, AB_OUT, D_MODEL), AB_OUT ** -0.5),
        'ret_w_in': nrm(ks[22], (N_ODD, D_MODEL, RET_IN), D_MODEL ** -0.5),
        'ret_norm': gain(ks[23], (N_ODD, RET_OUT)),
        'ret_w_out': nrm(ks[24], (N_ODD, RET_OUT, D_MODEL), RET_OUT ** -0.5),
        'final_norm': gain(ks[25], (D_MODEL,)),
    }


def reference(x, ffn1_norm, ffn1_w1, ffn1_w2, mix_norm, ffn2_norm, ffn2_w1, ffn2_w2,
              ab_w_in, s5_lambda_re, s5_lambda_im, s5_b_re, s5_b_im, s5_c_re, s5_c_im,
              s5_log_dt, s5_d, s5_w_glu, gla_w_gk, gla_b_gk, gla_norm, ab_w_out,
              ret_w_in, ret_norm, ret_w_out, final_norm):
    for i in range(DEPTH):
        j = i // 2
        x = x + 0.5 * swiglu_ffn(rmsnorm(x, ffn1_norm[i]), ffn1_w1[i], ffn1_w2[i])
        h = rmsnorm(x, mix_norm[i])
        if i % 2 == 0:
            x = x + s5_gla_mixer(h, ab_w_in[j], s5_lambda_re[j], s5_lambda_im[j], s5_b_re[j], s5_b_im[j],
                                 s5_c_re[j], s5_c_im[j], s5_log_dt[j], s5_d[j], s5_w_glu[j],
                                 gla_w_gk[j], gla_b_gk[j], gla_norm[j], ab_w_out[j])
        else:
            x = x + retention_mixer(h, ret_w_in[j], ret_norm[j], ret_w_out[j])
        x = x + 0.5 * swiglu_ffn(rmsnorm(x, ffn2_norm[i]), ffn2_w1[i], ffn2_w2[i])
    return rmsnorm(x, final_norm)
```

```python
import functools
import math

import jax
import jax.numpy as jnp
from jax import lax
from jax.experimental import pallas as pl
from jax.experimental.pallas import tpu as pltpu

F32 = jnp.float32
BF16 = jnp.bfloat16

EPS = 1e-6
S5_GROUP = 16
S5_STATE = 64
GLA_HEADS = 4
GLA_RANK = 16
GLA_GATE_NORM = 16.0
GLA_CHUNK = 64
RET_HEADS = 8
ROPE_BASE = 10000.0

LANES = 128
VMEM_LIMIT = 56 * 1024 * 1024
HI = lax.Precision.HIGHEST


def _const_spec(shape):
    nd = len(shape)
    return pl.BlockSpec(shape, lambda *_: (0,) * nd, pipeline_mode=pl.Buffered(1))


def _params(*sem):
    return pltpu.CompilerParams(dimension_semantics=sem, vmem_limit_bytes=VMEM_LIMIT)


def _rmsnorm(x, g):
    return x * lax.rsqrt(jnp.mean(x * x, axis=-1, keepdims=True) + EPS) * g


def _sigmoid(x):
    return 1.0 / (1.0 + jnp.exp(-x))


def _silu(x):
    return x * _sigmoid(x)


def _gelu_tanh(x):
    return 0.5 * x * (1.0 + jnp.tanh(math.sqrt(2.0 / math.pi) * (x + 0.044715 * (x * x * x))))


def _log_sigmoid(x):
    return jnp.minimum(x, 0.0) - jnp.log1p(jnp.exp(-jnp.abs(x)))


def _dot(a, b):
    return jnp.dot(a, b, preferred_element_type=F32)


def _dot_nt(a, b):
    return lax.dot_general(a, b, (((1,), (1,)), ((), ())), preferred_element_type=F32)


def _dot_tn(a, b):
    return lax.dot_general(a, b, (((0,), (0,)), ((), ())), preferred_element_type=F32)


def _ffn_body(x_ref, g_ref, w1_ref, w2_ref, fg_ref, o_ref, *, d_ff, f_chunk, final):
    x = x_ref[...]
    xn = _rmsnorm(x, g_ref[...]).astype(BF16)
    acc = jnp.zeros(x.shape, F32)
    for f0 in range(0, d_ff, f_chunk):
        gate = _dot(xn, w1_ref[:, f0:f0 + f_chunk])
        up = _dot(xn, w1_ref[:, d_ff + f0:d_ff + f0 + f_chunk])
        act = (_silu(gate) * up).astype(BF16)
        acc = acc + _dot(act, w2_ref[f0:f0 + f_chunk, :])
    y = x + 0.5 * acc
    if final:
        y = _rmsnorm(y, fg_ref[...])
    o_ref[...] = y


def _ffn(x2, g, w1, w2, fg, *, final, tm):
    t, d = x2.shape
    d_ff = w2.shape[0]
    f_chunk = 256 if d_ff % 256 == 0 else d_ff
    body = functools.partial(_ffn_body, d_ff=d_ff, f_chunk=f_chunk, final=final)
    return pl.pallas_call(
        body,
        grid=(t // tm,),
        in_specs=[pl.BlockSpec((tm, d), lambda i: (i, 0)),
                  _const_spec((1, d)), _const_spec(w1.shape), _const_spec(w2.shape),
                  _const_spec((1, d))],
        out_specs=pl.BlockSpec((tm, d), lambda i: (i, 0)),
        out_shape=jax.ShapeDtypeStruct((t, d), F32),
        compiler_params=_params("parallel"),
        name="ffn",
    )(x2, g.reshape(1, d), w1, w2, fg.reshape(1, d))


def _ab_in_body(x_ref, g_ref, wut_ref, wr_ref, wgk_ref, bgk_ref,
                ut_ref, q_ref, k_ref, v_ref, og_ref, gf_ref, gb_ref, *, hk, hv):
    h = _rmsnorm(x_ref[...], g_ref[...]).astype(BF16)
    ut_ref[0] = _dot_nt(wut_ref[...], h).astype(BF16)
    proj = _dot(h, wr_ref[...])
    q_ref[...] = proj[:, :hk] * (hk // GLA_HEADS) ** -0.5
    k_ref[...] = proj[:, hk:2 * hk]
    v_ref[...] = proj[:, 2 * hk:2 * hk + hv].astype(BF16)
    og_ref[...] = proj[:, 2 * hk + hv:2 * hk + 2 * hv].astype(BF16)
    glo = proj[:, 2 * hk + 2 * hv:].astype(BF16)
    gk = _log_sigmoid(_dot(glo, wgk_ref[...]) + bgk_ref[...]) * (1.0 / GLA_GATE_NORM)
    gf_ref[...] = gk[:, :hk]
    gb_ref[...] = gk[:, hk:]


def _ab_in(x2, g, wut, wr, wgk, bgk, *, b, l, tm, hk, hv):
    t, d = x2.shape
    su = wut.shape[0]
    ni = l // tm
    tok = lambda w: pl.BlockSpec((tm, w), lambda bi, i: (bi * ni + i, 0))
    body = functools.partial(_ab_in_body, hk=hk, hv=hv)
    return pl.pallas_call(
        body,
        grid=(b, ni),
        in_specs=[tok(d), _const_spec((1, d)), _const_spec(wut.shape), _const_spec(wr.shape),
                  _const_spec(wgk.shape), _const_spec(bgk.shape)],
        out_specs=[pl.BlockSpec((1, su, tm), lambda bi, i: (bi, 0, i)),
                   tok(hk), tok(hk), tok(hv), tok(hv), tok(hk), tok(hk)],
        out_shape=[jax.ShapeDtypeStruct((b, su, l), BF16),
                   jax.ShapeDtypeStruct((t, hk), F32), jax.ShapeDtypeStruct((t, hk), F32),
                   jax.ShapeDtypeStruct((t, hv), BF16), jax.ShapeDtypeStruct((t, hv), BF16),
                   jax.ShapeDtypeStruct((t, hk), F32), jax.ShapeDtypeStruct((t, hk), F32)],
        compiler_params=_params("parallel", "parallel"),
        name="ab_in",
    )(x2, g.reshape(1, d), wut, wr, wgk, bgk)


def _cpow_table(ar, ai, count):
    ks = jnp.arange(count)
    pr = jnp.ones(ar.shape + (count,), F32)
    pi = jnp.zeros(ar.shape + (count,), F32)
    sr, si = ar, ai
    for bit in range(max(1, (count - 1).bit_length())):
        on = ((ks >> bit) & 1) == 1
        nr = pr * sr[..., None] - pi * si[..., None]
        ni = pr * si[..., None] + pi * sr[..., None]
        pr = jnp.where(on, nr, pr)
        pi = jnp.where(on, ni, pi)
        sr, si = sr * sr - si * si, 2.0 * sr * si
    return pr, pi


def _s5_tables(lam_re, lam_im, b_re, b_im, c_re, c_im, log_dt, d_skip, n_chunks):
    ch = LANES
    g_, n_ = lam_re.shape[1], lam_re.shape[2]
    p_ = b_re.shape[-1]
    f32 = lambda a: a.astype(F32)
    lr = jnp.minimum(f32(lam_re), -1e-4)
    li = f32(lam_im)
    dt = jnp.exp(f32(log_dt))[..., None]
    mag = jnp.exp(lr * dt)
    ar = mag * jnp.cos(li * dt)
    ai = mag * jnp.sin(li * dt)
    den = lr * lr + li * li
    cr = ((ar - 1.0) * lr + ai * li) / den
    ci = (ai * lr - (ar - 1.0) * li) / den
    br, bi = f32(b_re), f32(b_im)
    bbr = cr[..., None] * br - ci[..., None] * bi
    bbi = cr[..., None] * bi + ci[..., None] * br
    ccr, cci = f32(c_re), f32(c_im)
    pwr, pwi = _cpow_table(ar, ai, ch + 1)

    cnr = ccr.transpose(0, 1, 3, 2)[..., None]
    cni = cci.transpose(0, 1, 3, 2)[..., None]
    cbr = cnr * bbr[:, :, :, None, :] - cni * bbi[:, :, :, None, :]
    cbi = cnr * bbi[:, :, :, None, :] + cni * bbr[:, :, :, None, :]
    taps = (jnp.einsum('dgnpq,dgnl->dgqpl', cbr, pwr[..., :ch], precision=HI)
            - jnp.einsum('dgnpq,dgnl->dgqpl', cbi, pwi[..., :ch], precision=HI))
    kf, kb = taps[0], taps[1]
    dmat = f32(d_skip).reshape(g_, p_)[:, None, :, None] * jnp.eye(p_, dtype=F32)[None, :, :, None]
    center = kf[..., 0:1] + kb[..., 0:1] + dmat
    kfull = jnp.concatenate([kb[..., :0:-1], center, kf[..., 1:]], axis=-1)
    kpad = jnp.pad(kfull, ((0, 0),) * 3 + ((0, 1),)).astype(BF16)
    skew = jnp.broadcast_to(kpad[..., None, :], kpad.shape[:3] + (ch, 2 * ch))
    skew = skew.reshape(kpad.shape[:3] + (ch * 2 * ch,))[..., :ch * (2 * ch - 1)]
    toep = skew.reshape(kpad.shape[:3] + (ch, 2 * ch - 1))[..., ch - 1:]
    toep = toep.transpose(0, 1, 3, 2, 4).reshape(g_, p_ * ch, p_ * ch)

    def win(d, rev):
        pr = pwr[d][..., :ch]
        pi = pwi[d][..., :ch]
        if rev:
            pr, pi = pr[..., ::-1], pi[..., ::-1]
        pr = pr.transpose(0, 2, 1)[:, None]
        pi = pi.transpose(0, 2, 1)[:, None]
        xr = bbr[d].transpose(0, 2, 1)[:, :, None, :]
        xi = bbi[d].transpose(0, 2, 1)[:, :, None, :]
        return pr * xr - pi * xi, pr * xi + pi * xr
    fre, fim = win(0, True)
    bre, bim = win(1, False)
    w_in = jnp.concatenate([fre, fim, bre, bim], axis=-1).reshape(g_, p_ * ch, 4 * n_)
    wt = jnp.concatenate([toep, w_in.astype(BF16)], axis=-1)

    def wout(d, rev):
        pr = pwr[d][..., 1:ch + 1]
        pi = pwi[d][..., 1:ch + 1]
        if rev:
            pr, pi = pr[..., ::-1], pi[..., ::-1]
        pr = pr[:, :, None, :]
        pi = pi[:, :, None, :]
        xr = ccr[d].transpose(0, 2, 1)[..., None]
        xi = cci[d].transpose(0, 2, 1)[..., None]
        return xr * pr - xi * pi, -(xr * pi + xi * pr)
    fre, fim = wout(0, False)
    bre, bim = wout(1, True)
    wc = jnp.concatenate([fre, fim, bre, bim], axis=1).reshape(g_, 4 * n_, p_ * ch).astype(BF16)

    steps = max(1, (n_chunks - 1).bit_length())
    sr, si = pwr[..., ch], pwi[..., ch]
    rows = []
    for _ in range(steps):
        rows.append(jnp.stack([jnp.concatenate([sr, sr], -1), jnp.concatenate([-si, si], -1)], axis=2))
        sr, si = sr * sr - si * si, 2.0 * sr * si
    pq = jnp.stack(rows, axis=2)
    pq = pq.transpose(1, 0, 2, 3, 4)
    return wt, wc, pq


def _s5_body(u_ref, wt_ref, wc_ref, pq_ref, y_ref, *, nb, nc, p, n, steps):
    ch = LANES
    m = nb * nc
    u = jnp.concatenate([u_ref[:, q].reshape(m, ch) for q in range(p)], axis=1)
    r = _dot(u, wt_ref[0])
    y = r[:, :p * ch]
    c = lax.broadcasted_iota(jnp.int32, (m, 2 * n), 0) % nc

    def scan(x, d):
        for k in range(steps):
            s = 1 << k
            if d == 0:
                t = jnp.where(c >= s, pltpu.roll(x, s, axis=0), 0.0)
            else:
                t = jnp.where(c < nc - s, pltpu.roll(x, m - s, axis=0), 0.0)
            x = x + t * pq_ref[0, d, k, 0:1, :] + pltpu.roll(t, n, axis=1) * pq_ref[0, d, k, 1:2, :]
        if d == 0:
            return jnp.where(c >= 1, pltpu.roll(x, 1, axis=0), 0.0)
        return jnp.where(c < nc - 1, pltpu.roll(x, m - 1, axis=0), 0.0)

    xf = scan(r[:, p * ch:p * ch + 2 * n], 0)
    xb = scan(r[:, p * ch + 2 * n:], 1)
    xc = jnp.concatenate([xf, xb], axis=1).astype(BF16)
    y = y + _dot(xc, wc_ref[0])
    for q in range(p):
        y_ref[:, q] = y[:, q * ch:(q + 1) * ch].reshape(nb, nc, ch)


def _s5(u4, wt, wc, pq):
    nb, su, nc, ch = u4.shape
    g_ = wt.shape[0]
    p = su // g_
    n = wc.shape[1] // 4
    steps = pq.shape[2]
    body = functools.partial(_s5_body, nb=nb, nc=nc, p=p, n=n, steps=steps)
    return pl.pallas_call(
        body,
        grid=(g_,),
        in_specs=[pl.BlockSpec((nb, p, nc, ch), lambda g: (0, g, 0, 0)),
                  pl.BlockSpec((1,) + wt.shape[1:], lambda g: (g, 0, 0)),
                  pl.BlockSpec((1,) + wc.shape[1:], lambda g: (g, 0, 0)),
                  pl.BlockSpec((1,) + pq.shape[1:], lambda g: (g, 0, 0, 0, 0))],
        out_specs=pl.BlockSpec((nb, p, nc, ch), lambda g: (0, g, 0, 0)),
        out_shape=jax.ShapeDtypeStruct((nb, su, nc, ch), F32),
        compiler_params=_params("parallel"),
        name="s5",
    )(u4, wt, wc, pq)


def _gla_chunk(q, k, v, g, st_ref, tri, mask, last_row, dk, dv):
    g_hi = g.astype(BF16)
    g_lo = (g - g_hi.astype(F32)).astype(BF16)
    cum = _dot(tri, g_hi) + _dot(tri, g_lo)
    last = cum[last_row:last_row + 1, :]
    e = jnp.exp(cum)
    q_dec = (q * e).astype(BF16)
    k_inv = (k * jnp.exp(-cum)).astype(BF16)
    k_dec = (k * jnp.exp(last - cum)).astype(BF16)
    decay = jnp.exp(last)
    outs = []
    for h in range(GLA_HEADS):
        ks = slice(h * dk, (h + 1) * dk)
        vs = slice(h * dv, (h + 1) * dv)
        s = jnp.where(mask, _dot_nt(q_dec[:, ks], k_inv[:, ks]), 0.0)
        st = st_ref[h]
        o = _dot(s.astype(BF16), v[:, vs]) + _dot_nt(q_dec[:, ks], st.astype(BF16))
        st_ref[h] = decay[:, ks] * st + _dot_tn(v[:, vs], k_dec[:, ks])
        outs.append(o)
    return jnp.concatenate(outs, axis=1)


def _gla_body(qf_ref, kf_ref, vf_ref, gf_ref, qb_ref, kb_ref, vb_ref, gb_ref,
              of_ref, ob_ref, sf_ref, sb_ref, *, tb, dk, dv):
    c = GLA_CHUNK

    @pl.when(pl.program_id(1) == 0)
    def _():
        sf_ref[...] = jnp.zeros(sf_ref.shape, F32)
        sb_ref[...] = jnp.zeros(sb_ref.shape, F32)

    ii = lax.broadcasted_iota(jnp.int32, (c, c), 0)
    jj = lax.broadcasted_iota(jnp.int32, (c, c), 1)
    tri_f = (jj <= ii).astype(BF16)
    tri_b = (jj >= ii).astype(BF16)
    mask_f = jj <= ii
    mask_b = jj > ii
    for n in range(tb // c):
        rf = slice(n * c, (n + 1) * c)
        of_ref[rf, :] = _gla_chunk(qf_ref[rf, :], kf_ref[rf, :], vf_ref[rf, :], gf_ref[rf, :],
                                   sf_ref, tri_f, mask_f, c - 1, dk, dv)
        nr = tb // c - 1 - n
        rb = slice(nr * c, (nr + 1) * c)
        ob_ref[rb, :] = _gla_chunk(qb_ref[rb, :], kb_ref[rb, :], vb_ref[rb, :], gb_ref[rb, :],
                                   sb_ref, tri_b, mask_b, 0, dk, dv)


def _gla(q, k, v, gf, gb, *, b, l, tb):
    t, hk = q.shape
    hv = v.shape[1]
    dk, dv = hk // GLA_HEADS, hv // GLA_HEADS
    ni = l // tb
    fwd = lambda w: pl.BlockSpec((tb, w), lambda bi, i: (bi * ni + i, 0))
    bwd = lambda w: pl.BlockSpec((tb, w), lambda bi, i: (bi * ni + ni - 1 - i, 0))
    body = functools.partial(_gla_body, tb=tb, dk=dk, dv=dv)
    return pl.pallas_call(
        body,
        grid=(b, ni),
        in_specs=[fwd(hk), fwd(hk), fwd(hv), fwd(hk), bwd(hk), bwd(hk), bwd(hv), bwd(hk)],
        out_specs=[fwd(hv), bwd(hv)],
        out_shape=[jax.ShapeDtypeStruct((t, hv), F32), jax.ShapeDtypeStruct((t, hv), F32)],
        scratch_shapes=[pltpu.VMEM((GLA_HEADS, dv, dk), F32), pltpu.VMEM((GLA_HEADS, dv, dk), F32)],
        compiler_params=_params("parallel", "arbitrary"),
        name="gla",
    )(q, k, v, gf, q, k, v, gb)


def _head_norm(o, g, heads):
    dh = o.shape[1] // heads
    parts = []
    for h in range(heads):
        oh = o[:, h * dh:(h + 1) * dh]
        parts.append(oh * lax.rsqrt(jnp.mean(oh * oh, axis=-1, keepdims=True) + EPS))
    return jnp.concatenate(parts, axis=1) * g


def _ab_out_body(x_ref, yt_ref, wglut_ref, of_ref, ob_ref, og_ref, gn_ref, wo_ref, o_ref, *, su):
    gy = _gelu_tanh(yt_ref[0])
    z = _dot(wglut_ref[...], gy.astype(BF16))
    s5_out = (gy * _sigmoid(z)).T.astype(BF16)
    o = _head_norm(of_ref[...] + ob_ref[...], gn_ref[...], GLA_HEADS)
    gla_out = (o * _silu(og_ref[...].astype(F32))).astype(BF16)
    o_ref[...] = x_ref[...] + _dot(s5_out, wo_ref[:su, :]) + _dot(gla_out, wo_ref[su:, :])


def _ab_out(x2, yt, wglut, o_f, o_b, og, gn, wo, *, b, l, tm):
    t, d = x2.shape
    su = yt.shape[1]
    hv = o_f.shape[1]
    ni = l // tm
    tok = lambda w: pl.BlockSpec((tm, w), lambda bi, i: (bi * ni + i, 0))
    body = functools.partial(_ab_out_body, su=su)
    return pl.pallas_call(
        body,
        grid=(b, ni),
        in_specs=[tok(d), pl.BlockSpec((1, su, tm), lambda bi, i: (bi, 0, i)), _const_spec(wglut.shape),
                  tok(hv), tok(hv), tok(hv), _const_spec((1, hv)), _const_spec(wo.shape)],
        out_specs=tok(d),
        out_shape=jax.ShapeDtypeStruct((t, d), F32),
        compiler_params=_params("parallel", "parallel"),
        name="ab_out",
    )(x2, yt, wglut, o_f, o_b, og, gn.reshape(1, hv), wo)


def _ret_in_body(x_ref, g_ref, w_ref, cos_ref, sin_ref, q_ref, k_ref, v_ref, og_ref, *, hk, hv, dk):
    h = _rmsnorm(x_ref[...], g_ref[...]).astype(BF16)
    cos2 = cos_ref[...]
    sin2 = sin_ref[...]

    def rope(t):
        parts = []
        for hh in range(hk // dk):
            th = t[:, hh * dk:(hh + 1) * dk]
            parts.append(th * cos2 + pltpu.roll(th, dk // 2, axis=1) * sin2)
        return jnp.concatenate(parts, axis=1)

    q_ref[...] = rope(_dot(h, w_ref[:, :hk])).astype(BF16)
    k_ref[...] = (rope(_dot(h, w_ref[:, hk:2 * hk])) * dk ** -0.5).astype(BF16)
    v_ref[...] = _dot(h, w_ref[:, 2 * hk:2 * hk + hv]).astype(BF16)
    og_ref[...] = _dot(h, w_ref[:, 2 * hk + hv:]).astype(BF16)


def _ret_in(x2, g, w, cos2, sin2, *, b, l, tm, hk, hv):
    t, d = x2.shape
    dk = hk // RET_HEADS
    ni = l // tm
    tok = lambda w_: pl.BlockSpec((tm, w_), lambda bi, i: (bi * ni + i, 0))
    pos = pl.BlockSpec((tm, dk), lambda bi, i: (i, 0))
    body = functools.partial(_ret_in_body, hk=hk, hv=hv, dk=dk)
    return pl.pallas_call(
        body,
        grid=(b, ni),
        in_specs=[tok(d), _const_spec((1, d)), _const_spec(w.shape), pos, pos],
        out_specs=[tok(hk), tok(hk), tok(hv), tok(hv)],
        out_shape=[jax.ShapeDtypeStruct((t, hk), BF16), jax.ShapeDtypeStruct((t, hk), BF16),
                   jax.ShapeDtypeStruct((t, hv), BF16), jax.ShapeDtypeStruct((t, hv), BF16)],
        compiler_params=_params("parallel", "parallel"),
        name="ret_in",
    )(x2, g.reshape(1, d), w, cos2, sin2)


def _ret_tables(c, dk, dv):
    lg_f = jnp.log1p(-jnp.exp2(-5.0 - jnp.arange(RET_HEADS, dtype=F32)))
    lg_b = lg_f[::-1]
    idx = jnp.arange(c)
    diff = (idx[:, None] - idx[None, :]).astype(F32)
    dmat = jnp.where(diff >= 0, jnp.exp(jnp.maximum(diff, 0.0)[None] * lg_f[:, None, None]),
                     jnp.exp(jnp.maximum(-diff, 0.0)[None] * lg_b[:, None, None]))
    pos = idx.astype(F32)
    ones_k = jnp.ones((1, 1, dk), F32)
    ones_v = jnp.ones((1, 1, dv), F32)
    xi_f = jnp.exp((pos + 1.0)[None, :] * lg_f[:, None])[..., None] * ones_v
    xi_b = jnp.exp((c - pos)[None, :] * lg_b[:, None])[..., None] * ones_v
    zeta_f = jnp.exp((c - 1.0 - pos)[None, :] * lg_f[:, None])[..., None] * ones_k
    zeta_b = jnp.exp(pos[None, :] * lg_b[:, None])[..., None] * ones_k
    cd_f = jnp.exp(c * lg_f)[:, None, None] * ones_v
    cd_b = jnp.exp(c * lg_b)[:, None, None] * ones_v
    return dmat, xi_f, xi_b, zeta_f, zeta_b, cd_f, cd_b


def _ret_body(qf_ref, kf_ref, vf_ref, qb_ref, kb_ref, vb_ref,
              dmat_ref, xif_ref, xib_ref, zf_ref, zb_ref, cdf_ref, cdb_ref,
              of_ref, ob_ref, sf_ref, sb_ref, *, dk, dv):
    @pl.when(pl.program_id(1) == 0)
    def _():
        sf_ref[...] = jnp.zeros(sf_ref.shape, F32)
        sb_ref[...] = jnp.zeros(sb_ref.shape, F32)

    for h in range(RET_HEADS):
        ks = slice(h * dk, (h + 1) * dk)
        vs = slice(h * dv, (h + 1) * dv)
        q = qf_ref[:, ks]
        k = kf_ref[:, ks]
        v = vf_ref[:, vs]
        s = _dot_nt(q, k) * dmat_ref[h]
        st = sf_ref[h]
        of_ref[:, vs] = _dot(s.astype(BF16), v) + _dot(q, st.astype(BF16)) * xif_ref[h]
        kz = (k.astype(F32) * zf_ref[h]).astype(BF16)
        sf_ref[h] = cdf_ref[h] * st + _dot_tn(kz, v)
        q = qb_ref[:, ks]
        k = kb_ref[:, ks]
        v = vb_ref[:, vs]
        st = sb_ref[h]
        ob_ref[:, vs] = _dot(q, st.astype(BF16)) * xib_ref[h]
        kz = (k.astype(F32) * zb_ref[h]).astype(BF16)
        sb_ref[h] = cdb_ref[h] * st + _dot_tn(kz, v)


def _ret(q, k, v, *, b, l, tb):
    t, hk = q.shape
    hv = v.shape[1]
    dk, dv = hk // RET_HEADS, hv // RET_HEADS
    ni = l // tb
    tabs = _ret_tables(tb, dk, dv)
    fwd = lambda w: pl.BlockSpec((tb, w), lambda bi, i: (bi * ni + i, 0))
    bwd = lambda w: pl.BlockSpec((tb, w), lambda bi, i: (bi * ni + ni - 1 - i, 0))
    body = functools.partial(_ret_body, dk=dk, dv=dv)
    return pl.pallas_call(
        body,
        grid=(b, ni),
        in_specs=[fwd(hk), fwd(hk), fwd(hv), bwd(hk), bwd(hk), bwd(hv)] + [_const_spec(a.shape) for a in tabs],
        out_specs=[fwd(hv), bwd(hv)],
        out_shape=[jax.ShapeDtypeStruct((t, hv), F32), jax.ShapeDtypeStruct((t, hv), F32)],
        scratch_shapes=[pltpu.VMEM((RET_HEADS, dk, dv), F32), pltpu.VMEM((RET_HEADS, dk, dv), F32)],
        compiler_params=_params("parallel", "arbitrary"),
        name="ret",
    )(q, k, v, q, k, v, *tabs)


def _ret_out_body(x_ref, of_ref, ob_ref, og_ref, gn_ref, wo_ref, o_ref):
    o = _head_norm(of_ref[...] + ob_ref[...], gn_ref[...], RET_HEADS)
    gated = (o * _silu(og_ref[...].astype(F32))).astype(BF16)
    o_ref[...] = x_ref[...] + _dot(gated, wo_ref[...])


def _ret_out(x2, o_f, o_b, og, gn, wo, *, tm):
    t, d = x2.shape
    hv = o_f.shape[1]
    tok = lambda w: pl.BlockSpec((tm, w), lambda i: (i, 0))
    return pl.pallas_call(
        _ret_out_body,
        grid=(t // tm,),
        in_specs=[tok(d), tok(hv), tok(hv), tok(hv), _const_spec((1, hv)), _const_spec(wo.shape)],
        out_specs=tok(d),
        out_shape=jax.ShapeDtypeStruct((t, d), F32),
        compiler_params=_params("parallel"),
        name="ret_out",
    )(x2, o_f, o_b, og, gn.reshape(1, hv), wo)


def _tile(l, want):
    t = min(l, want)
    while l % t:
        t //= 2
    return t


def _s5_gla_layer(x2, mix_g, w_in, lam_re, lam_im, b_re, b_im, c_re, c_im, log_dt, d_skip, w_glu,
                  w_gk, b_gk, gla_norm, w_out, *, b, l):
    su = d_skip.shape[0]
    hk = w_gk.shape[-1]
    r = w_gk.shape[1]
    hv = gla_norm.shape[0]
    tm = _tile(l, 512)
    wut = w_in[:, :su].T.astype(BF16)
    wr = w_in[:, su:].astype(BF16)
    wgk = jnp.zeros((2 * r, 2 * hk), F32)
    wgk = wgk.at[:r, :hk].set(w_gk[0]).at[r:, hk:].set(w_gk[1]).astype(BF16)
    bgk = b_gk.reshape(1, 2 * hk).astype(F32)
    ut, q, k, v, og, gf, gb = _ab_in(x2, mix_g, wut, wr, wgk, bgk, b=b, l=l, tm=tm, hk=hk, hv=hv)

    nc = l // LANES
    wt, wc, pq = _s5_tables(lam_re, lam_im, b_re, b_im, c_re, c_im, log_dt, d_skip, nc)
    y4 = _s5(ut.reshape(b, su, nc, LANES), wt, wc, pq)
    yt = y4.reshape(b, su, l)

    o_f, o_b = _gla(q, k, v, gf, gb, b=b, l=l, tb=_tile(l, 512))
    return _ab_out(x2, yt, w_glu.T.astype(BF16), o_f, o_b, og, gla_norm, w_out.astype(BF16), b=b, l=l, tm=tm)


def _retention_layer(x2, mix_g, w_in, ret_norm, w_out, *, b, l):
    hv = ret_norm.shape[0]
    hk = (w_in.shape[1] - 2 * hv) // 2
    dk = hk // RET_HEADS
    half = dk // 2
    tm = _tile(l, 512)
    pos = jnp.arange(l, dtype=F32)
    inv = jnp.exp(-math.log(ROPE_BASE) * jnp.arange(half, dtype=F32) / half)
    ang = pos[:, None] * inv[None, :]
    cos, sin = jnp.cos(ang), jnp.sin(ang)
    cos2 = jnp.concatenate([cos, cos], axis=1)
    sin2 = jnp.concatenate([-sin, sin], axis=1)
    q, k, v, og = _ret_in(x2, mix_g, w_in.astype(BF16), cos2, sin2, b=b, l=l, tm=tm, hk=hk, hv=hv)
    o_f, o_b = _ret(q, k, v, b=b, l=l, tb=_tile(l, 256))
    return _ret_out(x2, o_f, o_b, og, ret_norm, w_out.astype(BF16), tm=tm)


def kernel(x, ffn1_norm, ffn1_w1, ffn1_w2, mix_norm, ffn2_norm, ffn2_w1, ffn2_w2, ab_w_in, s5_lambda_re, s5_lambda_im, s5_b_re, s5_b_im, s5_c_re, s5_c_im, s5_log_dt, s5_d, s5_w_glu, gla_w_gk, gla_b_gk, gla_norm, ab_w_out, ret_w_in, ret_norm, ret_w_out, final_norm):
    b, l, d = x.shape
    depth = ffn1_norm.shape[0]
    tm = _tile(b * l, 512)
    x2 = x.reshape(b * l, d).astype(F32)
    for i in range(depth):
        j = i // 2
        x2 = _ffn(x2, ffn1_norm[i], ffn1_w1[i].astype(BF16), ffn1_w2[i].astype(BF16), final_norm,
                  final=False, tm=tm)
        if i % 2 == 0:
            x2 = _s5_gla_layer(x2, mix_norm[i], ab_w_in[j], s5_lambda_re[j], s5_lambda_im[j], s5_b_re[j],
                               s5_b_im[j], s5_c_re[j], s5_c_im[j], s5_log_dt[j], s5_d[j], s5_w_glu[j],
                               gla_w_gk[j], gla_b_gk[j], gla_norm[j], ab_w_out[j], b=b, l=l)
        else:
            x2 = _retention_layer(x2, mix_norm[i], ret_w_in[j], ret_norm[j], ret_w_out[j], b=b, l=l)
        x2 = _ffn(x2, ffn2_norm[i], ffn2_w1[i].astype(BF16), ffn2_w2[i].astype(BF16), final_norm,
                  final=(i == depth - 1), tm=tm)
    return x2.reshape(b, l, d)
```

```python
import functools
import math

import jax
import jax.numpy as jnp
from jax import lax
from jax.experimental import pallas as pl
from jax.experimental.pallas import tpu as pltpu

F32 = jnp.float32
BF16 = jnp.bfloat16

EPS = 1e-6
S5_GROUP = 16
S5_STATE = 64
GLA_HEADS = 4
GLA_RANK = 16
GLA_GATE_NORM = 16.0
GLA_CHUNK = 64
RET_HEADS = 8
ROPE_BASE = 10000.0

LANES = 128
VMEM_LIMIT = 56 * 1024 * 1024
HI = lax.Precision.HIGHEST


def _const_spec(shape):
    nd = len(shape)
    return pl.BlockSpec(shape, lambda *_: (0,) * nd, pipeline_mode=pl.Buffered(1))


def _params(*sem):
    return pltpu.CompilerParams(dimension_semantics=sem, vmem_limit_bytes=VMEM_LIMIT)


def _rmsnorm(x, g):
    return x * lax.rsqrt(jnp.mean(x * x, axis=-1, keepdims=True) + EPS) * g


def _sigmoid(x):
    return 1.0 / (1.0 + jnp.exp(-x))


def _silu(x):
    return x * _sigmoid(x)


def _gelu_tanh(x):
    return 0.5 * x * (1.0 + jnp.tanh(math.sqrt(2.0 / math.pi) * (x + 0.044715 * (x * x * x))))


def _log_sigmoid(x):
    return jnp.minimum(x, 0.0) - jnp.log1p(jnp.exp(-jnp.abs(x)))


def _dot(a, b):
    return jnp.dot(a, b, preferred_element_type=F32)


def _dot_nt(a, b):
    return lax.dot_general(a, b, (((1,), (1,)), ((), ())), preferred_element_type=F32)


def _dot_tn(a, b):
    return lax.dot_general(a, b, (((0,), (0,)), ((), ())), preferred_element_type=F32)


def _ffn_body(x_ref, g_ref, w1_ref, w2_ref, fg_ref, o_ref, *, d_ff, f_chunk, final):
    x = x_ref[...]
    xn = _rmsnorm(x, g_ref[...]).astype(BF16)
    acc = jnp.zeros(x.shape, F32)
    for f0 in range(0, d_ff, f_chunk):
        gate = _dot(xn, w1_ref[:, f0:f0 + f_chunk])
        up = _dot(xn, w1_ref[:, d_ff + f0:d_ff + f0 + f_chunk])
        act = (_silu(gate) * up).astype(BF16)
        acc = acc + _dot(act, w2_ref[f0:f0 + f_chunk, :])
    y = x + 0.5 * acc
    if final:
        y = _rmsnorm(y, fg_ref[...])
    o_ref[...] = y


def _ffn(x2, g, w1, w2, fg, *, final, tm):
    t, d = x2.shape
    d_ff = w2.shape[0]
    f_chunk = 256 if d_ff % 256 == 0 else d_ff
    body = functools.partial(_ffn_body, d_ff=d_ff, f_chunk=f_chunk, final=final)
    return pl.pallas_call(
        body,
        grid=(t // tm,),
        in_specs=[pl.BlockSpec((tm, d), lambda i: (i, 0)),
                  _const_spec((1, d)), _const_spec(w1.shape), _const_spec(w2.shape),
                  _const_spec((1, d))],
        out_specs=pl.BlockSpec((tm, d), lambda i: (i, 0)),
        out_shape=jax.ShapeDtypeStruct((t, d), F32),
        compiler_params=_params("parallel"),
        name="ffn",
    )(x2, g.reshape(1, d), w1, w2, fg.reshape(1, d))


def _ab_in_body(x_ref, g_ref, wut_ref, wr_ref, wgk_ref, bgk_ref,
                ut_ref, q_ref, k_ref, v_ref, og_ref, gf_ref, gb_ref, *, hk, hv):
    h = _rmsnorm(x_ref[...], g_ref[...]).astype(BF16)
    ut_ref[0] = _dot_nt(wut_ref[...], h).astype(BF16)
    proj = _dot(h, wr_ref[...])
    q_ref[...] = proj[:, :hk] * (hk // GLA_HEADS) ** -0.5
    k_ref[...] = proj[:, hk:2 * hk]
    v_ref[...] = proj[:, 2 * hk:2 * hk + hv].astype(BF16)
    og_ref[...] = proj[:, 2 * hk + hv:2 * hk + 2 * hv].astype(BF16)
    glo = proj[:, 2 * hk + 2 * hv:].astype(BF16)
    gk = _log_sigmoid(_dot(glo, wgk_ref[...]) + bgk_ref[...]) * (1.0 / GLA_GATE_NORM)
    gf_ref[...] = gk[:, :hk]
    gb_ref[...] = gk[:, hk:]


def _ab_in(x2, g, wut, wr, wgk, bgk, *, b, l, tm, hk, hv):
    t, d = x2.shape
    su = wut.shape[0]
    ni = l // tm
    tok = lambda w: pl.BlockSpec((tm, w), lambda bi, i: (bi * ni + i, 0))
    body = functools.partial(_ab_in_body, hk=hk, hv=hv)
    return pl.pallas_call(
        body,
        grid=(b, ni),
        in_specs=[tok(d), _const_spec((1, d)), _const_spec(wut.shape), _const_spec(wr.shape),
                  _const_spec(wgk.shape), _const_spec(bgk.shape)],
        out_specs=[pl.BlockSpec((1, su, tm), lambda bi, i: (bi, 0, i)),
                   tok(hk), tok(hk), tok(hv), tok(hv), tok(hk), tok(hk)],
        out_shape=[jax.ShapeDtypeStruct((b, su, l), BF16),
                   jax.ShapeDtypeStruct((t, hk), F32), jax.ShapeDtypeStruct((t, hk), F32),
                   jax.ShapeDtypeStruct((t, hv), BF16), jax.ShapeDtypeStruct((t, hv), BF16),
                   jax.ShapeDtypeStruct((t, hk), F32), jax.ShapeDtypeStruct((t, hk), F32)],
        compiler_params=_params("parallel", "parallel"),
        name="ab_in",
    )(x2, g.reshape(1, d), wut, wr, wgk, bgk)


def _cpow_table(ar, ai, count):
    ks = jnp.arange(count)
    pr = jnp.ones(ar.shape + (count,), F32)
    pi = jnp.zeros(ar.shape + (count,), F32)
    sr, si = ar, ai
    for bit in range(max(1, (count - 1).bit_length())):
        on = ((ks >> bit) & 1) == 1
        nr = pr * sr[..., None] - pi * si[..., None]
        ni = pr * si[..., None] + pi * sr[..., None]
        pr = jnp.where(on, nr, pr)
        pi = jnp.where(on, ni, pi)
        sr, si = sr * sr - si * si, 2.0 * sr * si
    return pr, pi


def _s5_tables(lam_re, lam_im, b_re, b_im, c_re, c_im, log_dt, d_skip, n_chunks):
    ch = LANES
    g_, n_ = lam_re.shape[1], lam_re.shape[2]
    p_ = b_re.shape[-1]
    f32 = lambda a: a.astype(F32)
    lr = jnp.minimum(f32(lam_re), -1e-4)
    li = f32(lam_im)
    dt = jnp.exp(f32(log_dt))[..., None]
    mag = jnp.exp(lr * dt)
    ar = mag * jnp.cos(li * dt)
    ai = mag * jnp.sin(li * dt)
    den = lr * lr + li * li
    cr = ((ar - 1.0) * lr + ai * li) / den
    ci = (ai * lr - (ar - 1.0) * li) / den
    br, bi = f32(b_re), f32(b_im)
    bbr = cr[..., None] * br - ci[..., None] * bi
    bbi = cr[..., None] * bi + ci[..., None] * br
    ccr, cci = f32(c_re), f32(c_im)
    pwr, pwi = _cpow_table(ar, ai, ch + 1)

    cnr = ccr.transpose(0, 1, 3, 2)[..., None]
    cni = cci.transpose(0, 1, 3, 2)[..., None]
    cbr = cnr * bbr[:, :, :, None, :] - cni * bbi[:, :, :, None, :]
    cbi = cnr * bbi[:, :, :, None, :] + cni * bbr[:, :, :, None, :]
    taps = (jnp.einsum('dgnpq,dgnl->dgqpl', cbr, pwr[..., :ch], precision=HI)
            - jnp.einsum('dgnpq,dgnl->dgqpl', cbi, pwi[..., :ch], precision=HI))
    kf, kb = taps[0], taps[1]
    dmat = f32(d_skip).reshape(g_, p_)[:, None, :, None] * jnp.eye(p_, dtype=F32)[None, :, :, None]
    center = kf[..., 0:1] + kb[..., 0:1] + dmat
    kfull = jnp.concatenate([kb[..., :0:-1], center, kf[..., 1:]], axis=-1)
    kpad = jnp.pad(kfull, ((0, 0),) * 3 + ((0, 1),))

    def win(d, rev):
        pr = pwr[d][..., :ch]
        pi = pwi[d][..., :ch]
        if rev:
            pr, pi = pr[..., ::-1], pi[..., ::-1]
        pr = pr.transpose(0, 2, 1)[:, None]
        pi = pi.transpose(0, 2, 1)[:, None]
        xr = bbr[d].transpose(0, 2, 1)[:, :, None, :]
        xi = bbi[d].transpose(0, 2, 1)[:, :, None, :]
        return pr * xr - pi * xi, pr * xi + pi * xr
    fre, fim = win(0, True)
    bre, bim = win(1, False)
    w_in = jnp.concatenate([fre, fim, bre, bim], axis=-1).reshape(g_, p_ * ch, 4 * n_).astype(BF16)

    def wout(d, rev):
        pr = pwr[d][..., 1:ch + 1]
        pi = pwi[d][..., 1:ch + 1]
        if rev:
            pr, pi = pr[..., ::-1], pi[..., ::-1]
        pr = pr[:, :, None, :]
        pi = pi[:, :, None, :]
        xr = ccr[d].transpose(0, 2, 1)[..., None]
        xi = cci[d].transpose(0, 2, 1)[..., None]
        return xr * pr - xi * pi, -(xr * pi + xi * pr)
    fre, fim = wout(0, False)
    bre, bim = wout(1, True)
    wc = jnp.concatenate([fre, fim, bre, bim], axis=1).reshape(g_, 4 * n_, p_ * ch).astype(BF16)

    steps = max(1, (n_chunks - 1).bit_length())
    sr, si = pwr[..., ch], pwi[..., ch]
    rows = []
    for _ in range(steps):
        rows.append(jnp.stack([jnp.concatenate([sr, sr], -1), jnp.concatenate([-si, si], -1)], axis=2))
        sr, si = sr * sr - si * si, 2.0 * sr * si
    pq = jnp.stack(rows, axis=2)
    pq = pq.transpose(1, 0, 2, 3, 4)
    return kpad, w_in, wc, pq


def _s5_body(u_ref, taps_ref, win_ref, wc_ref, pq_ref, y_ref, toep_ref, *, nb, nc, p, n, steps):
    ch = LANES
    m = nb * nc
    for q in range(p):
        for pp in range(p):
            row = jnp.broadcast_to(taps_ref[0, q, pp:pp + 1, :], (ch, 2 * ch))
            blk = pltpu.roll(row, ch + 1, 1, stride=1, stride_axis=0)[:, :ch]
            toep_ref[q * ch:(q + 1) * ch, pp * ch:(pp + 1) * ch] = blk.astype(BF16)
    u = jnp.concatenate([u_ref[:, q].reshape(m, ch) for q in range(p)], axis=1)
    y = _dot(u, toep_ref[...])
    r = _dot(u, win_ref[0])
    c = lax.broadcasted_iota(jnp.int32, (m, 2 * n), 0) % nc

    def scan(x, d):
        for k in range(steps):
            s = 1 << k
            if d == 0:
                t = jnp.where(c >= s, pltpu.roll(x, s, axis=0), 0.0)
            else:
                t = jnp.where(c < nc - s, pltpu.roll(x, m - s, axis=0), 0.0)
            x = x + t * pq_ref[0, d, k, 0:1, :] + pltpu.roll(t, n, axis=1) * pq_ref[0, d, k, 1:2, :]
        if d == 0:
            return jnp.where(c >= 1, pltpu.roll(x, 1, axis=0), 0.0)
        return jnp.where(c < nc - 1, pltpu.roll(x, m - 1, axis=0), 0.0)

    xf = scan(r[:, :2 * n], 0)
    xb = scan(r[:, 2 * n:], 1)
    xc = jnp.concatenate([xf, xb], axis=1).astype(BF16)
    y = y + _dot(xc, wc_ref[0])
    for q in range(p):
        y_ref[:, q] = y[:, q * ch:(q + 1) * ch].reshape(nb, nc, ch)


def _s5(u4, taps, w_in, wc, pq):
    nb, su, nc, ch = u4.shape
    g_ = taps.shape[0]
    p = su // g_
    n = wc.shape[1] // 4
    steps = pq.shape[2]
    body = functools.partial(_s5_body, nb=nb, nc=nc, p=p, n=n, steps=steps)
    return pl.pallas_call(
        body,
        grid=(g_,),
        in_specs=[pl.BlockSpec((nb, p, nc, ch), lambda g: (0, g, 0, 0)),
                  pl.BlockSpec((1,) + taps.shape[1:], lambda g: (g, 0, 0, 0)),
                  pl.BlockSpec((1,) + w_in.shape[1:], lambda g: (g, 0, 0)),
                  pl.BlockSpec((1,) + wc.shape[1:], lambda g: (g, 0, 0)),
                  pl.BlockSpec((1,) + pq.shape[1:], lambda g: (g, 0, 0, 0, 0))],
        out_specs=pl.BlockSpec((nb, p, nc, ch), lambda g: (0, g, 0, 0)),
        out_shape=jax.ShapeDtypeStruct((nb, su, nc, ch), F32),
        scratch_shapes=[pltpu.VMEM((p * ch, p * ch), BF16)],
        compiler_params=_params("parallel"),
        name="s5",
    )(u4, taps, w_in, wc, pq)


def _gla_chunk(q, k, v, g, st_ref, tri, mask, last_row, dk, dv):
    g_hi = g.astype(BF16)
    g_lo = (g - g_hi.astype(F32)).astype(BF16)
    cum = _dot(tri, g_hi) + _dot(tri, g_lo)
    last = cum[last_row:last_row + 1, :]
    e = jnp.exp(cum)
    q_dec = (q * e).astype(BF16)
    k_inv = (k * jnp.exp(-cum)).astype(BF16)
    k_dec = (k * jnp.exp(last - cum)).astype(BF16)
    decay = jnp.exp(last)
    outs = []
    for h in range(GLA_HEADS):
        ks = slice(h * dk, (h + 1) * dk)
        vs = slice(h * dv, (h + 1) * dv)
        s = jnp.where(mask, _dot_nt(q_dec[:, ks], k_inv[:, ks]), 0.0)
        st = st_ref[h]
        o = _dot(s.astype(BF16), v[:, vs]) + _dot_nt(q_dec[:, ks], st.astype(BF16))
        st_ref[h] = decay[:, ks] * st + _dot_tn(v[:, vs], k_dec[:, ks])
        outs.append(o)
    return jnp.concatenate(outs, axis=1)


def _gla_body(qf_ref, kf_ref, vf_ref, gf_ref, qb_ref, kb_ref, vb_ref, gb_ref,
              of_ref, ob_ref, sf_ref, sb_ref, *, tb, dk, dv):
    c = GLA_CHUNK

    @pl.when(pl.program_id(1) == 0)
    def _():
        sf_ref[...] = jnp.zeros(sf_ref.shape, F32)
        sb_ref[...] = jnp.zeros(sb_ref.shape, F32)

    ii = lax.broadcasted_iota(jnp.int32, (c, c), 0)
    jj = lax.broadcasted_iota(jnp.int32, (c, c), 1)
    tri_f = (jj <= ii).astype(BF16)
    tri_b = (jj >= ii).astype(BF16)
    mask_f = jj <= ii
    mask_b = jj > ii
    for n in range(tb // c):
        rf = slice(n * c, (n + 1) * c)
        of_ref[rf, :] = _gla_chunk(qf_ref[rf, :], kf_ref[rf, :], vf_ref[rf, :], gf_ref[rf, :],
                                   sf_ref, tri_f, mask_f, c - 1, dk, dv)
        nr = tb // c - 1 - n
        rb = slice(nr * c, (nr + 1) * c)
        ob_ref[rb, :] = _gla_chunk(qb_ref[rb, :], kb_ref[rb, :], vb_ref[rb, :], gb_ref[rb, :],
                                   sb_ref, tri_b, mask_b, 0, dk, dv)


def _gla(q, k, v, gf, gb, *, b, l, tb):
    t, hk = q.shape
    hv = v.shape[1]
    dk, dv = hk // GLA_HEADS, hv // GLA_HEADS
    ni = l // tb
    fwd = lambda w: pl.BlockSpec((tb, w), lambda bi, i: (bi * ni + i, 0))
    bwd = lambda w: pl.BlockSpec((tb, w), lambda bi, i: (bi * ni + ni - 1 - i, 0))
    body = functools.partial(_gla_body, tb=tb, dk=dk, dv=dv)
    return pl.pallas_call(
        body,
        grid=(b, ni),
        in_specs=[fwd(hk), fwd(hk), fwd(hv), fwd(hk), bwd(hk), bwd(hk), bwd(hv), bwd(hk)],
        out_specs=[fwd(hv), bwd(hv)],
        out_shape=[jax.ShapeDtypeStruct((t, hv), F32), jax.ShapeDtypeStruct((t, hv), F32)],
        scratch_shapes=[pltpu.VMEM((GLA_HEADS, dv, dk), F32), pltpu.VMEM((GLA_HEADS, dv, dk), F32)],
        compiler_params=_params("parallel", "arbitrary"),
        name="gla",
    )(q, k, v, gf, q, k, v, gb)


def _head_norm(o, g, heads):
    dh = o.shape[1] // heads
    parts = []
    for h in range(heads):
        oh = o[:, h * dh:(h + 1) * dh]
        parts.append(oh * lax.rsqrt(jnp.mean(oh * oh, axis=-1, keepdims=True) + EPS))
    return jnp.concatenate(parts, axis=1) * g


def _ab_out_body(x_ref, yt_ref, wglut_ref, of_ref, ob_ref, og_ref, gn_ref, wo_ref, o_ref, *, su):
    gy = _gelu_tanh(yt_ref[0])
    z = _dot(wglut_ref[...], gy.astype(BF16))
    s5_out = (gy * _sigmoid(z)).T.astype(BF16)
    o = _head_norm(of_ref[...] + ob_ref[...], gn_ref[...], GLA_HEADS)
    gla_out = (o * _silu(og_ref[...].astype(F32))).astype(BF16)
    o_ref[...] = x_ref[...] + _dot(s5_out, wo_ref[:su, :]) + _dot(gla_out, wo_ref[su:, :])


def _ab_out(x2, yt, wglut, o_f, o_b, og, gn, wo, *, b, l, tm):
    t, d = x2.shape
    su = yt.shape[1]
    hv = o_f.shape[1]
    ni = l // tm
    tok = lambda w: pl.BlockSpec((tm, w), lambda bi, i: (bi * ni + i, 0))
    body = functools.partial(_ab_out_body, su=su)
    return pl.pallas_call(
        body,
        grid=(b, ni),
        in_specs=[tok(d), pl.BlockSpec((1, su, tm), lambda bi, i: (bi, 0, i)), _const_spec(wglut.shape),
                  tok(hv), tok(hv), tok(hv), _const_spec((1, hv)), _const_spec(wo.shape)],
        out_specs=tok(d),
        out_shape=jax.ShapeDtypeStruct((t, d), F32),
        compiler_params=_params("parallel", "parallel"),
        name="ab_out",
    )(x2, yt, wglut, o_f, o_b, og, gn.reshape(1, hv), wo)


def _ret_in_body(x_ref, g_ref, w_ref, cos_ref, sin_ref, q_ref, k_ref, v_ref, og_ref, *, hk, hv, dk):
    h = _rmsnorm(x_ref[...], g_ref[...]).astype(BF16)
    cos2 = cos_ref[...]
    sin2 = sin_ref[...]

    def rope(t):
        parts = []
        for hh in range(hk // dk):
            th = t[:, hh * dk:(hh + 1) * dk]
            parts.append(th * cos2 + pltpu.roll(th, dk // 2, axis=1) * sin2)
        return jnp.concatenate(parts, axis=1)

    q_ref[...] = rope(_dot(h, w_ref[:, :hk])).astype(BF16)
    k_ref[...] = (rope(_dot(h, w_ref[:, hk:2 * hk])) * dk ** -0.5).astype(BF16)
    v_ref[...] = _dot(h, w_ref[:, 2 * hk:2 * hk + hv]).astype(BF16)
    og_ref[...] = _dot(h, w_ref[:, 2 * hk + hv:]).astype(BF16)


def _ret_in(x2, g, w, cos2, sin2, *, b, l, tm, hk, hv):
    t, d = x2.shape
    dk = hk // RET_HEADS
    ni = l // tm
    tok = lambda w_: pl.BlockSpec((tm, w_), lambda bi, i: (bi * ni + i, 0))
    pos = pl.BlockSpec((tm, dk), lambda bi, i: (i, 0))
    body = functools.partial(_ret_in_body, hk=hk, hv=hv, dk=dk)
    return pl.pallas_call(
        body,
        grid=(b, ni),
        in_specs=[tok(d), _const_spec((1, d)), _const_spec(w.shape), pos, pos],
        out_specs=[tok(hk), tok(hk), tok(hv), tok(hv)],
        out_shape=[jax.ShapeDtypeStruct((t, hk), BF16), jax.ShapeDtypeStruct((t, hk), BF16),
                   jax.ShapeDtypeStruct((t, hv), BF16), jax.ShapeDtypeStruct((t, hv), BF16)],
        compiler_params=_params("parallel", "parallel"),
        name="ret_in",
    )(x2, g.reshape(1, d), w, cos2, sin2)


def _ret_tables(c, dk, dv):
    lg_f = jnp.log1p(-jnp.exp2(-5.0 - jnp.arange(RET_HEADS, dtype=F32)))
    lg_b = lg_f[::-1]
    idx = jnp.arange(c)
    diff = (idx[:, None] - idx[None, :]).astype(F32)
    dmat = jnp.where(diff >= 0, jnp.exp(jnp.maximum(diff, 0.0)[None] * lg_f[:, None, None]),
                     jnp.exp(jnp.maximum(-diff, 0.0)[None] * lg_b[:, None, None]))
    pos = idx.astype(F32)
    ones_k = jnp.ones((1, 1, dk), F32)
    ones_v = jnp.ones((1, 1, dv), F32)
    xi_f = jnp.exp((pos + 1.0)[None, :] * lg_f[:, None])[..., None] * ones_v
    xi_b = jnp.exp((c - pos)[None, :] * lg_b[:, None])[..., None] * ones_v
    zeta_f = jnp.exp((c - 1.0 - pos)[None, :] * lg_f[:, None])[..., None] * ones_k
    zeta_b = jnp.exp(pos[None, :] * lg_b[:, None])[..., None] * ones_k
    cd_f = jnp.exp(c * lg_f)[:, None, None] * ones_v
    cd_b = jnp.exp(c * lg_b)[:, None, None] * ones_v
    return dmat, xi_f, xi_b, zeta_f, zeta_b, cd_f, cd_b


def _ret_body(qf_ref, kf_ref, vf_ref, qb_ref, kb_ref, vb_ref,
              dmat_ref, xif_ref, xib_ref, zf_ref, zb_ref, cdf_ref, cdb_ref,
              of_ref, ob_ref, sf_ref, sb_ref, *, dk, dv):
    @pl.when(pl.program_id(1) == 0)
    def _():
        sf_ref[...] = jnp.zeros(sf_ref.shape, F32)
        sb_ref[...] = jnp.zeros(sb_ref.shape, F32)

    for h in range(RET_HEADS):
        ks = slice(h * dk, (h + 1) * dk)
        vs = slice(h * dv, (h + 1) * dv)
        q = qf_ref[:, ks]
        k = kf_ref[:, ks]
        v = vf_ref[:, vs]
        s = _dot_nt(q, k) * dmat_ref[h]
        st = sf_ref[h]
        of_ref[:, vs] = _dot(s.astype(BF16), v) + _dot(q, st.astype(BF16)) * xif_ref[h]
        kz = (k.astype(F32) * zf_ref[h]).astype(BF16)
        sf_ref[h] = cdf_ref[h] * st + _dot_tn(kz, v)
        q = qb_ref[:, ks]
        k = kb_ref[:, ks]
        v = vb_ref[:, vs]
        st = sb_ref[h]
        ob_ref[:, vs] = _dot(q, st.astype(BF16)) * xib_ref[h]
        kz = (k.astype(F32) * zb_ref[h]).astype(BF16)
        sb_ref[h] = cdb_ref[h] * st + _dot_tn(kz, v)


def _ret(q, k, v, *, b, l, tb):
    t, hk = q.shape
    hv = v.shape[1]
    dk, dv = hk // RET_HEADS, hv // RET_HEADS
    ni = l // tb
    tabs = _ret_tables(tb, dk, dv)
    fwd = lambda w: pl.BlockSpec((tb, w), lambda bi, i: (bi * ni + i, 0))
    bwd = lambda w: pl.BlockSpec((tb, w), lambda bi, i: (bi * ni + ni - 1 - i, 0))
    body = functools.partial(_ret_body, dk=dk, dv=dv)
    return pl.pallas_call(
        body,
        grid=(b, ni),
        in_specs=[fwd(hk), fwd(hk), fwd(hv), bwd(hk), bwd(hk), bwd(hv)] + [_const_spec(a.shape) for a in tabs],
        out_specs=[fwd(hv), bwd(hv)],
        out_shape=[jax.ShapeDtypeStruct((t, hv), F32), jax.ShapeDtypeStruct((t, hv), F32)],
        scratch_shapes=[pltpu.VMEM((RET_HEADS, dk, dv), F32), pltpu.VMEM((RET_HEADS, dk, dv), F32)],
        compiler_params=_params("parallel", "arbitrary"),
        name="ret",
    )(q, k, v, q, k, v, *tabs)


def _ret_out_body(x_ref, of_ref, ob_ref, og_ref, gn_ref, wo_ref, o_ref):
    o = _head_norm(of_ref[...] + ob_ref[...], gn_ref[...], RET_HEADS)
    gated = (o * _silu(og_ref[...].astype(F32))).astype(BF16)
    o_ref[...] = x_ref[...] + _dot(gated, wo_ref[...])


def _ret_out(x2, o_f, o_b, og, gn, wo, *, tm):
    t, d = x2.shape
    hv = o_f.shape[1]
    tok = lambda w: pl.BlockSpec((tm, w), lambda i: (i, 0))
    return pl.pallas_call(
        _ret_out_body,
        grid=(t // tm,),
        in_specs=[tok(d), tok(hv), tok(hv), tok(hv), _const_spec((1, hv)), _const_spec(wo.shape)],
        out_specs=tok(d),
        out_shape=jax.ShapeDtypeStruct((t, d), F32),
        compiler_params=_params("parallel"),
        name="ret_out",
    )(x2, o_f, o_b, og, gn.reshape(1, hv), wo)


def _tile(l, want):
    t = min(l, want)
    while l % t:
        t //= 2
    return t


def _s5_gla_layer(x2, mix_g, w_in, lam_re, lam_im, b_re, b_im, c_re, c_im, log_dt, d_skip, w_glu,
                  w_gk, b_gk, gla_norm, w_out, *, b, l):
    su = d_skip.shape[0]
    hk = w_gk.shape[-1]
    r = w_gk.shape[1]
    hv = gla_norm.shape[0]
    tm = _tile(l, 512)
    wut = w_in[:, :su].T.astype(BF16)
    wr = w_in[:, su:].astype(BF16)
    wgk = jnp.zeros((2 * r, 2 * hk), F32)
    wgk = wgk.at[:r, :hk].set(w_gk[0]).at[r:, hk:].set(w_gk[1]).astype(BF16)
    bgk = b_gk.reshape(1, 2 * hk).astype(F32)
    ut, q, k, v, og, gf, gb = _ab_in(x2, mix_g, wut, wr, wgk, bgk, b=b, l=l, tm=tm, hk=hk, hv=hv)

    nc = l // LANES
    taps, w_st, wc, pq = _s5_tables(lam_re, lam_im, b_re, b_im, c_re, c_im, log_dt, d_skip, nc)
    y4 = _s5(ut.reshape(b, su, nc, LANES), taps, w_st, wc, pq)
    yt = y4.reshape(b, su, l)

    o_f, o_b = _gla(q, k, v, gf, gb, b=b, l=l, tb=_tile(l, 512))
    return _ab_out(x2, yt, w_glu.T.astype(BF16), o_f, o_b, og, gla_norm, w_out.astype(BF16), b=b, l=l, tm=tm)


def _retention_layer(x2, mix_g, w_in, ret_norm, w_out, *, b, l):
    hv = ret_norm.shape[0]
    hk = (w_in.shape[1] - 2 * hv) // 2
    dk = hk // RET_HEADS
    half = dk // 2
    tm = _tile(l, 512)
    pos = jnp.arange(l, dtype=F32)
    inv = jnp.exp(-math.log(ROPE_BASE) * jnp.arange(half, dtype=F32) / half)
    ang = pos[:, None] * inv[None, :]
    cos, sin = jnp.cos(ang), jnp.sin(ang)
    cos2 = jnp.concatenate([cos, cos], axis=1)
    sin2 = jnp.concatenate([-sin, sin], axis=1)
    q, k, v, og = _ret_in(x2, mix_g, w_in.astype(BF16), cos2, sin2, b=b, l=l, tm=tm, hk=hk, hv=hv)
    o_f, o_b = _ret(q, k, v, b=b, l=l, tb=_tile(l, 256))
    return _ret_out(x2, o_f, o_b, og, ret_norm, w_out.astype(BF16), tm=tm)


def kernel(x, ffn1_norm, ffn1_w1, ffn1_w2, mix_norm, ffn2_norm, ffn2_w1, ffn2_w2, ab_w_in, s5_lambda_re, s5_lambda_im, s5_b_re, s5_b_im, s5_c_re, s5_c_im, s5_log_dt, s5_d, s5_w_glu, gla_w_gk, gla_b_gk, gla_norm, ab_w_out, ret_w_in, ret_norm, ret_w_out, final_norm):
    b, l, d = x.shape
    depth = ffn1_norm.shape[0]
    tm = _tile(b * l, 512)
    x2 = x.reshape(b * l, d).astype(F32)
    for i in range(depth):
        j = i // 2
        x2 = _ffn(x2, ffn1_norm[i], ffn1_w1[i].astype(BF16), ffn1_w2[i].astype(BF16), final_norm,
                  final=False, tm=tm)
        if i % 2 == 0:
            x2 = _s5_gla_layer(x2, mix_norm[i], ab_w_in[j], s5_lambda_re[j], s5_lambda_im[j], s5_b_re[j],
                               s5_b_im[j], s5_c_re[j], s5_c_im[j], s5_log_dt[j], s5_d[j], s5_w_glu[j],
                               gla_w_gk[j], gla_b_gk[j], gla_norm[j], ab_w_out[j], b=b, l=l)
        else:
            x2 = _retention_layer(x2, mix_norm[i], ret_w_in[j], ret_norm[j], ret_w_out[j], b=b, l=l)
        x2 = _ffn(x2, ffn2_norm[i], ffn2_w1[i].astype(BF16), ffn2_w2[i].astype(BF16), final_norm,
                  final=(i == depth - 1), tm=tm)
    return x2.reshape(b, l, d)
```

```python
import functools
import math

import jax
import jax.numpy as jnp
from jax import lax
from jax.experimental import pallas as pl
from jax.experimental.pallas import tpu as pltpu

F32 = jnp.float32
BF16 = jnp.bfloat16

EPS = 1e-6
S5_GROUP = 16
S5_STATE = 64
GLA_HEADS = 4
GLA_RANK = 16
GLA_GATE_NORM = 16.0
GLA_CHUNK = 64
RET_HEADS = 8
ROPE_BASE = 10000.0

LANES = 128
VMEM_LIMIT = 56 * 1024 * 1024
HI = lax.Precision.HIGHEST


def _const_spec(shape):
    nd = len(shape)
    return pl.BlockSpec(shape, lambda *_: (0,) * nd, pipeline_mode=pl.Buffered(1))


def _params(*sem):
    return pltpu.CompilerParams(dimension_semantics=sem, vmem_limit_bytes=VMEM_LIMIT)


def _rmsnorm(x, g):
    return x * lax.rsqrt(jnp.mean(x * x, axis=-1, keepdims=True) + EPS) * g


def _sigmoid(x):
    return 1.0 / (1.0 + jnp.exp(-x))


def _silu(x):
    return x * _sigmoid(x)


def _gelu_tanh(x):
    return 0.5 * x * (1.0 + jnp.tanh(math.sqrt(2.0 / math.pi) * (x + 0.044715 * (x * x * x))))


def _log_sigmoid(x):
    return jnp.minimum(x, 0.0) - jnp.log1p(jnp.exp(-jnp.abs(x)))


def _dot(a, b):
    return jnp.dot(a, b, preferred_element_type=F32)


def _dot_nt(a, b):
    return lax.dot_general(a, b, (((1,), (1,)), ((), ())), preferred_element_type=F32)


def _dot_tn(a, b):
    return lax.dot_general(a, b, (((0,), (0,)), ((), ())), preferred_element_type=F32)


def _ffn_body(x_ref, g_ref, w1_ref, w2_ref, fg_ref, o_ref, *, d_ff, f_chunk, final):
    x = x_ref[...]
    xn = _rmsnorm(x, g_ref[...]).astype(BF16)
    acc = jnp.zeros(x.shape, F32)
    for f0 in range(0, d_ff, f_chunk):
        gate = _dot(xn, w1_ref[:, f0:f0 + f_chunk])
        up = _dot(xn, w1_ref[:, d_ff + f0:d_ff + f0 + f_chunk])
        act = (_silu(gate) * up).astype(BF16)
        acc = acc + _dot(act, w2_ref[f0:f0 + f_chunk, :])
    y = x + 0.5 * acc
    if final:
        y = _rmsnorm(y, fg_ref[...])
    o_ref[...] = y


def _ffn(x2, g, w1, w2, fg, *, final, tm, f_chunk):
    t, d = x2.shape
    d_ff = w2.shape[0]
    body = functools.partial(_ffn_body, d_ff=d_ff, f_chunk=f_chunk, final=final)
    return pl.pallas_call(
        body,
        grid=(t // tm,),
        in_specs=[pl.BlockSpec((tm, d), lambda i: (i, 0)),
                  _const_spec((1, d)), _const_spec(w1.shape), _const_spec(w2.shape),
                  _const_spec((1, d))],
        out_specs=pl.BlockSpec((tm, d), lambda i: (i, 0)),
        out_shape=jax.ShapeDtypeStruct((t, d), F32),
        compiler_params=_params("parallel"),
        name="ffn",
    )(x2, g.reshape(1, d), w1, w2, fg.reshape(1, d))


def _ab_in_body(x_ref, g_ref, wut_ref, wr_ref, wgk_ref, bgk_ref,
                ut_ref, q_ref, k_ref, v_ref, og_ref, gf_ref, gb_ref, *, hk, hv):
    h = _rmsnorm(x_ref[...], g_ref[...]).astype(BF16)
    ut_ref[0] = _dot_nt(wut_ref[...], h).astype(BF16)
    proj = _dot(h, wr_ref[...])
    q_ref[...] = proj[:, :hk] * (hk // GLA_HEADS) ** -0.5
    k_ref[...] = proj[:, hk:2 * hk]
    v_ref[...] = proj[:, 2 * hk:2 * hk + hv].astype(BF16)
    og_ref[...] = proj[:, 2 * hk + hv:2 * hk + 2 * hv].astype(BF16)
    glo = proj[:, 2 * hk + 2 * hv:].astype(BF16)
    gk = _log_sigmoid(_dot(glo, wgk_ref[...]) + bgk_ref[...]) * (1.0 / GLA_GATE_NORM)
    gf_ref[...] = gk[:, :hk]
    gb_ref[...] = gk[:, hk:]


def _ab_in(x2, g, wut, wr, wgk, bgk, *, b, l, tm, hk, hv):
    t, d = x2.shape
    su = wut.shape[0]
    ni = l // tm
    tok = lambda w: pl.BlockSpec((tm, w), lambda bi, i: (bi * ni + i, 0))
    body = functools.partial(_ab_in_body, hk=hk, hv=hv)
    return pl.pallas_call(
        body,
        grid=(b, ni),
        in_specs=[tok(d), _const_spec((1, d)), _const_spec(wut.shape), _const_spec(wr.shape),
                  _const_spec(wgk.shape), _const_spec(bgk.shape)],
        out_specs=[pl.BlockSpec((1, su, tm), lambda bi, i: (bi, 0, i)),
                   tok(hk), tok(hk), tok(hv), tok(hv), tok(hk), tok(hk)],
        out_shape=[jax.ShapeDtypeStruct((b, su, l), BF16),
                   jax.ShapeDtypeStruct((t, hk), F32), jax.ShapeDtypeStruct((t, hk), F32),
                   jax.ShapeDtypeStruct((t, hv), BF16), jax.ShapeDtypeStruct((t, hv), BF16),
                   jax.ShapeDtypeStruct((t, hk), F32), jax.ShapeDtypeStruct((t, hk), F32)],
        compiler_params=_params("parallel", "parallel"),
        name="ab_in",
    )(x2, g.reshape(1, d), wut, wr, wgk, bgk)


def _cpow_table(ar, ai, count):
    ks = jnp.arange(count)
    pr = jnp.ones(ar.shape + (count,), F32)
    pi = jnp.zeros(ar.shape + (count,), F32)
    sr, si = ar, ai
    for bit in range(max(1, (count - 1).bit_length())):
        on = ((ks >> bit) & 1) == 1
        nr = pr * sr[..., None] - pi * si[..., None]
        ni = pr * si[..., None] + pi * sr[..., None]
        pr = jnp.where(on, nr, pr)
        pi = jnp.where(on, ni, pi)
        sr, si = sr * sr - si * si, 2.0 * sr * si
    return pr, pi


def _s5_tables(lam_re, lam_im, b_re, b_im, c_re, c_im, log_dt, d_skip, n_chunks):
    ch = LANES
    g_, n_ = lam_re.shape[1], lam_re.shape[2]
    p_ = b_re.shape[-1]
    f32 = lambda a: a.astype(F32)
    lr = jnp.minimum(f32(lam_re), -1e-4)
    li = f32(lam_im)
    dt = jnp.exp(f32(log_dt))[..., None]
    mag = jnp.exp(lr * dt)
    ar = mag * jnp.cos(li * dt)
    ai = mag * jnp.sin(li * dt)
    den = lr * lr + li * li
    cr = ((ar - 1.0) * lr + ai * li) / den
    ci = (ai * lr - (ar - 1.0) * li) / den
    br, bi = f32(b_re), f32(b_im)
    bbr = cr[..., None] * br - ci[..., None] * bi
    bbi = cr[..., None] * bi + ci[..., None] * br
    ccr, cci = f32(c_re), f32(c_im)
    pwr, pwi = _cpow_table(ar, ai, ch + 1)

    cnr = ccr.transpose(0, 1, 3, 2)[..., None]
    cni = cci.transpose(0, 1, 3, 2)[..., None]
    cbr = cnr * bbr[:, :, :, None, :] - cni * bbi[:, :, :, None, :]
    cbi = cnr * bbi[:, :, :, None, :] + cni * bbr[:, :, :, None, :]
    taps = (jnp.einsum('dgnpq,dgnl->dgqpl', cbr, pwr[..., :ch], precision=HI)
            - jnp.einsum('dgnpq,dgnl->dgqpl', cbi, pwi[..., :ch], precision=HI))
    kf, kb = taps[0], taps[1]
    dmat = f32(d_skip).reshape(g_, p_)[:, None, :, None] * jnp.eye(p_, dtype=F32)[None, :, :, None]
    center = kf[..., 0:1] + kb[..., 0:1] + dmat
    kfull = jnp.concatenate([kb[..., :0:-1], center, kf[..., 1:]], axis=-1)
    kpad = jnp.pad(kfull, ((0, 0),) * 3 + ((0, 1),))

    def win(d, rev):
        pr = pwr[d][..., :ch]
        pi = pwi[d][..., :ch]
        if rev:
            pr, pi = pr[..., ::-1], pi[..., ::-1]
        pr = pr.transpose(0, 2, 1)[:, None]
        pi = pi.transpose(0, 2, 1)[:, None]
        xr = bbr[d].transpose(0, 2, 1)[:, :, None, :]
        xi = bbi[d].transpose(0, 2, 1)[:, :, None, :]
        return pr * xr - pi * xi, pr * xi + pi * xr
    fre, fim = win(0, True)
    bre, bim = win(1, False)
    w_in = jnp.concatenate([fre, fim, bre, bim], axis=-1).reshape(g_, p_ * ch, 4 * n_).astype(BF16)

    def wout(d, rev):
        pr = pwr[d][..., 1:ch + 1]
        pi = pwi[d][..., 1:ch + 1]
        if rev:
            pr, pi = pr[..., ::-1], pi[..., ::-1]
        pr = pr[:, :, None, :]
        pi = pi[:, :, None, :]
        xr = ccr[d].transpose(0, 2, 1)[..., None]
        xi = cci[d].transpose(0, 2, 1)[..., None]
        return xr * pr - xi * pi, -(xr * pi + xi * pr)
    fre, fim = wout(0, False)
    bre, bim = wout(1, True)
    wc = jnp.concatenate([fre, fim, bre, bim], axis=1).reshape(g_, 4 * n_, p_ * ch).astype(BF16)

    steps = max(1, (n_chunks - 1).bit_length())
    sr, si = pwr[..., ch], pwi[..., ch]
    rows = []
    for _ in range(steps):
        rows.append(jnp.stack([jnp.concatenate([sr, sr], -1), jnp.concatenate([-si, si], -1)], axis=2))
        sr, si = sr * sr - si * si, 2.0 * sr * si
    pq = jnp.stack(rows, axis=2)
    pq = pq.transpose(1, 0, 2, 3, 4)
    return kpad, w_in, wc, pq


def _s5_body(u_ref, taps_ref, win_ref, wc_ref, pq_ref, y_ref, toep_ref, *, nb, nc, p, n, steps):
    ch = LANES
    m = nb * nc
    for q in range(p):
        for pp in range(p):
            row = jnp.broadcast_to(taps_ref[0, q, pp:pp + 1, :], (ch, 2 * ch))
            blk = pltpu.roll(row, ch + 1, 1, stride=1, stride_axis=0)[:, :ch]
            toep_ref[q * ch:(q + 1) * ch, pp * ch:(pp + 1) * ch] = blk.astype(BF16)
    u = jnp.concatenate([u_ref[:, q].reshape(m, ch) for q in range(p)], axis=1)
    y = _dot(u, toep_ref[...])
    r = _dot(u, win_ref[0])
    c = lax.broadcasted_iota(jnp.int32, (m, 2 * n), 0) % nc

    def scan(x, d):
        for k in range(steps):
            s = 1 << k
            if d == 0:
                t = jnp.where(c >= s, pltpu.roll(x, s, axis=0), 0.0)
            else:
                t = jnp.where(c < nc - s, pltpu.roll(x, m - s, axis=0), 0.0)
            x = x + t * pq_ref[0, d, k, 0:1, :] + pltpu.roll(t, n, axis=1) * pq_ref[0, d, k, 1:2, :]
        if d == 0:
            return jnp.where(c >= 1, pltpu.roll(x, 1, axis=0), 0.0)
        return jnp.where(c < nc - 1, pltpu.roll(x, m - 1, axis=0), 0.0)

    xf = scan(r[:, :2 * n], 0)
    xb = scan(r[:, 2 * n:], 1)
    xc = jnp.concatenate([xf, xb], axis=1).astype(BF16)
    y = y + _dot(xc, wc_ref[0])
    for q in range(p):
        y_ref[:, q] = y[:, q * ch:(q + 1) * ch].reshape(nb, nc, ch)


def _s5(u4, taps, w_in, wc, pq):
    nb, su, nc, ch = u4.shape
    g_ = taps.shape[0]
    p = su // g_
    n = wc.shape[1] // 4
    steps = pq.shape[2]
    body = functools.partial(_s5_body, nb=nb, nc=nc, p=p, n=n, steps=steps)
    return pl.pallas_call(
        body,
        grid=(g_,),
        in_specs=[pl.BlockSpec((nb, p, nc, ch), lambda g: (0, g, 0, 0)),
                  pl.BlockSpec((1,) + taps.shape[1:], lambda g: (g, 0, 0, 0)),
                  pl.BlockSpec((1,) + w_in.shape[1:], lambda g: (g, 0, 0)),
                  pl.BlockSpec((1,) + wc.shape[1:], lambda g: (g, 0, 0)),
                  pl.BlockSpec((1,) + pq.shape[1:], lambda g: (g, 0, 0, 0, 0))],
        out_specs=pl.BlockSpec((nb, p, nc, ch), lambda g: (0, g, 0, 0)),
        out_shape=jax.ShapeDtypeStruct((nb, su, nc, ch), F32),
        scratch_shapes=[pltpu.VMEM((p * ch, p * ch), BF16)],
        compiler_params=_params("parallel"),
        name="s5",
    )(u4, taps, w_in, wc, pq)


def _gla_chunk(q, k, v, g, st_ref, tri, mask, last_row, dk, dv):
    g_hi = g.astype(BF16)
    g_lo = (g - g_hi.astype(F32)).astype(BF16)
    cum = _dot(tri, g_hi) + _dot(tri, g_lo)
    last = cum[last_row:last_row + 1, :]
    e = jnp.exp(cum)
    q_dec = (q * e).astype(BF16)
    k_inv = (k * jnp.exp(-cum)).astype(BF16)
    k_dec = (k * jnp.exp(last - cum)).astype(BF16)
    decay = jnp.exp(last)
    outs = []
    for h in range(GLA_HEADS):
        ks = slice(h * dk, (h + 1) * dk)
        vs = slice(h * dv, (h + 1) * dv)
        s = jnp.where(mask, _dot_nt(q_dec[:, ks], k_inv[:, ks]), 0.0)
        st = st_ref[h]
        o = _dot(s.astype(BF16), v[:, vs]) + _dot_nt(q_dec[:, ks], st.astype(BF16))
        st_ref[h] = decay[:, ks] * st + _dot_tn(v[:, vs], k_dec[:, ks])
        outs.append(o)
    return jnp.concatenate(outs, axis=1)


def _gla_body(qf_ref, kf_ref, vf_ref, gf_ref, qb_ref, kb_ref, vb_ref, gb_ref,
              of_ref, ob_ref, sf_ref, sb_ref, *, tb, dk, dv):
    c = GLA_CHUNK

    @pl.when(pl.program_id(1) == 0)
    def _():
        sf_ref[...] = jnp.zeros(sf_ref.shape, F32)
        sb_ref[...] = jnp.zeros(sb_ref.shape, F32)

    ii = lax.broadcasted_iota(jnp.int32, (c, c), 0)
    jj = lax.broadcasted_iota(jnp.int32, (c, c), 1)
    tri_f = (jj <= ii).astype(BF16)
    tri_b = (jj >= ii).astype(BF16)
    mask_f = jj <= ii
    mask_b = jj > ii
    for n in range(tb // c):
        rf = slice(n * c, (n + 1) * c)
        of_ref[rf, :] = _gla_chunk(qf_ref[rf, :], kf_ref[rf, :], vf_ref[rf, :], gf_ref[rf, :],
                                   sf_ref, tri_f, mask_f, c - 1, dk, dv)
        nr = tb // c - 1 - n
        rb = slice(nr * c, (nr + 1) * c)
        ob_ref[rb, :] = _gla_chunk(qb_ref[rb, :], kb_ref[rb, :], vb_ref[rb, :], gb_ref[rb, :],
                                   sb_ref, tri_b, mask_b, 0, dk, dv)


def _gla(q, k, v, gf, gb, *, b, l, tb):
    t, hk = q.shape
    hv = v.shape[1]
    dk, dv = hk // GLA_HEADS, hv // GLA_HEADS
    ni = l // tb
    fwd = lambda w: pl.BlockSpec((tb, w), lambda bi, i: (bi * ni + i, 0))
    bwd = lambda w: pl.BlockSpec((tb, w), lambda bi, i: (bi * ni + ni - 1 - i, 0))
    body = functools.partial(_gla_body, tb=tb, dk=dk, dv=dv)
    return pl.pallas_call(
        body,
        grid=(b, ni),
        in_specs=[fwd(hk), fwd(hk), fwd(hv), fwd(hk), bwd(hk), bwd(hk), bwd(hv), bwd(hk)],
        out_specs=[fwd(hv), bwd(hv)],
        out_shape=[jax.ShapeDtypeStruct((t, hv), F32), jax.ShapeDtypeStruct((t, hv), F32)],
        scratch_shapes=[pltpu.VMEM((GLA_HEADS, dv, dk), F32), pltpu.VMEM((GLA_HEADS, dv, dk), F32)],
        compiler_params=_params("parallel", "arbitrary"),
        name="gla",
    )(q, k, v, gf, q, k, v, gb)


def _head_norm(o, g, heads):
    dh = o.shape[1] // heads
    parts = []
    for h in range(heads):
        oh = o[:, h * dh:(h + 1) * dh]
        parts.append(oh * lax.rsqrt(jnp.mean(oh * oh, axis=-1, keepdims=True) + EPS))
    return jnp.concatenate(parts, axis=1) * g


def _ab_out_body(x_ref, yt_ref, wglut_ref, of_ref, ob_ref, og_ref, gn_ref, wo_ref, o_ref, *, su):
    gy = _gelu_tanh(yt_ref[0])
    z = _dot(wglut_ref[...], gy.astype(BF16))
    s5_out = (gy * _sigmoid(z)).T.astype(BF16)
    o = _head_norm(of_ref[...] + ob_ref[...], gn_ref[...], GLA_HEADS)
    gla_out = (o * _silu(og_ref[...].astype(F32))).astype(BF16)
    o_ref[...] = x_ref[...] + _dot(s5_out, wo_ref[:su, :]) + _dot(gla_out, wo_ref[su:, :])


def _ab_out(x2, yt, wglut, o_f, o_b, og, gn, wo, *, b, l, tm):
    t, d = x2.shape
    su = yt.shape[1]
    hv = o_f.shape[1]
    ni = l // tm
    tok = lambda w: pl.BlockSpec((tm, w), lambda bi, i: (bi * ni + i, 0))
    body = functools.partial(_ab_out_body, su=su)
    return pl.pallas_call(
        body,
        grid=(b, ni),
        in_specs=[tok(d), pl.BlockSpec((1, su, tm), lambda bi, i: (bi, 0, i)), _const_spec(wglut.shape),
                  tok(hv), tok(hv), tok(hv), _const_spec((1, hv)), _const_spec(wo.shape)],
        out_specs=tok(d),
        out_shape=jax.ShapeDtypeStruct((t, d), F32),
        compiler_params=_params("parallel", "parallel"),
        name="ab_out",
    )(x2, yt, wglut, o_f, o_b, og, gn.reshape(1, hv), wo)


def _ret_in_body(x_ref, g_ref, w_ref, cos_ref, sin_ref, q_ref, k_ref, v_ref, og_ref, *, hk, hv, dk):
    h = _rmsnorm(x_ref[...], g_ref[...]).astype(BF16)
    cos2 = cos_ref[...]
    sin2 = sin_ref[...]

    def rope(t):
        parts = []
        for hh in range(hk // dk):
            th = t[:, hh * dk:(hh + 1) * dk]
            parts.append(th * cos2 + pltpu.roll(th, dk // 2, axis=1) * sin2)
        return jnp.concatenate(parts, axis=1)

    q_ref[...] = rope(_dot(h, w_ref[:, :hk])).astype(BF16)
    k_ref[...] = (rope(_dot(h, w_ref[:, hk:2 * hk])) * dk ** -0.5).astype(BF16)
    v_ref[...] = _dot(h, w_ref[:, 2 * hk:2 * hk + hv]).astype(BF16)
    og_ref[...] = _dot(h, w_ref[:, 2 * hk + hv:]).astype(BF16)


def _ret_in(x2, g, w, cos2, sin2, *, b, l, tm, hk, hv):
    t, d = x2.shape
    dk = hk // RET_HEADS
    ni = l // tm
    tok = lambda w_: pl.BlockSpec((tm, w_), lambda bi, i: (bi * ni + i, 0))
    pos = pl.BlockSpec((tm, dk), lambda bi, i: (i, 0))
    body = functools.partial(_ret_in_body, hk=hk, hv=hv, dk=dk)
    return pl.pallas_call(
        body,
        grid=(b, ni),
        in_specs=[tok(d), _const_spec((1, d)), _const_spec(w.shape), pos, pos],
        out_specs=[tok(hk), tok(hk), tok(hv), tok(hv)],
        out_shape=[jax.ShapeDtypeStruct((t, hk), BF16), jax.ShapeDtypeStruct((t, hk), BF16),
                   jax.ShapeDtypeStruct((t, hv), BF16), jax.ShapeDtypeStruct((t, hv), BF16)],
        compiler_params=_params("parallel", "parallel"),
        name="ret_in",
    )(x2, g.reshape(1, d), w, cos2, sin2)


def _ret_tables(c, dk, dv):
    lg_f = jnp.log1p(-jnp.exp2(-5.0 - jnp.arange(RET_HEADS, dtype=F32)))
    lg_b = lg_f[::-1]
    idx = jnp.arange(c)
    diff = (idx[:, None] - idx[None, :]).astype(F32)
    dmat = jnp.where(diff >= 0, jnp.exp(jnp.maximum(diff, 0.0)[None] * lg_f[:, None, None]),
                     jnp.exp(jnp.maximum(-diff, 0.0)[None] * lg_b[:, None, None]))
    pos = idx.astype(F32)
    ones_k = jnp.ones((1, 1, dk), F32)
    ones_v = jnp.ones((1, 1, dv), F32)
    xi_f = jnp.exp((pos + 1.0)[None, :] * lg_f[:, None])[..., None] * ones_v
    xi_b = jnp.exp((c - pos)[None, :] * lg_b[:, None])[..., None] * ones_v
    zeta_f = jnp.exp((c - 1.0 - pos)[None, :] * lg_f[:, None])[..., None] * ones_k
    zeta_b = jnp.exp(pos[None, :] * lg_b[:, None])[..., None] * ones_k
    cd_f = jnp.exp(c * lg_f)[:, None, None] * ones_v
    cd_b = jnp.exp(c * lg_b)[:, None, None] * ones_v
    return dmat, xi_f, xi_b, zeta_f, zeta_b, cd_f, cd_b


def _ret_state_body(k_ref, v_ref, zb_ref, cdb_ref, sb_out_ref, sb_ref, *, dk, dv):
    @pl.when(pl.program_id(1) == 0)
    def _():
        sb_ref[...] = jnp.zeros(sb_ref.shape, F32)

    for h in range(RET_HEADS):
        st = sb_ref[h]
        sb_out_ref[0, 0, h] = st.astype(BF16)
        kz = (k_ref[:, h * dk:(h + 1) * dk].astype(F32) * zb_ref[h]).astype(BF16)
        sb_ref[h] = cdb_ref[h] * st + _dot_tn(kz, v_ref[:, h * dv:(h + 1) * dv])


def _ret_state(k, v, zeta_b, cd_b, *, b, l, tb):
    hk, hv = k.shape[1], v.shape[1]
    dk, dv = hk // RET_HEADS, hv // RET_HEADS
    ni = l // tb
    bwd = lambda w: pl.BlockSpec((tb, w), lambda bi, i: (bi * ni + ni - 1 - i, 0))
    body = functools.partial(_ret_state_body, dk=dk, dv=dv)
    return pl.pallas_call(
        body,
        grid=(b, ni),
        in_specs=[bwd(hk), bwd(hv), _const_spec(zeta_b.shape), _const_spec(cd_b.shape)],
        out_specs=pl.BlockSpec((1, 1, RET_HEADS, dk, dv), lambda bi, i: (bi, ni - 1 - i, 0, 0, 0)),
        out_shape=jax.ShapeDtypeStruct((b, ni, RET_HEADS, dk, dv), BF16),
        scratch_shapes=[pltpu.VMEM((RET_HEADS, dk, dv), F32)],
        compiler_params=_params("parallel", "arbitrary"),
        name="ret_state",
    )(k, v, zeta_b, cd_b)


def _ret_main_body(x_ref, q_ref, k_ref, v_ref, og_ref, sb_ref, dmat_ref, xif_ref, xib_ref, zf_ref, cdf_ref,
                   gn_ref, wo_ref, o_ref, sf_ref, gated_ref, *, dk, dv):
    @pl.when(pl.program_id(1) == 0)
    def _():
        sf_ref[...] = jnp.zeros(sf_ref.shape, F32)

    for h in range(RET_HEADS):
        ks = slice(h * dk, (h + 1) * dk)
        vs = slice(h * dv, (h + 1) * dv)
        q = q_ref[:, ks]
        k = k_ref[:, ks]
        v = v_ref[:, vs]
        s = _dot_nt(q, k) * dmat_ref[h]
        st = sf_ref[h]
        o = (_dot(s.astype(BF16), v) + _dot(q, st.astype(BF16)) * xif_ref[h]
             + _dot(q, sb_ref[0, 0, h]) * xib_ref[h])
        kz = (k.astype(F32) * zf_ref[h]).astype(BF16)
        sf_ref[h] = cdf_ref[h] * st + _dot_tn(kz, v)
        o = o * lax.rsqrt(jnp.mean(o * o, axis=-1, keepdims=True) + EPS) * gn_ref[:, vs]
        gated_ref[:, vs] = (o * _silu(og_ref[:, vs].astype(F32))).astype(BF16)
    o_ref[...] = x_ref[...] + _dot(gated_ref[...], wo_ref[...])


def _ret_main(x2, q, k, v, og, sb, tabs, gn, wo, *, b, l, tb):
    t, d = x2.shape
    hk, hv = q.shape[1], v.shape[1]
    dk, dv = hk // RET_HEADS, hv // RET_HEADS
    ni = l // tb
    tok = lambda w: pl.BlockSpec((tb, w), lambda bi, i: (bi * ni + i, 0))
    body = functools.partial(_ret_main_body, dk=dk, dv=dv)
    return pl.pallas_call(
        body,
        grid=(b, ni),
        in_specs=[tok(d), tok(hk), tok(hk), tok(hv), tok(hv),
                  pl.BlockSpec((1, 1, RET_HEADS, dk, dv), lambda bi, i: (bi, i, 0, 0, 0))]
                 + [_const_spec(a.shape) for a in tabs] + [_const_spec((1, hv)), _const_spec(wo.shape)],
        out_specs=tok(d),
        out_shape=jax.ShapeDtypeStruct((t, d), F32),
        scratch_shapes=[pltpu.VMEM((RET_HEADS, dk, dv), F32), pltpu.VMEM((tb, hv), BF16)],
        compiler_params=_params("parallel", "arbitrary"),
        name="ret_main",
    )(x2, q, k, v, og, sb, *tabs, gn.reshape(1, hv), wo)


def _tile(l, want):
    t = min(l, want)
    while l % t:
        t //= 2
    return t


def _s5_gla_layer(x2, mix_g, w_in, lam_re, lam_im, b_re, b_im, c_re, c_im, log_dt, d_skip, w_glu,
                  w_gk, b_gk, gla_norm, w_out, *, b, l):
    su = d_skip.shape[0]
    hk = w_gk.shape[-1]
    r = w_gk.shape[1]
    hv = gla_norm.shape[0]
    tm = _tile(l, 512)
    wut = w_in[:, :su].T.astype(BF16)
    wr = w_in[:, su:].astype(BF16)
    wgk = jnp.zeros((2 * r, 2 * hk), F32)
    wgk = wgk.at[:r, :hk].set(w_gk[0]).at[r:, hk:].set(w_gk[1]).astype(BF16)
    bgk = b_gk.reshape(1, 2 * hk).astype(F32)
    ut, q, k, v, og, gf, gb = _ab_in(x2, mix_g, wut, wr, wgk, bgk, b=b, l=l, tm=tm, hk=hk, hv=hv)

    nc = l // LANES
    taps, w_st, wc, pq = _s5_tables(lam_re, lam_im, b_re, b_im, c_re, c_im, log_dt, d_skip, nc)
    y4 = _s5(ut.reshape(b, su, nc, LANES), taps, w_st, wc, pq)
    yt = y4.reshape(b, su, l)

    o_f, o_b = _gla(q, k, v, gf, gb, b=b, l=l, tb=_tile(l, 512))
    return _ab_out(x2, yt, w_glu.T.astype(BF16), o_f, o_b, og, gla_norm, w_out.astype(BF16), b=b, l=l, tm=tm)


def _retention_layer(x2, mix_g, w_in, ret_norm, w_out, *, b, l):
    hv = ret_norm.shape[0]
    hk = (w_in.shape[1] - 2 * hv) // 2
    dk = hk // RET_HEADS
    half = dk // 2
    tm = _tile(l, 512)
    pos = jnp.arange(l, dtype=F32)
    inv = jnp.exp(-math.log(ROPE_BASE) * jnp.arange(half, dtype=F32) / half)
    ang = pos[:, None] * inv[None, :]
    cos, sin = jnp.cos(ang), jnp.sin(ang)
    cos2 = jnp.concatenate([cos, cos], axis=1)
    sin2 = jnp.concatenate([-sin, sin], axis=1)
    q, k, v, og = _ret_in(x2, mix_g, w_in.astype(BF16), cos2, sin2, b=b, l=l, tm=tm, hk=hk, hv=hv)
    tb = _tile(l, 256)
    dmat, xi_f, xi_b, zeta_f, zeta_b, cd_f, cd_b = _ret_tables(tb, dk, hv // RET_HEADS)
    sb = _ret_state(k, v, zeta_b, cd_b, b=b, l=l, tb=tb)
    return _ret_main(x2, q, k, v, og, sb, (dmat, xi_f, xi_b, zeta_f, cd_f), ret_norm, w_out.astype(BF16),
                     b=b, l=l, tb=tb)


def kernel(x, ffn1_norm, ffn1_w1, ffn1_w2, mix_norm, ffn2_norm, ffn2_w1, ffn2_w2, ab_w_in, s5_lambda_re, s5_lambda_im, s5_b_re, s5_b_im, s5_c_re, s5_c_im, s5_log_dt, s5_d, s5_w_glu, gla_w_gk, gla_b_gk, gla_norm, ab_w_out, ret_w_in, ret_norm, ret_w_out, final_norm):
    b, l, d = x.shape
    depth = ffn1_norm.shape[0]
    d_ff = ffn1_w2.shape[1]
    x2 = x.reshape(b * l, d).astype(F32)
    for i in range(depth):
        j = i // 2
        x2 = _ffn(x2, ffn1_norm[i], ffn1_w1[i].astype(BF16), ffn1_w2[i].astype(BF16), final_norm,
                  final=False, tm=_tile(b * l, 512), f_chunk=256 if i == 0 else d_ff)
        if i % 2 == 0:
            x2 = _s5_gla_layer(x2, mix_norm[i], ab_w_in[j], s5_lambda_re[j], s5_lambda_im[j], s5_b_re[j],
                               s5_b_im[j], s5_c_re[j], s5_c_im[j], s5_log_dt[j], s5_d[j], s5_w_glu[j],
                               gla_w_gk[j], gla_b_gk[j], gla_norm[j], ab_w_out[j], b=b, l=l)
        else:
            x2 = _retention_layer(x2, mix_norm[i], ret_w_in[j], ret_norm[j], ret_w_out[j], b=b, l=l)
        x2 = _ffn(x2, ffn2_norm[i], ffn2_w1[i].astype(BF16), ffn2_w2[i].astype(BF16), final_norm,
                  final=(i == depth - 1), tm=_tile(b * l, 1024), f_chunk=256 if i == 0 else d_ff)
    return x2.reshape(b, l, d)
```

```python
import functools
import math

import jax
import jax.numpy as jnp
from jax import lax
from jax.experimental import pallas as pl
from jax.experimental.pallas import tpu as pltpu

F32 = jnp.float32
BF16 = jnp.bfloat16

EPS = 1e-6
S5_GROUP = 16
S5_STATE = 64
GLA_HEADS = 4
GLA_RANK = 16
GLA_GATE_NORM = 16.0
GLA_CHUNK = 64
RET_HEADS = 8
ROPE_BASE = 10000.0

LANES = 128
VMEM_LIMIT = 56 * 1024 * 1024
HI = lax.Precision.HIGHEST


def _const_spec(shape):
    nd = len(shape)
    return pl.BlockSpec(shape, lambda *_: (0,) * nd, pipeline_mode=pl.Buffered(1))


def _params(*sem):
    return pltpu.CompilerParams(dimension_semantics=sem, vmem_limit_bytes=VMEM_LIMIT)


def _rmsnorm(x, g):
    return x * lax.rsqrt(jnp.mean(x * x, axis=-1, keepdims=True) + EPS) * g


def _sigmoid(x):
    return 1.0 / (1.0 + jnp.exp(-x))


def _silu(x):
    return x * _sigmoid(x)


def _gelu_tanh(x):
    return 0.5 * x * (1.0 + jnp.tanh(math.sqrt(2.0 / math.pi) * (x + 0.044715 * (x * x * x))))


def _log_sigmoid(x):
    return jnp.minimum(x, 0.0) - jnp.log1p(jnp.exp(-jnp.abs(x)))


def _dot(a, b):
    return jnp.dot(a, b, preferred_element_type=F32)


def _dot_nt(a, b):
    return lax.dot_general(a, b, (((1,), (1,)), ((), ())), preferred_element_type=F32)


def _dot_tn(a, b):
    return lax.dot_general(a, b, (((0,), (0,)), ((), ())), preferred_element_type=F32)


def _ffn_body(x_ref, g_ref, w1_ref, w2_ref, fg_ref, o_ref, *, d_ff, f_chunk, final):
    x = x_ref[...]
    xn = _rmsnorm(x, g_ref[...]).astype(BF16)
    acc = jnp.zeros(x.shape, F32)
    for f0 in range(0, d_ff, f_chunk):
        gate = _dot(xn, w1_ref[:, f0:f0 + f_chunk])
        up = _dot(xn, w1_ref[:, d_ff + f0:d_ff + f0 + f_chunk])
        act = (_silu(gate) * up).astype(BF16)
        acc = acc + _dot(act, w2_ref[f0:f0 + f_chunk, :])
    y = x + 0.5 * acc
    if final:
        y = _rmsnorm(y, fg_ref[...])
    o_ref[...] = y


def _ffn(x2, g, w1, w2, fg, *, final, tm, f_chunk):
    t, d = x2.shape
    d_ff = w2.shape[0]
    body = functools.partial(_ffn_body, d_ff=d_ff, f_chunk=f_chunk, final=final)
    return pl.pallas_call(
        body,
        grid=(t // tm,),
        in_specs=[pl.BlockSpec((tm, d), lambda i: (i, 0)),
                  _const_spec((1, d)), _const_spec(w1.shape), _const_spec(w2.shape),
                  _const_spec((1, d))],
        out_specs=pl.BlockSpec((tm, d), lambda i: (i, 0)),
        out_shape=jax.ShapeDtypeStruct((t, d), F32),
        compiler_params=_params("parallel"),
        name="ffn",
    )(x2, g.reshape(1, d), w1, w2, fg.reshape(1, d))


def _ab_in_body(x_ref, g_ref, wut_ref, wr_ref, wgk_ref, bgk_ref,
                ut_ref, q_ref, k_ref, v_ref, og_ref, gf_ref, gb_ref, *, hk, hv):
    h = _rmsnorm(x_ref[...], g_ref[...]).astype(BF16)
    ut_ref[0] = _dot_nt(wut_ref[...], h).astype(BF16)
    proj = _dot(h, wr_ref[...])
    q_ref[...] = proj[:, :hk] * (hk // GLA_HEADS) ** -0.5
    k_ref[...] = proj[:, hk:2 * hk]
    v_ref[...] = proj[:, 2 * hk:2 * hk + hv].astype(BF16)
    og_ref[...] = proj[:, 2 * hk + hv:2 * hk + 2 * hv].astype(BF16)
    glo = proj[:, 2 * hk + 2 * hv:].astype(BF16)
    gk = _log_sigmoid(_dot(glo, wgk_ref[...]) + bgk_ref[...]) * (1.0 / GLA_GATE_NORM)
    gf_ref[...] = gk[:, :hk]
    gb_ref[...] = gk[:, hk:]


def _ab_in(x2, g, wut, wr, wgk, bgk, *, b, l, tm, hk, hv):
    t, d = x2.shape
    su = wut.shape[0]
    ni = l // tm
    tok = lambda w: pl.BlockSpec((tm, w), lambda bi, i: (bi * ni + i, 0))
    body = functools.partial(_ab_in_body, hk=hk, hv=hv)
    return pl.pallas_call(
        body,
        grid=(b, ni),
        in_specs=[tok(d), _const_spec((1, d)), _const_spec(wut.shape), _const_spec(wr.shape),
                  _const_spec(wgk.shape), _const_spec(bgk.shape)],
        out_specs=[pl.BlockSpec((1, su, tm), lambda bi, i: (bi, 0, i)),
                   tok(hk), tok(hk), tok(hv), tok(hv), tok(hk), tok(hk)],
        out_shape=[jax.ShapeDtypeStruct((b, su, l), BF16),
                   jax.ShapeDtypeStruct((t, hk), F32), jax.ShapeDtypeStruct((t, hk), F32),
                   jax.ShapeDtypeStruct((t, hv), BF16), jax.ShapeDtypeStruct((t, hv), BF16),
                   jax.ShapeDtypeStruct((t, hk), F32), jax.ShapeDtypeStruct((t, hk), F32)],
        compiler_params=_params("parallel", "parallel"),
        name="ab_in",
    )(x2, g.reshape(1, d), wut, wr, wgk, bgk)


def _cpow_table(ar, ai, count):
    ks = jnp.arange(count)
    pr = jnp.ones(ar.shape + (count,), F32)
    pi = jnp.zeros(ar.shape + (count,), F32)
    sr, si = ar, ai
    for bit in range(max(1, (count - 1).bit_length())):
        on = ((ks >> bit) & 1) == 1
        nr = pr * sr[..., None] - pi * si[..., None]
        ni = pr * si[..., None] + pi * sr[..., None]
        pr = jnp.where(on, nr, pr)
        pi = jnp.where(on, ni, pi)
        sr, si = sr * sr - si * si, 2.0 * sr * si
    return pr, pi


def _s5_tables(lam_re, lam_im, b_re, b_im, c_re, c_im, log_dt, d_skip, n_chunks):
    ch = LANES
    g_, n_ = lam_re.shape[1], lam_re.shape[2]
    p_ = b_re.shape[-1]
    f32 = lambda a: a.astype(F32)
    lr = jnp.minimum(f32(lam_re), -1e-4)
    li = f32(lam_im)
    dt = jnp.exp(f32(log_dt))[..., None]
    mag = jnp.exp(lr * dt)
    ar = mag * jnp.cos(li * dt)
    ai = mag * jnp.sin(li * dt)
    den = lr * lr + li * li
    cr = ((ar - 1.0) * lr + ai * li) / den
    ci = (ai * lr - (ar - 1.0) * li) / den
    br, bi = f32(b_re), f32(b_im)
    bbr = cr[..., None] * br - ci[..., None] * bi
    bbi = cr[..., None] * bi + ci[..., None] * br
    ccr, cci = f32(c_re), f32(c_im)
    pwr, pwi = _cpow_table(ar, ai, ch + 1)

    cnr = ccr.transpose(0, 1, 3, 2)[..., None]
    cni = cci.transpose(0, 1, 3, 2)[..., None]
    cbr = cnr * bbr[:, :, :, None, :] - cni * bbi[:, :, :, None, :]
    cbi = cnr * bbi[:, :, :, None, :] + cni * bbr[:, :, :, None, :]
    taps = (jnp.einsum('dgnpq,dgnl->dgqpl', cbr, pwr[..., :ch], precision=HI)
            - jnp.einsum('dgnpq,dgnl->dgqpl', cbi, pwi[..., :ch], precision=HI))
    kf, kb = taps[0], taps[1]
    dmat = f32(d_skip).reshape(g_, p_)[:, None, :, None] * jnp.eye(p_, dtype=F32)[None, :, :, None]
    center = kf[..., 0:1] + kb[..., 0:1] + dmat
    kfull = jnp.concatenate([kb[..., :0:-1], center, kf[..., 1:]], axis=-1)
    kpad = jnp.pad(kfull, ((0, 0),) * 3 + ((0, 1),))

    def win(d, rev):
        pr = pwr[d][..., :ch]
        pi = pwi[d][..., :ch]
        if rev:
            pr, pi = pr[..., ::-1], pi[..., ::-1]
        pr = pr.transpose(0, 2, 1)[:, None]
        pi = pi.transpose(0, 2, 1)[:, None]
        xr = bbr[d].transpose(0, 2, 1)[:, :, None, :]
        xi = bbi[d].transpose(0, 2, 1)[:, :, None, :]
        return pr * xr - pi * xi, pr * xi + pi * xr
    fre, fim = win(0, True)
    bre, bim = win(1, False)
    w_in = jnp.concatenate([fre, fim, bre, bim], axis=-1).reshape(g_, p_ * ch, 4 * n_).astype(BF16)

    def wout(d, rev):
        pr = pwr[d][..., 1:ch + 1]
        pi = pwi[d][..., 1:ch + 1]
        if rev:
            pr, pi = pr[..., ::-1], pi[..., ::-1]
        pr = pr[:, :, None, :]
        pi = pi[:, :, None, :]
        xr = ccr[d].transpose(0, 2, 1)[..., None]
        xi = cci[d].transpose(0, 2, 1)[..., None]
        return xr * pr - xi * pi, -(xr * pi + xi * pr)
    fre, fim = wout(0, False)
    bre, bim = wout(1, True)
    wc = jnp.concatenate([fre, fim, bre, bim], axis=1).reshape(g_, 4 * n_, p_ * ch).astype(BF16)

    steps = max(1, (n_chunks - 1).bit_length())
    sr, si = pwr[..., ch], pwi[..., ch]
    rows = []
    for _ in range(steps):
        rows.append(jnp.stack([jnp.concatenate([sr, sr], -1), jnp.concatenate([-si, si], -1)], axis=2))
        sr, si = sr * sr - si * si, 2.0 * sr * si
    pq = jnp.stack(rows, axis=2)
    pq = pq.transpose(1, 0, 2, 3, 4)
    return kpad, w_in, wc, pq


def _s5_body(u_ref, taps_ref, win_ref, wc_ref, pq_ref, y_ref, toep_ref, *, nb, nc, p, n, steps):
    ch = LANES
    m = nb * nc
    for q in range(p):
        for pp in range(p):
            row = jnp.broadcast_to(taps_ref[0, q, pp:pp + 1, :], (ch, 2 * ch))
            blk = pltpu.roll(row, ch + 1, 1, stride=1, stride_axis=0)[:, :ch]
            toep_ref[q * ch:(q + 1) * ch, pp * ch:(pp + 1) * ch] = blk.astype(BF16)
    u = jnp.concatenate([u_ref[:, q].reshape(m, ch) for q in range(p)], axis=1)
    y = _dot(u, toep_ref[...])
    r = _dot(u, win_ref[0])
    c = lax.broadcasted_iota(jnp.int32, (m, 2 * n), 0) % nc

    def scan(x, d):
        for k in range(steps):
            s = 1 << k
            if d == 0:
                t = jnp.where(c >= s, pltpu.roll(x, s, axis=0), 0.0)
            else:
                t = jnp.where(c < nc - s, pltpu.roll(x, m - s, axis=0), 0.0)
            x = x + t * pq_ref[0, d, k, 0:1, :] + pltpu.roll(t, n, axis=1) * pq_ref[0, d, k, 1:2, :]
        if d == 0:
            return jnp.where(c >= 1, pltpu.roll(x, 1, axis=0), 0.0)
        return jnp.where(c < nc - 1, pltpu.roll(x, m - 1, axis=0), 0.0)

    xf = scan(r[:, :2 * n], 0)
    xb = scan(r[:, 2 * n:], 1)
    xc = jnp.concatenate([xf, xb], axis=1).astype(BF16)
    y = y + _dot(xc, wc_ref[0])
    for q in range(p):
        y_ref[:, q] = y[:, q * ch:(q + 1) * ch].reshape(nb, nc, ch)


def _s5(u4, taps, w_in, wc, pq):
    nb, su, nc, ch = u4.shape
    g_ = taps.shape[0]
    p = su // g_
    n = wc.shape[1] // 4
    steps = pq.shape[2]
    body = functools.partial(_s5_body, nb=nb, nc=nc, p=p, n=n, steps=steps)
    return pl.pallas_call(
        body,
        grid=(g_,),
        in_specs=[pl.BlockSpec((nb, p, nc, ch), lambda g: (0, g, 0, 0)),
                  pl.BlockSpec((1,) + taps.shape[1:], lambda g: (g, 0, 0, 0)),
                  pl.BlockSpec((1,) + w_in.shape[1:], lambda g: (g, 0, 0)),
                  pl.BlockSpec((1,) + wc.shape[1:], lambda g: (g, 0, 0)),
                  pl.BlockSpec((1,) + pq.shape[1:], lambda g: (g, 0, 0, 0, 0))],
        out_specs=pl.BlockSpec((nb, p, nc, ch), lambda g: (0, g, 0, 0)),
        out_shape=jax.ShapeDtypeStruct((nb, su, nc, ch), F32),
        scratch_shapes=[pltpu.VMEM((p * ch, p * ch), BF16)],
        compiler_params=_params("parallel"),
        name="s5",
    )(u4, taps, w_in, wc, pq)


def _gla_chunk(q, k, v, g, st_ref, base, tri, mask, last_row, dk, dv):
    g_hi = g.astype(BF16)
    g_lo = (g - g_hi.astype(F32)).astype(BF16)
    cum = _dot(tri, g_hi) + _dot(tri, g_lo)
    last = cum[last_row:last_row + 1, :]
    e = jnp.exp(cum)
    q_dec = (q * e).astype(BF16)
    k_inv = (k * jnp.exp(-cum)).astype(BF16)
    k_dec = (k * jnp.exp(last - cum)).astype(BF16)
    decay = jnp.exp(last)
    outs = []
    for h in range(GLA_HEADS):
        ks = slice(h * dk, (h + 1) * dk)
        vs = slice(h * dv, (h + 1) * dv)
        s = jnp.where(mask, _dot_nt(q_dec[:, ks], k_inv[:, ks]), 0.0)
        st = st_ref[base + h]
        o = _dot(s.astype(BF16), v[:, vs]) + _dot_nt(q_dec[:, ks], st.astype(BF16))
        st_ref[base + h] = decay[:, ks] * st + _dot_tn(v[:, vs], k_dec[:, ks])
        outs.append(o)
    return jnp.concatenate(outs, axis=1)


def _gla_body(qf_ref, kf_ref, vf_ref, gf_ref, qb_ref, kb_ref, vb_ref, gb_ref,
              of_ref, ob_ref, sf_ref, sb_ref, *, tb, dk, dv):
    c = GLA_CHUNK

    @pl.when(pl.program_id(1) == 0)
    def _():
        sf_ref[...] = jnp.zeros(sf_ref.shape, F32)
        sb_ref[...] = jnp.zeros(sb_ref.shape, F32)

    ii = lax.broadcasted_iota(jnp.int32, (c, c), 0)
    jj = lax.broadcasted_iota(jnp.int32, (c, c), 1)
    tri_f = (jj <= ii).astype(BF16)
    tri_b = (jj >= ii).astype(BF16)
    mask_f = jj <= ii
    mask_b = jj > ii
    for n in range(tb // c):
        rf = slice(n * c, (n + 1) * c)
        of_ref[rf, :] = _gla_chunk(qf_ref[rf, :], kf_ref[rf, :], vf_ref[rf, :], gf_ref[rf, :],
                                   sf_ref, 0, tri_f, mask_f, c - 1, dk, dv)
        nr = tb // c - 1 - n
        rb = slice(nr * c, (nr + 1) * c)
        ob_ref[rb, :] = _gla_chunk(qb_ref[rb, :], kb_ref[rb, :], vb_ref[rb, :], gb_ref[rb, :],
                                   sb_ref, 0, tri_b, mask_b, 0, dk, dv)


def _gla(q, k, v, gf, gb, *, b, l, tb):
    t, hk = q.shape
    hv = v.shape[1]
    dk, dv = hk // GLA_HEADS, hv // GLA_HEADS
    ni = l // tb
    fwd = lambda w: pl.BlockSpec((tb, w), lambda bi, i: (bi * ni + i, 0))
    bwd = lambda w: pl.BlockSpec((tb, w), lambda bi, i: (bi * ni + ni - 1 - i, 0))
    body = functools.partial(_gla_body, tb=tb, dk=dk, dv=dv)
    state = pltpu.VMEM((GLA_HEADS, dv, dk), F32)
    return pl.pallas_call(
        body,
        grid=(b, ni),
        in_specs=[fwd(hk), fwd(hk), fwd(hv), fwd(hk), bwd(hk), bwd(hk), bwd(hv), bwd(hk)],
        out_specs=[fwd(hv), bwd(hv)],
        out_shape=[jax.ShapeDtypeStruct((t, hv), F32), jax.ShapeDtypeStruct((t, hv), F32)],
        scratch_shapes=[state, state],
        compiler_params=_params("parallel", "arbitrary"),
        name="gla",
    )(q, k, v, gf, q, k, v, gb)


def _head_norm(o, g, heads):
    dh = o.shape[1] // heads
    parts = []
    for h in range(heads):
        oh = o[:, h * dh:(h + 1) * dh]
        parts.append(oh * lax.rsqrt(jnp.mean(oh * oh, axis=-1, keepdims=True) + EPS))
    return jnp.concatenate(parts, axis=1) * g


def _ab_out_body(x_ref, yt_ref, wglut_ref, of_ref, ob_ref, og_ref, gn_ref, wo_ref, o_ref, *, su):
    gy = _gelu_tanh(yt_ref[0])
    z = _dot(wglut_ref[...], gy.astype(BF16))
    s5_out = (gy * _sigmoid(z)).T.astype(BF16)
    o = _head_norm(of_ref[...] + ob_ref[...], gn_ref[...], GLA_HEADS)
    gla_out = (o * _silu(og_ref[...].astype(F32))).astype(BF16)
    o_ref[...] = x_ref[...] + _dot(s5_out, wo_ref[:su, :]) + _dot(gla_out, wo_ref[su:, :])


def _ab_out(x2, yt, wglut, o_f, o_b, og, gn, wo, *, b, l, tm):
    t, d = x2.shape
    su = yt.shape[1]
    hv = o_f.shape[1]
    ni = l // tm
    tok = lambda w: pl.BlockSpec((tm, w), lambda bi, i: (bi * ni + i, 0))
    body = functools.partial(_ab_out_body, su=su)
    return pl.pallas_call(
        body,
        grid=(b, ni),
        in_specs=[tok(d), pl.BlockSpec((1, su, tm), lambda bi, i: (bi, 0, i)), _const_spec(wglut.shape),
                  tok(hv), tok(hv), tok(hv), _const_spec((1, hv)), _const_spec(wo.shape)],
        out_specs=tok(d),
        out_shape=jax.ShapeDtypeStruct((t, d), F32),
        compiler_params=_params("parallel", "parallel"),
        name="ab_out",
    )(x2, yt, wglut, o_f, o_b, og, gn.reshape(1, hv), wo)


def _ret_in_body(x_ref, g_ref, w_ref, wkt_ref, cos_ref, sin_ref, cost_ref, sint_ref,
                 q_ref, kt_ref, v_ref, og_ref, *, hk, hv, dk):
    h = _rmsnorm(x_ref[...], g_ref[...]).astype(BF16)
    cos2 = cos_ref[...]
    sin2 = sin_ref[...]
    q = _dot(h, w_ref[:, :hk])
    q_ref[...] = jnp.concatenate(
        [q[:, s:s + dk] * cos2 + pltpu.roll(q[:, s:s + dk], dk // 2, axis=1) * sin2 for s in range(0, hk, dk)],
        axis=1).astype(BF16)
    kt = _dot_nt(wkt_ref[...], h) * dk ** -0.5
    cost = cost_ref[...]
    sint = sint_ref[...]
    parts = []
    for s in range(0, hk, dk):
        t1 = kt[s:s + dk // 2]
        t2 = kt[s + dk // 2:s + dk]
        parts += [t1 * cost - t2 * sint, t1 * sint + t2 * cost]
    kt_ref[0] = jnp.concatenate(parts, axis=0).astype(BF16)
    v_ref[...] = _dot(h, w_ref[:, hk:hk + hv]).astype(BF16)
    og_ref[...] = _dot(h, w_ref[:, hk + hv:]).astype(BF16)


def _ret_in(x2, g, w, wkt, cos2, sin2, cost, sint, *, b, l, tm, hk, hv):
    t, d = x2.shape
    dk = hk // RET_HEADS
    ni = l // tm
    tok = lambda w_: pl.BlockSpec((tm, w_), lambda bi, i: (bi * ni + i, 0))
    pos = pl.BlockSpec((tm, dk), lambda bi, i: (i, 0))
    post = pl.BlockSpec((dk // 2, tm), lambda bi, i: (0, i))
    body = functools.partial(_ret_in_body, hk=hk, hv=hv, dk=dk)
    return pl.pallas_call(
        body,
        grid=(b, ni),
        in_specs=[tok(d), _const_spec((1, d)), _const_spec(w.shape), _const_spec(wkt.shape), pos, pos, post, post],
        out_specs=[tok(hk), pl.BlockSpec((1, hk, tm), lambda bi, i: (bi, 0, i)), tok(hv), tok(hv)],
        out_shape=[jax.ShapeDtypeStruct((t, hk), BF16), jax.ShapeDtypeStruct((b, hk, l), BF16),
                   jax.ShapeDtypeStruct((t, hv), BF16), jax.ShapeDtypeStruct((t, hv), BF16)],
        compiler_params=_params("parallel", "parallel"),
        name="ret_in",
    )(x2, g.reshape(1, d), w, wkt, cos2, sin2, cost, sint)


def _ret_tables(c, dk, dv):
    lg_f = jnp.log1p(-jnp.exp2(-5.0 - jnp.arange(RET_HEADS, dtype=F32)))
    lg_b = lg_f[::-1]
    idx = jnp.arange(c)
    diff = (idx[:, None] - idx[None, :]).astype(F32)
    dmat = jnp.where(diff >= 0, jnp.exp(jnp.maximum(diff, 0.0)[None] * lg_f[:, None, None]),
                     jnp.exp(jnp.maximum(-diff, 0.0)[None] * lg_b[:, None, None]))
    pos = idx.astype(F32)
    ones_v = jnp.ones((1, 1, dv), F32)
    xi_f = jnp.exp((pos + 1.0)[None, :] * lg_f[:, None])[..., None] * ones_v
    xi_b = jnp.exp((c - pos)[None, :] * lg_b[:, None])[..., None] * ones_v
    zeta_f = jnp.exp((c - 1.0 - pos)[None, :] * lg_f[:, None])[:, None, :]
    zeta_b = jnp.exp(pos[None, :] * lg_b[:, None])[:, None, :]
    cd_f = jnp.exp(c * lg_f)[:, None, None] * ones_v
    cd_b = jnp.exp(c * lg_b)[:, None, None] * ones_v
    return dmat, xi_f, xi_b, zeta_f, zeta_b, cd_f, cd_b


def _ret_state_body(kt_ref, v_ref, zb_ref, cdb_ref, sb_out_ref, sb_ref, *, dk, dv):
    @pl.when(pl.program_id(1) == 0)
    def _():
        sb_ref[...] = jnp.zeros(sb_ref.shape, F32)

    for h in range(RET_HEADS):
        st = sb_ref[h]
        sb_out_ref[0, 0, h] = st.astype(BF16)
        kz = (kt_ref[0, h * dk:(h + 1) * dk, :].astype(F32) * zb_ref[h]).astype(BF16)
        sb_ref[h] = cdb_ref[h] * st + _dot(kz, v_ref[:, h * dv:(h + 1) * dv])


def _ret_state(kt, v, zeta_b, cd_b, *, b, l, tb):
    hk, hv = kt.shape[1], v.shape[1]
    dk, dv = hk // RET_HEADS, hv // RET_HEADS
    ni = l // tb
    bwd = lambda w: pl.BlockSpec((tb, w), lambda bi, i: (bi * ni + ni - 1 - i, 0))
    body = functools.partial(_ret_state_body, dk=dk, dv=dv)
    return pl.pallas_call(
        body,
        grid=(b, ni),
        in_specs=[pl.BlockSpec((1, hk, tb), lambda bi, i: (bi, 0, ni - 1 - i)), bwd(hv),
                  _const_spec(zeta_b.shape), _const_spec(cd_b.shape)],
        out_specs=pl.BlockSpec((1, 1, RET_HEADS, dk, dv), lambda bi, i: (bi, ni - 1 - i, 0, 0, 0)),
        out_shape=jax.ShapeDtypeStruct((b, ni, RET_HEADS, dk, dv), BF16),
        scratch_shapes=[pltpu.VMEM((RET_HEADS, dk, dv), F32)],
        compiler_params=_params("parallel", "arbitrary"),
        name="ret_state",
    )(kt, v, zeta_b, cd_b)


def _ret_main_body(x_ref, q_ref, kt_ref, v_ref, og_ref, sb_ref, dmat_ref, xif_ref, xib_ref, zf_ref, cdf_ref,
                   gn_ref, wo_ref, o_ref, sf_ref, gated_ref, *, dk, dv):
    @pl.when(pl.program_id(1) == 0)
    def _():
        sf_ref[...] = jnp.zeros(sf_ref.shape, F32)

    for h in range(RET_HEADS):
        ks = slice(h * dk, (h + 1) * dk)
        vs = slice(h * dv, (h + 1) * dv)
        q = q_ref[:, ks]
        kt = kt_ref[0, ks, :]
        v = v_ref[:, vs]
        s = _dot(q, kt) * dmat_ref[h]
        st = sf_ref[h]
        o = (_dot(s.astype(BF16), v) + _dot(q, st.astype(BF16)) * xif_ref[h]
             + _dot(q, sb_ref[0, 0, h]) * xib_ref[h])
        kz = (kt.astype(F32) * zf_ref[h]).astype(BF16)
        sf_ref[h] = cdf_ref[h] * st + _dot(kz, v)
        o = o * lax.rsqrt(jnp.mean(o * o, axis=-1, keepdims=True) + EPS) * gn_ref[:, vs]
        gated_ref[:, vs] = (o * _silu(og_ref[:, vs].astype(F32))).astype(BF16)
    o_ref[...] = x_ref[...] + _dot(gated_ref[...], wo_ref[...])


def _ret_main(x2, q, kt, v, og, sb, tabs, gn, wo, *, b, l, tb):
    t, d = x2.shape
    hk, hv = q.shape[1], v.shape[1]
    dk, dv = hk // RET_HEADS, hv // RET_HEADS
    ni = l // tb
    tok = lambda w: pl.BlockSpec((tb, w), lambda bi, i: (bi * ni + i, 0))
    body = functools.partial(_ret_main_body, dk=dk, dv=dv)
    return pl.pallas_call(
        body,
        grid=(b, ni),
        in_specs=[tok(d), tok(hk), pl.BlockSpec((1, hk, tb), lambda bi, i: (bi, 0, i)), tok(hv), tok(hv),
                  pl.BlockSpec((1, 1, RET_HEADS, dk, dv), lambda bi, i: (bi, i, 0, 0, 0))]
                 + [_const_spec(a.shape) for a in tabs] + [_const_spec((1, hv)), _const_spec(wo.shape)],
        out_specs=tok(d),
        out_shape=jax.ShapeDtypeStruct((t, d), F32),
        scratch_shapes=[pltpu.VMEM((RET_HEADS, dk, dv), F32), pltpu.VMEM((tb, hv), BF16)],
        compiler_params=_params("parallel", "arbitrary"),
        name="ret_main",
    )(x2, q, kt, v, og, sb, *tabs, gn.reshape(1, hv), wo)


def _tile(l, want):
    t = min(l, want)
    while l % t:
        t //= 2
    return t


def _s5_gla_layer(x2, mix_g, w_in, lam_re, lam_im, b_re, b_im, c_re, c_im, log_dt, d_skip, w_glu,
                  w_gk, b_gk, gla_norm, w_out, *, b, l):
    su = d_skip.shape[0]
    hk = w_gk.shape[-1]
    r = w_gk.shape[1]
    hv = gla_norm.shape[0]
    tm = _tile(l, 512)
    wut = w_in[:, :su].T.astype(BF16)
    wr = w_in[:, su:].astype(BF16)
    wgk = jnp.zeros((2 * r, 2 * hk), F32)
    wgk = wgk.at[:r, :hk].set(w_gk[0]).at[r:, hk:].set(w_gk[1]).astype(BF16)
    bgk = b_gk.reshape(1, 2 * hk).astype(F32)
    ut, q, k, v, og, gf, gb = _ab_in(x2, mix_g, wut, wr, wgk, bgk, b=b, l=l, tm=tm, hk=hk, hv=hv)

    nc = l // LANES
    taps, w_st, wc, pq = _s5_tables(lam_re, lam_im, b_re, b_im, c_re, c_im, log_dt, d_skip, nc)
    y4 = _s5(ut.reshape(b, su, nc, LANES), taps, w_st, wc, pq)
    yt = y4.reshape(b, su, l)

    o_f, o_b = _gla(q, k, v, gf, gb, b=b, l=l, tb=_tile(l, 512))
    return _ab_out(x2, yt, w_glu.T.astype(BF16), o_f, o_b, og, gla_norm, w_out.astype(BF16), b=b, l=l, tm=tm)


def _retention_layer(x2, mix_g, w_in, ret_norm, w_out, *, b, l):
    hv = ret_norm.shape[0]
    hk = (w_in.shape[1] - 2 * hv) // 2
    dk = hk // RET_HEADS
    half = dk // 2
    tm = _tile(l, 512)
    pos = jnp.arange(l, dtype=F32)
    inv = jnp.exp(-math.log(ROPE_BASE) * jnp.arange(half, dtype=F32) / half)
    ang = pos[:, None] * inv[None, :]
    cos, sin = jnp.cos(ang), jnp.sin(ang)
    cos2 = jnp.concatenate([cos, cos], axis=1)
    sin2 = jnp.concatenate([-sin, sin], axis=1)
    w_qvg = jnp.concatenate([w_in[:, :hk], w_in[:, 2 * hk:]], axis=1).astype(BF16)
    w_kt = w_in[:, hk:2 * hk].T.astype(BF16)
    q, kt, v, og = _ret_in(x2, mix_g, w_qvg, w_kt, cos2, sin2, cos.T, sin.T, b=b, l=l, tm=tm, hk=hk, hv=hv)
    tb = _tile(l, 256)
    dmat, xi_f, xi_b, zeta_f, zeta_b, cd_f, cd_b = _ret_tables(tb, dk, hv // RET_HEADS)
    sb = _ret_state(kt, v, zeta_b, cd_b, b=b, l=l, tb=tb)
    return _ret_main(x2, q, kt, v, og, sb, (dmat, xi_f, xi_b, zeta_f, cd_f), ret_norm, w_out.astype(BF16),
                     b=b, l=l, tb=tb)


def kernel(x, ffn1_norm, ffn1_w1, ffn1_w2, mix_norm, ffn2_norm, ffn2_w1, ffn2_w2, ab_w_in, s5_lambda_re, s5_lambda_im, s5_b_re, s5_b_im, s5_c_re, s5_c_im, s5_log_dt, s5_d, s5_w_glu, gla_w_gk, gla_b_gk, gla_norm, ab_w_out, ret_w_in, ret_norm, ret_w_out, final_norm):
    b, l, d = x.shape
    depth = ffn1_norm.shape[0]
    d_ff = ffn1_w2.shape[1]
    tm = _tile(b * l, 1024)
    f_chunk = 256 if d_ff % 256 == 0 else d_ff
    x2 = x.reshape(b * l, d).astype(F32)
    for i in range(depth):
        j = i // 2
        x2 = _ffn(x2, ffn1_norm[i], ffn1_w1[i].astype(BF16), ffn1_w2[i].astype(BF16), final_norm,
                  final=False, tm=tm, f_chunk=f_chunk)
        if i % 2 == 0:
            x2 = _s5_gla_layer(x2, mix_norm[i], ab_w_in[j], s5_lambda_re[j], s5_lambda_im[j], s5_b_re[j],
                               s5_b_im[j], s5_c_re[j], s5_c_im[j], s5_log_dt[j], s5_d[j], s5_w_glu[j],
                               gla_w_gk[j], gla_b_gk[j], gla_norm[j], ab_w_out[j], b=b, l=l)
        else:
            x2 = _retention_layer(x2, mix_norm[i], ret_w_in[j], ret_norm[j], ret_w_out[j], b=b, l=l)
        x2 = _ffn(x2, ffn2_norm[i], ffn2_w1[i].astype(BF16), ffn2_w2[i].astype(BF16), final_norm,
                  final=(i == depth - 1), tm=tm, f_chunk=f_chunk)
    return x2.reshape(b, l, d)
```

```python
import functools
import math

import jax
import jax.numpy as jnp
from jax import lax
from jax.experimental import pallas as pl
from jax.experimental.pallas import tpu as pltpu

F32 = jnp.float32
BF16 = jnp.bfloat16

EPS = 1e-6
S5_GROUP = 16
S5_STATE = 64
GLA_HEADS = 4
GLA_RANK = 16
GLA_GATE_NORM = 16.0
GLA_CHUNK = 64
RET_HEADS = 8
ROPE_BASE = 10000.0

LANES = 128
VMEM_LIMIT = 56 * 1024 * 1024
HI = lax.Precision.HIGHEST


def _const_spec(shape):
    nd = len(shape)
    return pl.BlockSpec(shape, lambda *_: (0,) * nd, pipeline_mode=pl.Buffered(1))


def _params(*sem):
    return pltpu.CompilerParams(dimension_semantics=sem, vmem_limit_bytes=VMEM_LIMIT)


def _rmsnorm(x, g):
    return x * lax.rsqrt(jnp.mean(x * x, axis=-1, keepdims=True) + EPS) * g


def _sigmoid(x):
    return 1.0 / (1.0 + jnp.exp(-x))


def _silu(x):
    return x * _sigmoid(x)


def _gelu_tanh(x):
    return 0.5 * x * (1.0 + jnp.tanh(math.sqrt(2.0 / math.pi) * (x + 0.044715 * (x * x * x))))


def _log_sigmoid(x):
    return jnp.minimum(x, 0.0) - jnp.log1p(jnp.exp(-jnp.abs(x)))


def _dot(a, b):
    return jnp.dot(a, b, preferred_element_type=F32)


def _dot_nt(a, b):
    return lax.dot_general(a, b, (((1,), (1,)), ((), ())), preferred_element_type=F32)


def _dot_tn(a, b):
    return lax.dot_general(a, b, (((0,), (0,)), ((), ())), preferred_element_type=F32)


def _ffn_body(x_ref, g_ref, w1_ref, w2_ref, fg_ref, o_ref, *, d_ff, f_chunk, final):
    x = x_ref[...]
    xn = _rmsnorm(x, g_ref[...]).astype(BF16)
    acc = jnp.zeros(x.shape, F32)
    for f0 in range(0, d_ff, f_chunk):
        gate = _dot(xn, w1_ref[:, f0:f0 + f_chunk])
        up = _dot(xn, w1_ref[:, d_ff + f0:d_ff + f0 + f_chunk])
        act = (_silu(gate) * up).astype(BF16)
        acc = acc + _dot(act, w2_ref[f0:f0 + f_chunk, :])
    y = x + 0.5 * acc
    if final:
        y = _rmsnorm(y, fg_ref[...])
    o_ref[...] = y


def _ffn(x2, g, w1, w2, fg, *, final, tm, f_chunk):
    t, d = x2.shape
    d_ff = w2.shape[0]
    body = functools.partial(_ffn_body, d_ff=d_ff, f_chunk=f_chunk, final=final)
    return pl.pallas_call(
        body,
        grid=(t // tm,),
        in_specs=[pl.BlockSpec((tm, d), lambda i: (i, 0)),
                  _const_spec((1, d)), _const_spec(w1.shape), _const_spec(w2.shape),
                  _const_spec((1, d))],
        out_specs=pl.BlockSpec((tm, d), lambda i: (i, 0)),
        out_shape=jax.ShapeDtypeStruct((t, d), F32),
        compiler_params=_params("parallel"),
        name="ffn",
    )(x2, g.reshape(1, d), w1, w2, fg.reshape(1, d))


def _ab_in_body(x_ref, g_ref, wut_ref, wr_ref, wgk_ref, bgk_ref,
                ut_ref, q_ref, k_ref, v_ref, og_ref, gf_ref, gb_ref, *, hk, hv):
    h = _rmsnorm(x_ref[...], g_ref[...]).astype(BF16)
    ut_ref[0] = _dot_nt(wut_ref[...], h).astype(BF16)
    proj = _dot(h, wr_ref[...])
    q_ref[...] = proj[:, :hk] * (hk // GLA_HEADS) ** -0.5
    k_ref[...] = proj[:, hk:2 * hk]
    v_ref[...] = proj[:, 2 * hk:2 * hk + hv].astype(BF16)
    og_ref[...] = proj[:, 2 * hk + hv:2 * hk + 2 * hv].astype(BF16)
    glo = proj[:, 2 * hk + 2 * hv:].astype(BF16)
    gk = _log_sigmoid(_dot(glo, wgk_ref[...]) + bgk_ref[...]) * (1.0 / GLA_GATE_NORM)
    gf_ref[...] = gk[:, :hk]
    gb_ref[...] = gk[:, hk:]


def _ab_in(x2, g, wut, wr, wgk, bgk, *, b, l, tm, hk, hv):
    t, d = x2.shape
    su = wut.shape[0]
    ni = l // tm
    tok = lambda w: pl.BlockSpec((tm, w), lambda bi, i: (bi * ni + i, 0))
    body = functools.partial(_ab_in_body, hk=hk, hv=hv)
    return pl.pallas_call(
        body,
        grid=(b, ni),
        in_specs=[tok(d), _const_spec((1, d)), _const_spec(wut.shape), _const_spec(wr.shape),
                  _const_spec(wgk.shape), _const_spec(bgk.shape)],
        out_specs=[pl.BlockSpec((1, su, tm), lambda bi, i: (bi, 0, i)),
                   tok(hk), tok(hk), tok(hv), tok(hv), tok(hk), tok(hk)],
        out_shape=[jax.ShapeDtypeStruct((b, su, l), BF16),
                   jax.ShapeDtypeStruct((t, hk), F32), jax.ShapeDtypeStruct((t, hk), F32),
                   jax.ShapeDtypeStruct((t, hv), BF16), jax.ShapeDtypeStruct((t, hv), BF16),
                   jax.ShapeDtypeStruct((t, hk), F32), jax.ShapeDtypeStruct((t, hk), F32)],
        compiler_params=_params("parallel", "parallel"),
        name="ab_in",
    )(x2, g.reshape(1, d), wut, wr, wgk, bgk)


def _cpow_table(ar, ai, count):
    ks = jnp.arange(count)
    pr = jnp.ones(ar.shape + (count,), F32)
    pi = jnp.zeros(ar.shape + (count,), F32)
    sr, si = ar, ai
    for bit in range(max(1, (count - 1).bit_length())):
        on = ((ks >> bit) & 1) == 1
        nr = pr * sr[..., None] - pi * si[..., None]
        ni = pr * si[..., None] + pi * sr[..., None]
        pr = jnp.where(on, nr, pr)
        pi = jnp.where(on, ni, pi)
        sr, si = sr * sr - si * si, 2.0 * sr * si
    return pr, pi


def _s5_tables(lam_re, lam_im, b_re, b_im, c_re, c_im, log_dt, d_skip, n_chunks):
    ch = LANES
    g_, n_ = lam_re.shape[1], lam_re.shape[2]
    p_ = b_re.shape[-1]
    f32 = lambda a: a.astype(F32)
    lr = jnp.minimum(f32(lam_re), -1e-4)
    li = f32(lam_im)
    dt = jnp.exp(f32(log_dt))[..., None]
    mag = jnp.exp(lr * dt)
    ar = mag * jnp.cos(li * dt)
    ai = mag * jnp.sin(li * dt)
    den = lr * lr + li * li
    cr = ((ar - 1.0) * lr + ai * li) / den
    ci = (ai * lr - (ar - 1.0) * li) / den
    br, bi = f32(b_re), f32(b_im)
    bbr = cr[..., None] * br - ci[..., None] * bi
    bbi = cr[..., None] * bi + ci[..., None] * br
    ccr, cci = f32(c_re), f32(c_im)
    pwr, pwi = _cpow_table(ar, ai, ch + 1)

    cnr = ccr.transpose(0, 1, 3, 2)[..., None]
    cni = cci.transpose(0, 1, 3, 2)[..., None]
    cbr = cnr * bbr[:, :, :, None, :] - cni * bbi[:, :, :, None, :]
    cbi = cnr * bbi[:, :, :, None, :] + cni * bbr[:, :, :, None, :]
    taps = (jnp.einsum('dgnpq,dgnl->dgqpl', cbr, pwr[..., :ch], precision=HI)
            - jnp.einsum('dgnpq,dgnl->dgqpl', cbi, pwi[..., :ch], precision=HI))
    kf, kb = taps[0], taps[1]
    dmat = f32(d_skip).reshape(g_, p_)[:, None, :, None] * jnp.eye(p_, dtype=F32)[None, :, :, None]
    center = kf[..., 0:1] + kb[..., 0:1] + dmat
    kfull = jnp.concatenate([kb[..., :0:-1], center, kf[..., 1:]], axis=-1)
    kpad = jnp.pad(kfull, ((0, 0),) * 3 + ((0, 1),))

    def win(d, rev):
        pr = pwr[d][..., :ch]
        pi = pwi[d][..., :ch]
        if rev:
            pr, pi = pr[..., ::-1], pi[..., ::-1]
        pr = pr.transpose(0, 2, 1)[:, None]
        pi = pi.transpose(0, 2, 1)[:, None]
        xr = bbr[d].transpose(0, 2, 1)[:, :, None, :]
        xi = bbi[d].transpose(0, 2, 1)[:, :, None, :]
        return pr * xr - pi * xi, pr * xi + pi * xr
    fre, fim = win(0, True)
    bre, bim = win(1, False)
    w_in = jnp.concatenate([fre, fim, bre, bim], axis=-1).reshape(g_, p_ * ch, 4 * n_).astype(BF16)

    def wout(d, rev):
        pr = pwr[d][..., 1:ch + 1]
        pi = pwi[d][..., 1:ch + 1]
        if rev:
            pr, pi = pr[..., ::-1], pi[..., ::-1]
        pr = pr[:, :, None, :]
        pi = pi[:, :, None, :]
        xr = ccr[d].transpose(0, 2, 1)[..., None]
        xi = cci[d].transpose(0, 2, 1)[..., None]
        return xr * pr - xi * pi, -(xr * pi + xi * pr)
    fre, fim = wout(0, False)
    bre, bim = wout(1, True)
    wc = jnp.concatenate([fre, fim, bre, bim], axis=1).reshape(g_, 4 * n_, p_ * ch).astype(BF16)

    steps = max(1, (n_chunks - 1).bit_length())
    sr, si = pwr[..., ch], pwi[..., ch]
    rows = []
    for _ in range(steps):
        rows.append(jnp.stack([jnp.concatenate([sr, sr], -1), jnp.concatenate([-si, si], -1)], axis=2))
        sr, si = sr * sr - si * si, 2.0 * sr * si
    pq = jnp.stack(rows, axis=2)
    pq = pq.transpose(1, 0, 2, 3, 4)
    return kpad, w_in, wc, pq


def _s5_body(u_ref, taps_ref, win_ref, wc_ref, pq_ref, y_ref, toep_ref, *, nb, nc, p, n, steps):
    ch = LANES
    m = nb * nc
    for q in range(p):
        for pp in range(p):
            row = jnp.broadcast_to(taps_ref[0, q, pp:pp + 1, :], (ch, 2 * ch))
            blk = pltpu.roll(row, ch + 1, 1, stride=1, stride_axis=0)[:, :ch]
            toep_ref[q * ch:(q + 1) * ch, pp * ch:(pp + 1) * ch] = blk.astype(BF16)
    u = jnp.concatenate([u_ref[:, q].reshape(m, ch) for q in range(p)], axis=1)
    y = _dot(u, toep_ref[...])
    r = _dot(u, win_ref[0])
    c = lax.broadcasted_iota(jnp.int32, (m, 2 * n), 0) % nc

    def scan(x, d):
        for k in range(steps):
            s = 1 << k
            if d == 0:
                t = jnp.where(c >= s, pltpu.roll(x, s, axis=0), 0.0)
            else:
                t = jnp.where(c < nc - s, pltpu.roll(x, m - s, axis=0), 0.0)
            x = x + t * pq_ref[0, d, k, 0:1, :] + pltpu.roll(t, n, axis=1) * pq_ref[0, d, k, 1:2, :]
        if d == 0:
            return jnp.where(c >= 1, pltpu.roll(x, 1, axis=0), 0.0)
        return jnp.where(c < nc - 1, pltpu.roll(x, m - 1, axis=0), 0.0)

    xf = scan(r[:, :2 * n], 0)
    xb = scan(r[:, 2 * n:], 1)
    xc = jnp.concatenate([xf, xb], axis=1).astype(BF16)
    y = y + _dot(xc, wc_ref[0])
    for q in range(p):
        y_ref[:, q] = y[:, q * ch:(q + 1) * ch].reshape(nb, nc, ch)


def _s5(u4, taps, w_in, wc, pq):
    nb, su, nc, ch = u4.shape
    g_ = taps.shape[0]
    p = su // g_
    n = wc.shape[1] // 4
    steps = pq.shape[2]
    body = functools.partial(_s5_body, nb=nb, nc=nc, p=p, n=n, steps=steps)
    return pl.pallas_call(
        body,
        grid=(g_,),
        in_specs=[pl.BlockSpec((nb, p, nc, ch), lambda g: (0, g, 0, 0)),
                  pl.BlockSpec((1,) + taps.shape[1:], lambda g: (g, 0, 0, 0)),
                  pl.BlockSpec((1,) + w_in.shape[1:], lambda g: (g, 0, 0)),
                  pl.BlockSpec((1,) + wc.shape[1:], lambda g: (g, 0, 0)),
                  pl.BlockSpec((1,) + pq.shape[1:], lambda g: (g, 0, 0, 0, 0))],
        out_specs=pl.BlockSpec((nb, p, nc, ch), lambda g: (0, g, 0, 0)),
        out_shape=jax.ShapeDtypeStruct((nb, su, nc, ch), F32),
        scratch_shapes=[pltpu.VMEM((p * ch, p * ch), BF16)],
        compiler_params=_params("parallel"),
        name="s5",
    )(u4, taps, w_in, wc, pq)


def _gla_prep(q, k, v, g, tri, cmask, bdk, bd, ones, last_row):
    g_hi = g.astype(BF16)
    g_lo = (g - g_hi.astype(F32)).astype(BF16)
    cum = _dot(tri, g_hi) + _dot(tri, g_lo)
    last = cum[last_row:last_row + 1, :]
    q_dec = (q * jnp.exp(cum)).astype(BF16)
    k_inv = (k * jnp.exp(-cum)).astype(BF16)
    k_dec = (k * jnp.exp(last - cum)).astype(BF16)
    kb = jnp.where(bdk, jnp.concatenate([k_inv] * GLA_HEADS, axis=0), 0.0)
    s = jnp.where(cmask, _dot_nt(q_dec, kb), 0.0).astype(BF16)
    vb = jnp.where(bd, jnp.concatenate([v] * GLA_HEADS, axis=0), 0.0)
    o_intra = _dot(s, vb)
    tot = _dot_tn(g_hi, ones) + _dot_tn(g_lo, ones)
    kv = jnp.where(bd, _dot_tn(k_dec, v), 0.0)
    return q_dec, o_intra, kv, jnp.exp(tot)


def _gla_sweep(preps, order, st_ref, o_ref, c):
    st = st_ref[...]
    for n in order:
        q_dec, o_intra, kv, decay = preps[n]
        o_ref[n * c:(n + 1) * c, :] = o_intra + _dot(q_dec, st.astype(BF16))
        st = jnp.concatenate([decay] * (st.shape[1] // LANES), axis=1) * st + kv
    st_ref[...] = st


def _gla_body(qf_ref, kf_ref, vf_ref, gf_ref, qb_ref, kb_ref, vb_ref, gb_ref,
              of_ref, ob_ref, sf_ref, sb_ref, *, tb, dk, dv):
    c = GLA_CHUNK

    @pl.when(pl.program_id(1) == 0)
    def _():
        sf_ref[...] = jnp.zeros(sf_ref.shape, F32)
        sb_ref[...] = jnp.zeros(sb_ref.shape, F32)

    ii = lax.broadcasted_iota(jnp.int32, (c, c), 0)
    jj = lax.broadcasted_iota(jnp.int32, (c, c), 1)
    tri_f = (jj <= ii).astype(BF16)
    tri_b = (jj >= ii).astype(BF16)
    hc = GLA_HEADS * c
    i2 = lax.broadcasted_iota(jnp.int32, (c, hc), 0)
    j2 = lax.broadcasted_iota(jnp.int32, (c, hc), 1) % c
    cmask_f = j2 <= i2
    cmask_b = j2 > i2
    bdk = (lax.broadcasted_iota(jnp.int32, (hc, GLA_HEADS * dk), 0) // c
           == lax.broadcasted_iota(jnp.int32, (hc, GLA_HEADS * dk), 1) // dk)
    bd = (lax.broadcasted_iota(jnp.int32, (hc, GLA_HEADS * dv), 0) // c
          == lax.broadcasted_iota(jnp.int32, (hc, GLA_HEADS * dv), 1) // dv)
    ones = jnp.ones((c, LANES), BF16)
    nch = tb // c
    preps_f, preps_b = [], []
    for n in range(nch):
        r = slice(n * c, (n + 1) * c)
        preps_f.append(_gla_prep(qf_ref[r, :], kf_ref[r, :], vf_ref[r, :], gf_ref[r, :],
                                 tri_f, cmask_f, bdk, bd, ones, c - 1))
        preps_b.append(_gla_prep(qb_ref[r, :], kb_ref[r, :], vb_ref[r, :], gb_ref[r, :],
                                 tri_b, cmask_b, bdk, bd, ones, 0))
    _gla_sweep(preps_f, range(nch), sf_ref, of_ref, c)
    _gla_sweep(preps_b, range(nch - 1, -1, -1), sb_ref, ob_ref, c)


def _gla(q, k, v, gf, gb, *, b, l, tb):
    t, hk = q.shape
    hv = v.shape[1]
    dk, dv = hk // GLA_HEADS, hv // GLA_HEADS
    ni = l // tb
    fwd = lambda w: pl.BlockSpec((tb, w), lambda bi, i: (bi * ni + i, 0))
    bwd = lambda w: pl.BlockSpec((tb, w), lambda bi, i: (bi * ni + ni - 1 - i, 0))
    assert dk == GLA_CHUNK, "state mask reuse needs chunk == key head width"
    body = functools.partial(_gla_body, tb=tb, dk=dk, dv=dv)
    state = pltpu.VMEM((hk, hv), F32)
    return pl.pallas_call(
        body,
        grid=(b, ni),
        in_specs=[fwd(hk), fwd(hk), fwd(hv), fwd(hk), bwd(hk), bwd(hk), bwd(hv), bwd(hk)],
        out_specs=[fwd(hv), bwd(hv)],
        out_shape=[jax.ShapeDtypeStruct((t, hv), F32), jax.ShapeDtypeStruct((t, hv), F32)],
        scratch_shapes=[state, state],
        compiler_params=_params("parallel", "arbitrary"),
        name="gla",
    )(q, k, v, gf, q, k, v, gb)


def _head_norm(o, g, heads):
    dh = o.shape[1] // heads
    parts = []
    for h in range(heads):
        oh = o[:, h * dh:(h + 1) * dh]
        parts.append(oh * lax.rsqrt(jnp.mean(oh * oh, axis=-1, keepdims=True) + EPS))
    return jnp.concatenate(parts, axis=1) * g


def _ab_out_body(x_ref, yt_ref, wglut_ref, of_ref, ob_ref, og_ref, gn_ref, wo_ref, o_ref, *, su):
    gy = _gelu_tanh(yt_ref[0])
    z = _dot(wglut_ref[...], gy.astype(BF16))
    s5_out = (gy * _sigmoid(z)).T.astype(BF16)
    o = _head_norm(of_ref[...] + ob_ref[...], gn_ref[...], GLA_HEADS)
    gla_out = (o * _silu(og_ref[...].astype(F32))).astype(BF16)
    o_ref[...] = x_ref[...] + _dot(s5_out, wo_ref[:su, :]) + _dot(gla_out, wo_ref[su:, :])


def _ab_out(x2, yt, wglut, o_f, o_b, og, gn, wo, *, b, l, tm):
    t, d = x2.shape
    su = yt.shape[1]
    hv = o_f.shape[1]
    ni = l // tm
    tok = lambda w: pl.BlockSpec((tm, w), lambda bi, i: (bi * ni + i, 0))
    body = functools.partial(_ab_out_body, su=su)
    return pl.pallas_call(
        body,
        grid=(b, ni),
        in_specs=[tok(d), pl.BlockSpec((1, su, tm), lambda bi, i: (bi, 0, i)), _const_spec(wglut.shape),
                  tok(hv), tok(hv), tok(hv), _const_spec((1, hv)), _const_spec(wo.shape)],
        out_specs=tok(d),
        out_shape=jax.ShapeDtypeStruct((t, d), F32),
        compiler_params=_params("parallel", "parallel"),
        name="ab_out",
    )(x2, yt, wglut, o_f, o_b, og, gn.reshape(1, hv), wo)


def _ret_in_body(x_ref, g_ref, w_ref, wkt_ref, cos_ref, sin_ref, cost_ref, sint_ref,
                 q_ref, kt_ref, v_ref, og_ref, *, hk, hv, dk):
    h = _rmsnorm(x_ref[...], g_ref[...]).astype(BF16)
    cos2 = cos_ref[...]
    sin2 = sin_ref[...]
    q = _dot(h, w_ref[:, :hk])
    q_ref[...] = jnp.concatenate(
        [q[:, s:s + dk] * cos2 + pltpu.roll(q[:, s:s + dk], dk // 2, axis=1) * sin2 for s in range(0, hk, dk)],
        axis=1).astype(BF16)
    kt = _dot_nt(wkt_ref[...], h) * dk ** -0.5
    cost = cost_ref[...]
    sint = sint_ref[...]
    parts = []
    for s in range(0, hk, dk):
        t1 = kt[s:s + dk // 2]
        t2 = kt[s + dk // 2:s + dk]
        parts += [t1 * cost - t2 * sint, t1 * sint + t2 * cost]
    kt_ref[0] = jnp.concatenate(parts, axis=0).astype(BF16)
    v_ref[...] = _dot(h, w_ref[:, hk:hk + hv]).astype(BF16)
    og_ref[...] = _dot(h, w_ref[:, hk + hv:]).astype(BF16)


def _ret_in(x2, g, w, wkt, cos2, sin2, cost, sint, *, b, l, tm, hk, hv):
    t, d = x2.shape
    dk = hk // RET_HEADS
    ni = l // tm
    tok = lambda w_: pl.BlockSpec((tm, w_), lambda bi, i: (bi * ni + i, 0))
    pos = pl.BlockSpec((tm, dk), lambda bi, i: (i, 0))
    post = pl.BlockSpec((dk // 2, tm), lambda bi, i: (0, i))
    body = functools.partial(_ret_in_body, hk=hk, hv=hv, dk=dk)
    return pl.pallas_call(
        body,
        grid=(b, ni),
        in_specs=[tok(d), _const_spec((1, d)), _const_spec(w.shape), _const_spec(wkt.shape), pos, pos, post, post],
        out_specs=[tok(hk), pl.BlockSpec((1, hk, tm), lambda bi, i: (bi, 0, i)), tok(hv), tok(hv)],
        out_shape=[jax.ShapeDtypeStruct((t, hk), BF16), jax.ShapeDtypeStruct((b, hk, l), BF16),
                   jax.ShapeDtypeStruct((t, hv), BF16), jax.ShapeDtypeStruct((t, hv), BF16)],
        compiler_params=_params("parallel", "parallel"),
        name="ret_in",
    )(x2, g.reshape(1, d), w, wkt, cos2, sin2, cost, sint)


def _ret_tables(c, dk, dv):
    lg_f = jnp.log1p(-jnp.exp2(-5.0 - jnp.arange(RET_HEADS, dtype=F32)))
    lg_b = lg_f[::-1]
    idx = jnp.arange(c)
    diff = (idx[:, None] - idx[None, :]).astype(F32)
    dmat = jnp.where(diff >= 0, jnp.exp(jnp.maximum(diff, 0.0)[None] * lg_f[:, None, None]),
                     jnp.exp(jnp.maximum(-diff, 0.0)[None] * lg_b[:, None, None]))
    pos = idx.astype(F32)
    ones_v = jnp.ones((1, 1, dv), F32)
    xi_f = jnp.exp((pos + 1.0)[None, :] * lg_f[:, None])[..., None] * ones_v
    xi_b = jnp.exp((c - pos)[None, :] * lg_b[:, None])[..., None] * ones_v
    zeta_f = jnp.exp((c - 1.0 - pos)[None, :] * lg_f[:, None])[:, None, :]
    zeta_b = jnp.exp(pos[None, :] * lg_b[:, None])[:, None, :]
    cd_f = jnp.exp(c * lg_f)[:, None, None] * ones_v
    cd_b = jnp.exp(c * lg_b)[:, None, None] * ones_v
    return dmat, xi_f, xi_b, zeta_f, zeta_b, cd_f, cd_b


def _ret_state_body(kt_ref, v_ref, zb_ref, cdb_ref, sb_out_ref, sb_ref, *, dk, dv):
    @pl.when(pl.program_id(1) == 0)
    def _():
        sb_ref[...] = jnp.zeros(sb_ref.shape, F32)

    for h in range(RET_HEADS):
        st = sb_ref[h]
        sb_out_ref[0, 0, h] = st.astype(BF16)
        kz = (kt_ref[0, h * dk:(h + 1) * dk, :].astype(F32) * zb_ref[h]).astype(BF16)
        sb_ref[h] = cdb_ref[h] * st + _dot(kz, v_ref[:, h * dv:(h + 1) * dv])


def _ret_state(kt, v, zeta_b, cd_b, *, b, l, tb):
    hk, hv = kt.shape[1], v.shape[1]
    dk, dv = hk // RET_HEADS, hv // RET_HEADS
    ni = l // tb
    bwd = lambda w: pl.BlockSpec((tb, w), lambda bi, i: (bi * ni + ni - 1 - i, 0))
    body = functools.partial(_ret_state_body, dk=dk, dv=dv)
    return pl.pallas_call(
        body,
        grid=(b, ni),
        in_specs=[pl.BlockSpec((1, hk, tb), lambda bi, i: (bi, 0, ni - 1 - i)), bwd(hv),
                  _const_spec(zeta_b.shape), _const_spec(cd_b.shape)],
        out_specs=pl.BlockSpec((1, 1, RET_HEADS, dk, dv), lambda bi, i: (bi, ni - 1 - i, 0, 0, 0)),
        out_shape=jax.ShapeDtypeStruct((b, ni, RET_HEADS, dk, dv), BF16),
        scratch_shapes=[pltpu.VMEM((RET_HEADS, dk, dv), F32)],
        compiler_params=_params("parallel", "arbitrary"),
        name="ret_state",
    )(kt, v, zeta_b, cd_b)


def _ret_main_body(x_ref, q_ref, kt_ref, v_ref, og_ref, sb_ref, dmat_ref, xif_ref, xib_ref, zf_ref, cdf_ref,
                   gn_ref, wo_ref, o_ref, sf_ref, gated_ref, *, dk, dv):
    @pl.when(pl.program_id(1) == 0)
    def _():
        sf_ref[...] = jnp.zeros(sf_ref.shape, F32)

    for h in range(RET_HEADS):
        ks = slice(h * dk, (h + 1) * dk)
        vs = slice(h * dv, (h + 1) * dv)
        q = q_ref[:, ks]
        kt = kt_ref[0, ks, :]
        v = v_ref[:, vs]
        s = _dot(q, kt) * dmat_ref[h]
        st = sf_ref[h]
        o = (_dot(s.astype(BF16), v) + _dot(q, st.astype(BF16)) * xif_ref[h]
             + _dot(q, sb_ref[0, 0, h]) * xib_ref[h])
        kz = (kt.astype(F32) * zf_ref[h]).astype(BF16)
        sf_ref[h] = cdf_ref[h] * st + _dot(kz, v)
        o = o * lax.rsqrt(jnp.mean(o * o, axis=-1, keepdims=True) + EPS) * gn_ref[:, vs]
        gated_ref[:, vs] = (o * _silu(og_ref[:, vs].astype(F32))).astype(BF16)
    o_ref[...] = x_ref[...] + _dot(gated_ref[...], wo_ref[...])


def _ret_main(x2, q, kt, v, og, sb, tabs, gn, wo, *, b, l, tb):
    t, d = x2.shape
    hk, hv = q.shape[1], v.shape[1]
    dk, dv = hk // RET_HEADS, hv // RET_HEADS
    ni = l // tb
    tok = lambda w: pl.BlockSpec((tb, w), lambda bi, i: (bi * ni + i, 0))
    body = functools.partial(_ret_main_body, dk=dk, dv=dv)
    return pl.pallas_call(
        body,
        grid=(b, ni),
        in_specs=[tok(d), tok(hk), pl.BlockSpec((1, hk, tb), lambda bi, i: (bi, 0, i)), tok(hv), tok(hv),
                  pl.BlockSpec((1, 1, RET_HEADS, dk, dv), lambda bi, i: (bi, i, 0, 0, 0))]
                 + [_const_spec(a.shape) for a in tabs] + [_const_spec((1, hv)), _const_spec(wo.shape)],
        out_specs=tok(d),
        out_shape=jax.ShapeDtypeStruct((t, d), F32),
        scratch_shapes=[pltpu.VMEM((RET_HEADS, dk, dv), F32), pltpu.VMEM((tb, hv), BF16)],
        compiler_params=_params("parallel", "arbitrary"),
        name="ret_main",
    )(x2, q, kt, v, og, sb, *tabs, gn.reshape(1, hv), wo)


def _tile(l, want):
    t = min(l, want)
    while l % t:
        t //= 2
    return t


def _s5_gla_layer(x2, mix_g, w_in, lam_re, lam_im, b_re, b_im, c_re, c_im, log_dt, d_skip, w_glu,
                  w_gk, b_gk, gla_norm, w_out, *, b, l):
    su = d_skip.shape[0]
    hk = w_gk.shape[-1]
    r = w_gk.shape[1]
    hv = gla_norm.shape[0]
    tm = _tile(l, 1024)
    wut = w_in[:, :su].T.astype(BF16)
    wr = w_in[:, su:].astype(BF16)
    wgk = jnp.zeros((2 * r, 2 * hk), F32)
    wgk = wgk.at[:r, :hk].set(w_gk[0]).at[r:, hk:].set(w_gk[1]).astype(BF16)
    bgk = b_gk.reshape(1, 2 * hk).astype(F32)
    ut, q, k, v, og, gf, gb = _ab_in(x2, mix_g, wut, wr, wgk, bgk, b=b, l=l, tm=tm, hk=hk, hv=hv)

    nc = l // LANES
    taps, w_st, wc, pq = _s5_tables(lam_re, lam_im, b_re, b_im, c_re, c_im, log_dt, d_skip, nc)
    y4 = _s5(ut.reshape(b, su, nc, LANES), taps, w_st, wc, pq)
    yt = y4.reshape(b, su, l)

    o_f, o_b = _gla(q, k, v, gf, gb, b=b, l=l, tb=_tile(l, 512))
    return _ab_out(x2, yt, w_glu.T.astype(BF16), o_f, o_b, og, gla_norm, w_out.astype(BF16), b=b, l=l, tm=tm)


def _retention_layer(x2, mix_g, w_in, ret_norm, w_out, *, b, l):
    hv = ret_norm.shape[0]
    hk = (w_in.shape[1] - 2 * hv) // 2
    dk = hk // RET_HEADS
    half = dk // 2
    tm = _tile(l, 512)
    pos = jnp.arange(l, dtype=F32)
    inv = jnp.exp(-math.log(ROPE_BASE) * jnp.arange(half, dtype=F32) / half)
    ang = pos[:, None] * inv[None, :]
    cos, sin = jnp.cos(ang), jnp.sin(ang)
    cos2 = jnp.concatenate([cos, cos], axis=1)
    sin2 = jnp.concatenate([-sin, sin], axis=1)
    w_qvg = jnp.concatenate([w_in[:, :hk], w_in[:, 2 * hk:]], axis=1).astype(BF16)
    w_kt = w_in[:, hk:2 * hk].T.astype(BF16)
    q, kt, v, og = _ret_in(x2, mix_g, w_qvg, w_kt, cos2, sin2, cos.T, sin.T, b=b, l=l, tm=tm, hk=hk, hv=hv)
    tb = _tile(l, 256)
    dmat, xi_f, xi_b, zeta_f, zeta_b, cd_f, cd_b = _ret_tables(tb, dk, hv // RET_HEADS)
    sb = _ret_state(kt, v, zeta_b, cd_b, b=b, l=l, tb=tb)
    return _ret_main(x2, q, kt, v, og, sb, (dmat, xi_f, xi_b, zeta_f, cd_f), ret_norm, w_out.astype(BF16),
                     b=b, l=l, tb=tb)


def kernel(x, ffn1_norm, ffn1_w1, ffn1_w2, mix_norm, ffn2_norm, ffn2_w1, ffn2_w2, ab_w_in, s5_lambda_re, s5_lambda_im, s5_b_re, s5_b_im, s5_c_re, s5_c_im, s5_log_dt, s5_d, s5_w_glu, gla_w_gk, gla_b_gk, gla_norm, ab_w_out, ret_w_in, ret_norm, ret_w_out, final_norm):
    b, l, d = x.shape
    depth = ffn1_norm.shape[0]
    d_ff = ffn1_w2.shape[1]
    tm = _tile(b * l, 1024)
    f_chunk = 256 if d_ff % 256 == 0 else d_ff
    x2 = x.reshape(b * l, d).astype(F32)
    for i in range(depth):
        j = i // 2
        x2 = _ffn(x2, ffn1_norm[i], ffn1_w1[i].astype(BF16), ffn1_w2[i].astype(BF16), final_norm,
                  final=False, tm=tm, f_chunk=f_chunk)
        if i % 2 == 0:
            x2 = _s5_gla_layer(x2, mix_norm[i], ab_w_in[j], s5_lambda_re[j], s5_lambda_im[j], s5_b_re[j],
                               s5_b_im[j], s5_c_re[j], s5_c_im[j], s5_log_dt[j], s5_d[j], s5_w_glu[j],
                               gla_w_gk[j], gla_b_gk[j], gla_norm[j], ab_w_out[j], b=b, l=l)
        else:
            x2 = _retention_layer(x2, mix_norm[i], ret_w_in[j], ret_norm[j], ret_w_out[j], b=b, l=l)
        x2 = _ffn(x2, ffn2_norm[i], ffn2_w1[i].astype(BF16), ffn2_w2[i].astype(BF16), final_norm,
                  final=(i == depth - 1), tm=tm, f_chunk=f_chunk)
    return x2.reshape(b, l, d)
```

```python
import functools
import math

import jax
import jax.numpy as jnp
from jax import lax
from jax.experimental import pallas as pl
from jax.experimental.pallas import tpu as pltpu

F32 = jnp.float32
BF16 = jnp.bfloat16

EPS = 1e-6
S5_GROUP = 16
S5_STATE = 64
GLA_HEADS = 4
GLA_RANK = 16
GLA_GATE_NORM = 16.0
GLA_CHUNK = 64
RET_HEADS = 8
ROPE_BASE = 10000.0

LANES = 128
VMEM_LIMIT = 56 * 1024 * 1024
HI = lax.Precision.HIGHEST


def _const_spec(shape):
    nd = len(shape)
    return pl.BlockSpec(shape, lambda *_: (0,) * nd, pipeline_mode=pl.Buffered(1))


def _params(*sem):
    return pltpu.CompilerParams(dimension_semantics=sem, vmem_limit_bytes=VMEM_LIMIT)


def _rmsnorm(x, g):
    return x * lax.rsqrt(jnp.mean(x * x, axis=-1, keepdims=True) + EPS) * g


def _sigmoid(x):
    return 1.0 / (1.0 + jnp.exp(-x))


def _silu(x):
    return x * _sigmoid(x)


def _gelu_tanh(x):
    return 0.5 * x * (1.0 + jnp.tanh(math.sqrt(2.0 / math.pi) * (x + 0.044715 * (x * x * x))))


def _log_sigmoid(x):
    return jnp.minimum(x, 0.0) - jnp.log1p(jnp.exp(-jnp.abs(x)))


def _dot(a, b):
    return jnp.dot(a, b, preferred_element_type=F32)


def _dot_nt(a, b):
    return lax.dot_general(a, b, (((1,), (1,)), ((), ())), preferred_element_type=F32)


def _dot_tn(a, b):
    return lax.dot_general(a, b, (((0,), (0,)), ((), ())), preferred_element_type=F32)


def _ffn_apply(x, g, w1_ref, w2_ref):
    d_ff = w2_ref.shape[0]
    f_chunk = 256 if d_ff % 256 == 0 else d_ff
    xn = _rmsnorm(x, g).astype(BF16)
    acc = jnp.zeros(x.shape, F32)
    for f0 in range(0, d_ff, f_chunk):
        gate = _dot(xn, w1_ref[:, f0:f0 + f_chunk])
        up = _dot(xn, w1_ref[:, d_ff + f0:d_ff + f0 + f_chunk])
        act = (_silu(gate) * up).astype(BF16)
        acc = acc + _dot(act, w2_ref[f0:f0 + f_chunk, :])
    return x + 0.5 * acc


def _ffn_body(x_ref, g_ref, w1_ref, w2_ref, fg_ref, o_ref, *, final):
    y = _ffn_apply(x_ref[...], g_ref[...], w1_ref, w2_ref)
    if final:
        y = _rmsnorm(y, fg_ref[...])
    o_ref[...] = y


def _ffn(x2, g, w1, w2, fg, *, final, tm):
    t, d = x2.shape
    body = functools.partial(_ffn_body, final=final)
    return pl.pallas_call(
        body,
        grid=(t // tm,),
        in_specs=[pl.BlockSpec((tm, d), lambda i: (i, 0)),
                  _const_spec((1, d)), _const_spec(w1.shape), _const_spec(w2.shape),
                  _const_spec((1, d))],
        out_specs=pl.BlockSpec((tm, d), lambda i: (i, 0)),
        out_shape=jax.ShapeDtypeStruct((t, d), F32),
        compiler_params=_params("parallel"),
        name="ffn",
    )(x2, g.reshape(1, d), w1, w2, fg.reshape(1, d))


def _ab_in_body(x_ref, fg_ref, w1_ref, w2_ref, g_ref, wut_ref, wr_ref, wgk_ref, bgk_ref,
                x1_ref, ut_ref, q_ref, k_ref, v_ref, og_ref, gf_ref, gb_ref, *, hk, hv):
    x1 = _ffn_apply(x_ref[...], fg_ref[...], w1_ref, w2_ref)
    x1_ref[...] = x1
    h = _rmsnorm(x1, g_ref[...]).astype(BF16)
    ut_ref[0] = _dot_nt(wut_ref[...], h).astype(BF16)
    proj = _dot(h, wr_ref[...])
    q_ref[...] = proj[:, :hk] * (hk // GLA_HEADS) ** -0.5
    k_ref[...] = proj[:, hk:2 * hk]
    v_ref[...] = proj[:, 2 * hk:2 * hk + hv].astype(BF16)
    og_ref[...] = proj[:, 2 * hk + hv:2 * hk + 2 * hv].astype(BF16)
    glo = proj[:, 2 * hk + 2 * hv:].astype(BF16)
    gk = _log_sigmoid(_dot(glo, wgk_ref[...]) + bgk_ref[...]) * (1.0 / GLA_GATE_NORM)
    gf_ref[...] = gk[:, :hk]
    gb_ref[...] = gk[:, hk:]


def _ab_in(x2, ffn, g, wut, wr, wgk, bgk, *, b, l, tm, hk, hv):
    t, d = x2.shape
    su = wut.shape[0]
    ni = l // tm
    fg, w1, w2 = ffn
    tok = lambda w: pl.BlockSpec((tm, w), lambda bi, i: (bi * ni + i, 0))
    body = functools.partial(_ab_in_body, hk=hk, hv=hv)
    return pl.pallas_call(
        body,
        grid=(b, ni),
        in_specs=[tok(d), _const_spec((1, d)), _const_spec(w1.shape), _const_spec(w2.shape),
                  _const_spec((1, d)), _const_spec(wut.shape), _const_spec(wr.shape),
                  _const_spec(wgk.shape), _const_spec(bgk.shape)],
        out_specs=[tok(d), pl.BlockSpec((1, su, tm), lambda bi, i: (bi, 0, i)),
                   tok(hk), tok(hk), tok(hv), tok(hv), tok(hk), tok(hk)],
        out_shape=[jax.ShapeDtypeStruct((t, d), F32), jax.ShapeDtypeStruct((b, su, l), BF16),
                   jax.ShapeDtypeStruct((t, hk), F32), jax.ShapeDtypeStruct((t, hk), F32),
                   jax.ShapeDtypeStruct((t, hv), BF16), jax.ShapeDtypeStruct((t, hv), BF16),
                   jax.ShapeDtypeStruct((t, hk), F32), jax.ShapeDtypeStruct((t, hk), F32)],
        compiler_params=_params("parallel", "parallel"),
        name="ffn_ab_in",
    )(x2, fg.reshape(1, d), w1, w2, g.reshape(1, d), wut, wr, wgk, bgk)


def _cpow_table(ar, ai, count):
    ks = jnp.arange(count)
    pr = jnp.ones(ar.shape + (count,), F32)
    pi = jnp.zeros(ar.shape + (count,), F32)
    sr, si = ar, ai
    for bit in range(max(1, (count - 1).bit_length())):
        on = ((ks >> bit) & 1) == 1
        nr = pr * sr[..., None] - pi * si[..., None]
        ni = pr * si[..., None] + pi * sr[..., None]
        pr = jnp.where(on, nr, pr)
        pi = jnp.where(on, ni, pi)
        sr, si = sr * sr - si * si, 2.0 * sr * si
    return pr, pi


def _s5_tables(lam_re, lam_im, b_re, b_im, c_re, c_im, log_dt, d_skip, n_chunks):
    ch = LANES
    g_, n_ = lam_re.shape[1], lam_re.shape[2]
    p_ = b_re.shape[-1]
    f32 = lambda a: a.astype(F32)
    lr = jnp.minimum(f32(lam_re), -1e-4)
    li = f32(lam_im)
    dt = jnp.exp(f32(log_dt))[..., None]
    mag = jnp.exp(lr * dt)
    ar = mag * jnp.cos(li * dt)
    ai = mag * jnp.sin(li * dt)
    den = lr * lr + li * li
    cr = ((ar - 1.0) * lr + ai * li) / den
    ci = (ai * lr - (ar - 1.0) * li) / den
    br, bi = f32(b_re), f32(b_im)
    bbr = cr[..., None] * br - ci[..., None] * bi
    bbi = cr[..., None] * bi + ci[..., None] * br
    ccr, cci = f32(c_re), f32(c_im)
    pwr, pwi = _cpow_table(ar, ai, ch + 1)

    cnr = ccr.transpose(0, 1, 3, 2)[..., None]
    cni = cci.transpose(0, 1, 3, 2)[..., None]
    cbr = cnr * bbr[:, :, :, None, :] - cni * bbi[:, :, :, None, :]
    cbi = cnr * bbi[:, :, :, None, :] + cni * bbr[:, :, :, None, :]
    taps = (jnp.einsum('dgnpq,dgnl->dgqpl', cbr, pwr[..., :ch], precision=HI)
            - jnp.einsum('dgnpq,dgnl->dgqpl', cbi, pwi[..., :ch], precision=HI))
    kf, kb = taps[0], taps[1]
    dmat = f32(d_skip).reshape(g_, p_)[:, None, :, None] * jnp.eye(p_, dtype=F32)[None, :, :, None]
    center = kf[..., 0:1] + kb[..., 0:1] + dmat
    kfull = jnp.concatenate([kb[..., :0:-1], center, kf[..., 1:]], axis=-1)
    kpad = jnp.pad(kfull, ((0, 0),) * 3 + ((0, 1),))

    def win(d, rev):
        pr = pwr[d][..., :ch]
        pi = pwi[d][..., :ch]
        if rev:
            pr, pi = pr[..., ::-1], pi[..., ::-1]
        pr = pr.transpose(0, 2, 1)[:, None]
        pi = pi.transpose(0, 2, 1)[:, None]
        xr = bbr[d].transpose(0, 2, 1)[:, :, None, :]
        xi = bbi[d].transpose(0, 2, 1)[:, :, None, :]
        return pr * xr - pi * xi, pr * xi + pi * xr
    fre, fim = win(0, True)
    bre, bim = win(1, False)
    w_in = jnp.concatenate([fre, fim, bre, bim], axis=-1).reshape(g_, p_ * ch, 4 * n_).astype(BF16)

    def wout(d, rev):
        pr = pwr[d][..., 1:ch + 1]
        pi = pwi[d][..., 1:ch + 1]
        if rev:
            pr, pi = pr[..., ::-1], pi[..., ::-1]
        pr = pr[:, :, None, :]
        pi = pi[:, :, None, :]
        xr = ccr[d].transpose(0, 2, 1)[..., None]
        xi = cci[d].transpose(0, 2, 1)[..., None]
        return xr * pr - xi * pi, -(xr * pi + xi * pr)
    fre, fim = wout(0, False)
    bre, bim = wout(1, True)
    wc = jnp.concatenate([fre, fim, bre, bim], axis=1).reshape(g_, 4 * n_, p_ * ch).astype(BF16)

    steps = max(1, (n_chunks - 1).bit_length())
    sr, si = pwr[..., ch], pwi[..., ch]
    rows = []
    for _ in range(steps):
        rows.append(jnp.stack([jnp.concatenate([sr, sr], -1), jnp.concatenate([-si, si], -1)], axis=2))
        sr, si = sr * sr - si * si, 2.0 * sr * si
    pq = jnp.stack(rows, axis=2)
    pq = pq.transpose(1, 0, 2, 3, 4)
    return kpad, w_in, wc, pq


def _s5_body(u_ref, taps_ref, win_ref, wc_ref, pq_ref, y_ref, toep_ref, *, nb, nc, p, n, steps):
    ch = LANES
    m = nb * nc
    for q in range(p):
        for pp in range(p):
            row = jnp.broadcast_to(taps_ref[0, q, pp:pp + 1, :], (ch, 2 * ch))
            blk = pltpu.roll(row, ch + 1, 1, stride=1, stride_axis=0)[:, :ch]
            toep_ref[q * ch:(q + 1) * ch, pp * ch:(pp + 1) * ch] = blk.astype(BF16)
    u = jnp.concatenate([u_ref[:, q].reshape(m, ch) for q in range(p)], axis=1)
    y = _dot(u, toep_ref[...])
    r = _dot(u, win_ref[0])
    c = lax.broadcasted_iota(jnp.int32, (m, 2 * n), 0) % nc

    def scan(x, d):
        for k in range(steps):
            s = 1 << k
            if d == 0:
                t = jnp.where(c >= s, pltpu.roll(x, s, axis=0), 0.0)
            else:
                t = jnp.where(c < nc - s, pltpu.roll(x, m - s, axis=0), 0.0)
            x = x + t * pq_ref[0, d, k, 0:1, :] + pltpu.roll(t, n, axis=1) * pq_ref[0, d, k, 1:2, :]
        if d == 0:
            return jnp.where(c >= 1, pltpu.roll(x, 1, axis=0), 0.0)
        return jnp.where(c < nc - 1, pltpu.roll(x, m - 1, axis=0), 0.0)

    xf = scan(r[:, :2 * n], 0)
    xb = scan(r[:, 2 * n:], 1)
    xc = jnp.concatenate([xf, xb], axis=1).astype(BF16)
    y = y + _dot(xc, wc_ref[0])
    for q in range(p):
        y_ref[:, q] = y[:, q * ch:(q + 1) * ch].reshape(nb, nc, ch)


def _s5(u4, taps, w_in, wc, pq):
    nb, su, nc, ch = u4.shape
    g_ = taps.shape[0]
    p = su // g_
    n = wc.shape[1] // 4
    steps = pq.shape[2]
    body = functools.partial(_s5_body, nb=nb, nc=nc, p=p, n=n, steps=steps)
    return pl.pallas_call(
        body,
        grid=(g_,),
        in_specs=[pl.BlockSpec((nb, p, nc, ch), lambda g: (0, g, 0, 0)),
                  pl.BlockSpec((1,) + taps.shape[1:], lambda g: (g, 0, 0, 0)),
                  pl.BlockSpec((1,) + w_in.shape[1:], lambda g: (g, 0, 0)),
                  pl.BlockSpec((1,) + wc.shape[1:], lambda g: (g, 0, 0)),
                  pl.BlockSpec((1,) + pq.shape[1:], lambda g: (g, 0, 0, 0, 0))],
        out_specs=pl.BlockSpec((nb, p, nc, ch), lambda g: (0, g, 0, 0)),
        out_shape=jax.ShapeDtypeStruct((nb, su, nc, ch), F32),
        scratch_shapes=[pltpu.VMEM((p * ch, p * ch), BF16)],
        compiler_params=_params("parallel"),
        name="s5",
    )(u4, taps, w_in, wc, pq)


def _gla_prep(q, k, v, g, tri, cmask, bdk, bd, ones, last_row):
    g_hi = g.astype(BF16)
    g_lo = (g - g_hi.astype(F32)).astype(BF16)
    cum = _dot(tri, g_hi) + _dot(tri, g_lo)
    last = cum[last_row:last_row + 1, :]
    q_dec = (q * jnp.exp(cum)).astype(BF16)
    k_inv = (k * jnp.exp(-cum)).astype(BF16)
    k_dec = (k * jnp.exp(last - cum)).astype(BF16)
    kb = jnp.where(bdk, jnp.concatenate([k_inv] * GLA_HEADS, axis=0), 0.0)
    s = jnp.where(cmask, _dot_nt(q_dec, kb), 0.0).astype(BF16)
    vb = jnp.where(bd, jnp.concatenate([v] * GLA_HEADS, axis=0), 0.0)
    o_intra = _dot(s, vb)
    tot = _dot_tn(g_hi, ones) + _dot_tn(g_lo, ones)
    kv = jnp.where(bd, _dot_tn(k_dec, v), 0.0)
    return q_dec, o_intra, kv, jnp.exp(tot)


def _gla_sweep(preps, order, st_ref, o_ref, c):
    st = st_ref[...]
    for n in order:
        q_dec, o_intra, kv, decay = preps[n]
        o_ref[n * c:(n + 1) * c, :] = o_intra + _dot(q_dec, st.astype(BF16))
        st = jnp.concatenate([decay] * (st.shape[1] // LANES), axis=1) * st + kv
    st_ref[...] = st


def _gla_body(qf_ref, kf_ref, vf_ref, gf_ref, qb_ref, kb_ref, vb_ref, gb_ref,
              of_ref, ob_ref, sf_ref, sb_ref, *, tb, dk, dv):
    c = GLA_CHUNK

    @pl.when(pl.program_id(1) == 0)
    def _():
        sf_ref[...] = jnp.zeros(sf_ref.shape, F32)
        sb_ref[...] = jnp.zeros(sb_ref.shape, F32)

    ii = lax.broadcasted_iota(jnp.int32, (c, c), 0)
    jj = lax.broadcasted_iota(jnp.int32, (c, c), 1)
    tri_f = (jj <= ii).astype(BF16)
    tri_b = (jj >= ii).astype(BF16)
    hc = GLA_HEADS * c
    i2 = lax.broadcasted_iota(jnp.int32, (c, hc), 0)
    j2 = lax.broadcasted_iota(jnp.int32, (c, hc), 1) % c
    cmask_f = j2 <= i2
    cmask_b = j2 > i2
    bdk = (lax.broadcasted_iota(jnp.int32, (hc, GLA_HEADS * dk), 0) // c
           == lax.broadcasted_iota(jnp.int32, (hc, GLA_HEADS * dk), 1) // dk)
    bd = (lax.broadcasted_iota(jnp.int32, (hc, GLA_HEADS * dv), 0) // c
          == lax.broadcasted_iota(jnp.int32, (hc, GLA_HEADS * dv), 1) // dv)
    ones = jnp.ones((c, LANES), BF16)
    nch = tb // c
    preps_f, preps_b = [], []
    for n in range(nch):
        r = slice(n * c, (n + 1) * c)
        preps_f.append(_gla_prep(qf_ref[r, :], kf_ref[r, :], vf_ref[r, :], gf_ref[r, :],
                                 tri_f, cmask_f, bdk, bd, ones, c - 1))
        preps_b.append(_gla_prep(qb_ref[r, :], kb_ref[r, :], vb_ref[r, :], gb_ref[r, :],
                                 tri_b, cmask_b, bdk, bd, ones, 0))
    _gla_sweep(preps_f, range(nch), sf_ref, of_ref, c)
    _gla_sweep(preps_b, range(nch - 1, -1, -1), sb_ref, ob_ref, c)


def _gla(q, k, v, gf, gb, *, b, l, tb):
    t, hk = q.shape
    hv = v.shape[1]
    dk, dv = hk // GLA_HEADS, hv // GLA_HEADS
    ni = l // tb
    fwd = lambda w: pl.BlockSpec((tb, w), lambda bi, i: (bi * ni + i, 0))
    bwd = lambda w: pl.BlockSpec((tb, w), lambda bi, i: (bi * ni + ni - 1 - i, 0))
    assert dk == GLA_CHUNK, "state mask reuse needs chunk == key head width"
    body = functools.partial(_gla_body, tb=tb, dk=dk, dv=dv)
    state = pltpu.VMEM((hk, hv), F32)
    return pl.pallas_call(
        body,
        grid=(b, ni),
        in_specs=[fwd(hk), fwd(hk), fwd(hv), fwd(hk), bwd(hk), bwd(hk), bwd(hv), bwd(hk)],
        out_specs=[fwd(hv), bwd(hv)],
        out_shape=[jax.ShapeDtypeStruct((t, hv), F32), jax.ShapeDtypeStruct((t, hv), F32)],
        scratch_shapes=[state, state],
        compiler_params=_params("parallel", "arbitrary"),
        name="gla",
    )(q, k, v, gf, q, k, v, gb)


def _head_norm(o, g, heads):
    dh = o.shape[1] // heads
    parts = []
    for h in range(heads):
        oh = o[:, h * dh:(h + 1) * dh]
        parts.append(oh * lax.rsqrt(jnp.mean(oh * oh, axis=-1, keepdims=True) + EPS))
    return jnp.concatenate(parts, axis=1) * g


def _ab_out_body(x_ref, yt_ref, wglut_ref, of_ref, ob_ref, og_ref, gn_ref, wo_ref, fg_ref, w1_ref, w2_ref,
                 o_ref, *, su):
    gy = _gelu_tanh(yt_ref[0])
    z = _dot(wglut_ref[...], gy.astype(BF16))
    s5_out = (gy * _sigmoid(z)).T.astype(BF16)
    o = _head_norm(of_ref[...] + ob_ref[...], gn_ref[...], GLA_HEADS)
    gla_out = (o * _silu(og_ref[...].astype(F32))).astype(BF16)
    x1 = x_ref[...] + _dot(s5_out, wo_ref[:su, :]) + _dot(gla_out, wo_ref[su:, :])
    o_ref[...] = _ffn_apply(x1, fg_ref[...], w1_ref, w2_ref)


def _ab_out(x2, yt, wglut, o_f, o_b, og, gn, wo, ffn, *, b, l, tm):
    t, d = x2.shape
    su = yt.shape[1]
    hv = o_f.shape[1]
    ni = l // tm
    fg, w1, w2 = ffn
    tok = lambda w: pl.BlockSpec((tm, w), lambda bi, i: (bi * ni + i, 0))
    body = functools.partial(_ab_out_body, su=su)
    return pl.pallas_call(
        body,
        grid=(b, ni),
        in_specs=[tok(d), pl.BlockSpec((1, su, tm), lambda bi, i: (bi, 0, i)), _const_spec(wglut.shape),
                  tok(hv), tok(hv), tok(hv), _const_spec((1, hv)), _const_spec(wo.shape),
                  _const_spec((1, d)), _const_spec(w1.shape), _const_spec(w2.shape)],
        out_specs=tok(d),
        out_shape=jax.ShapeDtypeStruct((t, d), F32),
        compiler_params=_params("parallel", "parallel"),
        name="ab_out_ffn",
    )(x2, yt, wglut, o_f, o_b, og, gn.reshape(1, hv), wo, fg.reshape(1, d), w1, w2)


def _ret_in_body(x_ref, g_ref, w_ref, wkt_ref, cos_ref, sin_ref, cost_ref, sint_ref,
                 q_ref, kt_ref, v_ref, og_ref, *, hk, hv, dk):
    h = _rmsnorm(x_ref[...], g_ref[...]).astype(BF16)
    cos2 = cos_ref[...]
    sin2 = sin_ref[...]
    q = _dot(h, w_ref[:, :hk])
    q_ref[...] = jnp.concatenate(
        [q[:, s:s + dk] * cos2 + pltpu.roll(q[:, s:s + dk], dk // 2, axis=1) * sin2 for s in range(0, hk, dk)],
        axis=1).astype(BF16)
    kt = _dot_nt(wkt_ref[...], h) * dk ** -0.5
    cost = cost_ref[...]
    sint = sint_ref[...]
    parts = []
    for s in range(0, hk, dk):
        t1 = kt[s:s + dk // 2]
        t2 = kt[s + dk // 2:s + dk]
        parts += [t1 * cost - t2 * sint, t1 * sint + t2 * cost]
    kt_ref[0] = jnp.concatenate(parts, axis=0).astype(BF16)
    v_ref[...] = _dot(h, w_ref[:, hk:hk + hv]).astype(BF16)
    og_ref[...] = _dot(h, w_ref[:, hk + hv:]).astype(BF16)


def _ret_in(x2, g, w, wkt, cos2, sin2, cost, sint, *, b, l, tm, hk, hv):
    t, d = x2.shape
    dk = hk // RET_HEADS
    ni = l // tm
    tok = lambda w_: pl.BlockSpec((tm, w_), lambda bi, i: (bi * ni + i, 0))
    pos = pl.BlockSpec((tm, dk), lambda bi, i: (i, 0))
    post = pl.BlockSpec((dk // 2, tm), lambda bi, i: (0, i))
    body = functools.partial(_ret_in_body, hk=hk, hv=hv, dk=dk)
    return pl.pallas_call(
        body,
        grid=(b, ni),
        in_specs=[tok(d), _const_spec((1, d)), _const_spec(w.shape), _const_spec(wkt.shape), pos, pos, post, post],
        out_specs=[tok(hk), pl.BlockSpec((1, hk, tm), lambda bi, i: (bi, 0, i)), tok(hv), tok(hv)],
        out_shape=[jax.ShapeDtypeStruct((t, hk), BF16), jax.ShapeDtypeStruct((b, hk, l), BF16),
                   jax.ShapeDtypeStruct((t, hv), BF16), jax.ShapeDtypeStruct((t, hv), BF16)],
        compiler_params=_params("parallel", "parallel"),
        name="ret_in",
    )(x2, g.reshape(1, d), w, wkt, cos2, sin2, cost, sint)


def _ret_tables(c, dk, dv):
    lg_f = jnp.log1p(-jnp.exp2(-5.0 - jnp.arange(RET_HEADS, dtype=F32)))
    lg_b = lg_f[::-1]
    idx = jnp.arange(c)
    diff = (idx[:, None] - idx[None, :]).astype(F32)
    dmat = jnp.where(diff >= 0, jnp.exp(jnp.maximum(diff, 0.0)[None] * lg_f[:, None, None]),
                     jnp.exp(jnp.maximum(-diff, 0.0)[None] * lg_b[:, None, None]))
    pos = idx.astype(F32)
    ones_v = jnp.ones((1, 1, dv), F32)
    xi_f = jnp.exp((pos + 1.0)[None, :] * lg_f[:, None])[..., None] * ones_v
    xi_b = jnp.exp((c - pos)[None, :] * lg_b[:, None])[..., None] * ones_v
    zeta_f = jnp.exp((c - 1.0 - pos)[None, :] * lg_f[:, None])[:, None, :]
    zeta_b = jnp.exp(pos[None, :] * lg_b[:, None])[:, None, :]
    cd_f = jnp.exp(c * lg_f)[:, None, None] * ones_v
    cd_b = jnp.exp(c * lg_b)[:, None, None] * ones_v
    return dmat, xi_f, xi_b, zeta_f, zeta_b, cd_f, cd_b


def _ret_state_body(kt_ref, v_ref, zb_ref, cdb_ref, sb_out_ref, sb_ref, *, dk, dv):
    @pl.when(pl.program_id(1) == 0)
    def _():
        sb_ref[...] = jnp.zeros(sb_ref.shape, F32)

    for h in range(RET_HEADS):
        st = sb_ref[h]
        sb_out_ref[0, 0, h] = st.astype(BF16)
        kz = (kt_ref[0, h * dk:(h + 1) * dk, :].astype(F32) * zb_ref[h]).astype(BF16)
        sb_ref[h] = cdb_ref[h] * st + _dot(kz, v_ref[:, h * dv:(h + 1) * dv])


def _ret_state(kt, v, zeta_b, cd_b, *, b, l, tb):
    hk, hv = kt.shape[1], v.shape[1]
    dk, dv = hk // RET_HEADS, hv // RET_HEADS
    ni = l // tb
    bwd = lambda w: pl.BlockSpec((tb, w), lambda bi, i: (bi * ni + ni - 1 - i, 0))
    body = functools.partial(_ret_state_body, dk=dk, dv=dv)
    return pl.pallas_call(
        body,
        grid=(b, ni),
        in_specs=[pl.BlockSpec((1, hk, tb), lambda bi, i: (bi, 0, ni - 1 - i)), bwd(hv),
                  _const_spec(zeta_b.shape), _const_spec(cd_b.shape)],
        out_specs=pl.BlockSpec((1, 1, RET_HEADS, dk, dv), lambda bi, i: (bi, ni - 1 - i, 0, 0, 0)),
        out_shape=jax.ShapeDtypeStruct((b, ni, RET_HEADS, dk, dv), BF16),
        scratch_shapes=[pltpu.VMEM((RET_HEADS, dk, dv), F32)],
        compiler_params=_params("parallel", "arbitrary"),
        name="ret_state",
    )(kt, v, zeta_b, cd_b)


def _ret_main_body(x_ref, q_ref, kt_ref, v_ref, og_ref, sb_ref, dmat_ref, xif_ref, xib_ref, zf_ref, cdf_ref,
                   gn_ref, wo_ref, o_ref, sf_ref, gated_ref, *, dk, dv):
    @pl.when(pl.program_id(1) == 0)
    def _():
        sf_ref[...] = jnp.zeros(sf_ref.shape, F32)

    for h in range(RET_HEADS):
        ks = slice(h * dk, (h + 1) * dk)
        vs = slice(h * dv, (h + 1) * dv)
        q = q_ref[:, ks]
        kt = kt_ref[0, ks, :]
        v = v_ref[:, vs]
        s = _dot(q, kt) * dmat_ref[h]
        st = sf_ref[h]
        o = (_dot(s.astype(BF16), v) + _dot(q, st.astype(BF16)) * xif_ref[h]
             + _dot(q, sb_ref[0, 0, h]) * xib_ref[h])
        kz = (kt.astype(F32) * zf_ref[h]).astype(BF16)
        sf_ref[h] = cdf_ref[h] * st + _dot(kz, v)
        o = o * lax.rsqrt(jnp.mean(o * o, axis=-1, keepdims=True) + EPS) * gn_ref[:, vs]
        gated_ref[:, vs] = (o * _silu(og_ref[:, vs].astype(F32))).astype(BF16)
    o_ref[...] = x_ref[...] + _dot(gated_ref[...], wo_ref[...])


def _ret_main(x2, q, kt, v, og, sb, tabs, gn, wo, *, b, l, tb):
    t, d = x2.shape
    hk, hv = q.shape[1], v.shape[1]
    dk, dv = hk // RET_HEADS, hv // RET_HEADS
    ni = l // tb
    tok = lambda w: pl.BlockSpec((tb, w), lambda bi, i: (bi * ni + i, 0))
    body = functools.partial(_ret_main_body, dk=dk, dv=dv)
    return pl.pallas_call(
        body,
        grid=(b, ni),
        in_specs=[tok(d), tok(hk), pl.BlockSpec((1, hk, tb), lambda bi, i: (bi, 0, i)), tok(hv), tok(hv),
                  pl.BlockSpec((1, 1, RET_HEADS, dk, dv), lambda bi, i: (bi, i, 0, 0, 0))]
                 + [_const_spec(a.shape) for a in tabs] + [_const_spec((1, hv)), _const_spec(wo.shape)],
        out_specs=tok(d),
        out_shape=jax.ShapeDtypeStruct((t, d), F32),
        scratch_shapes=[pltpu.VMEM((RET_HEADS, dk, dv), F32), pltpu.VMEM((tb, hv), BF16)],
        compiler_params=_params("parallel", "arbitrary"),
        name="ret_main",
    )(x2, q, kt, v, og, sb, *tabs, gn.reshape(1, hv), wo)


def _tile(l, want):
    t = min(l, want)
    while l % t:
        t //= 2
    return t


def _s5_gla_layer(x2, ffn1, mix_g, w_in, lam_re, lam_im, b_re, b_im, c_re, c_im, log_dt, d_skip, w_glu,
                  w_gk, b_gk, gla_norm, w_out, ffn2, *, b, l):
    su = d_skip.shape[0]
    hk = w_gk.shape[-1]
    r = w_gk.shape[1]
    hv = gla_norm.shape[0]
    tm = _tile(l, 512)
    wut = w_in[:, :su].T.astype(BF16)
    wr = w_in[:, su:].astype(BF16)
    wgk = jnp.zeros((2 * r, 2 * hk), F32)
    wgk = wgk.at[:r, :hk].set(w_gk[0]).at[r:, hk:].set(w_gk[1]).astype(BF16)
    bgk = b_gk.reshape(1, 2 * hk).astype(F32)
    x2, ut, q, k, v, og, gf, gb = _ab_in(x2, ffn1, mix_g, wut, wr, wgk, bgk, b=b, l=l, tm=tm, hk=hk, hv=hv)

    nc = l // LANES
    taps, w_st, wc, pq = _s5_tables(lam_re, lam_im, b_re, b_im, c_re, c_im, log_dt, d_skip, nc)
    y4 = _s5(ut.reshape(b, su, nc, LANES), taps, w_st, wc, pq)
    yt = y4.reshape(b, su, l)

    o_f, o_b = _gla(q, k, v, gf, gb, b=b, l=l, tb=_tile(l, 512))
    return _ab_out(x2, yt, w_glu.T.astype(BF16), o_f, o_b, og, gla_norm, w_out.astype(BF16), ffn2,
                   b=b, l=l, tm=tm)


def _retention_layer(x2, mix_g, w_in, ret_norm, w_out, *, b, l):
    hv = ret_norm.shape[0]
    hk = (w_in.shape[1] - 2 * hv) // 2
    dk = hk // RET_HEADS
    half = dk // 2
    tm = _tile(l, 512)
    pos = jnp.arange(l, dtype=F32)
    inv = jnp.exp(-math.log(ROPE_BASE) * jnp.arange(half, dtype=F32) / half)
    ang = pos[:, None] * inv[None, :]
    cos, sin = jnp.cos(ang), jnp.sin(ang)
    cos2 = jnp.concatenate([cos, cos], axis=1)
    sin2 = jnp.concatenate([-sin, sin], axis=1)
    w_qvg = jnp.concatenate([w_in[:, :hk], w_in[:, 2 * hk:]], axis=1).astype(BF16)
    w_kt = w_in[:, hk:2 * hk].T.astype(BF16)
    q, kt, v, og = _ret_in(x2, mix_g, w_qvg, w_kt, cos2, sin2, cos.T, sin.T, b=b, l=l, tm=tm, hk=hk, hv=hv)
    tb = _tile(l, 256)
    dmat, xi_f, xi_b, zeta_f, zeta_b, cd_f, cd_b = _ret_tables(tb, dk, hv // RET_HEADS)
    sb = _ret_state(kt, v, zeta_b, cd_b, b=b, l=l, tb=tb)
    return _ret_main(x2, q, kt, v, og, sb, (dmat, xi_f, xi_b, zeta_f, cd_f), ret_norm, w_out.astype(BF16),
                     b=b, l=l, tb=tb)


def kernel(x, ffn1_norm, ffn1_w1, ffn1_w2, mix_norm, ffn2_norm, ffn2_w1, ffn2_w2, ab_w_in, s5_lambda_re, s5_lambda_im, s5_b_re, s5_b_im, s5_c_re, s5_c_im, s5_log_dt, s5_d, s5_w_glu, gla_w_gk, gla_b_gk, gla_norm, ab_w_out, ret_w_in, ret_norm, ret_w_out, final_norm):
    b, l, d = x.shape
    depth = ffn1_norm.shape[0]
    assert depth % 2 == 0, "the final norm is applied by the retention layer's second FFN"
    tm = _tile(b * l, 1024)
    x2 = x.reshape(b * l, d).astype(F32)
    for i in range(depth):
        j = i // 2
        ffn1 = (ffn1_norm[i], ffn1_w1[i].astype(BF16), ffn1_w2[i].astype(BF16))
        ffn2 = (ffn2_norm[i], ffn2_w1[i].astype(BF16), ffn2_w2[i].astype(BF16))
        if i % 2 == 0:
            x2 = _s5_gla_layer(x2, ffn1, mix_norm[i], ab_w_in[j], s5_lambda_re[j], s5_lambda_im[j], s5_b_re[j],
                               s5_b_im[j], s5_c_re[j], s5_c_im[j], s5_log_dt[j], s5_d[j], s5_w_glu[j],
                               gla_w_gk[j], gla_b_gk[j], gla_norm[j], ab_w_out[j], ffn2, b=b, l=l)
        else:
            x2 = _ffn(x2, *ffn1, final_norm, final=False, tm=tm)
            x2 = _retention_layer(x2, mix_norm[i], ret_w_in[j], ret_norm[j], ret_w_out[j], b=b, l=l)
            x2 = _ffn(x2, *ffn2, final_norm, final=(i == depth - 1), tm=tm)
    return x2.reshape(b, l, d)
```

```python
import functools
import math

import jax
import jax.numpy as jnp
from jax import lax
from jax.experimental import pallas as pl
from jax.experimental.pallas import tpu as pltpu

F32 = jnp.float32
BF16 = jnp.bfloat16

EPS = 1e-6
S5_GROUP = 16
S5_STATE = 64
GLA_HEADS = 4
GLA_RANK = 16
GLA_GATE_NORM = 16.0
GLA_CHUNK = 64
RET_HEADS = 8
ROPE_BASE = 10000.0

LANES = 128
VMEM_LIMIT = 56 * 1024 * 1024
HI = lax.Precision.HIGHEST


def _const_spec(shape):
    nd = len(shape)
    return pl.BlockSpec(shape, lambda *_: (0,) * nd, pipeline_mode=pl.Buffered(1))


def _params(*sem):
    return pltpu.CompilerParams(dimension_semantics=sem, vmem_limit_bytes=VMEM_LIMIT)


def _rmsnorm(x, g):
    return x * lax.rsqrt(jnp.mean(x * x, axis=-1, keepdims=True) + EPS) * g


def _sigmoid(x):
    return 1.0 / (1.0 + jnp.exp(-x))


def _silu(x):
    return x * _sigmoid(x)


def _gelu_tanh(x):
    return 0.5 * x * (1.0 + jnp.tanh(math.sqrt(2.0 / math.pi) * (x + 0.044715 * (x * x * x))))


def _log_sigmoid(x):
    return jnp.minimum(x, 0.0) - jnp.log1p(jnp.exp(-jnp.abs(x)))


def _dot(a, b):
    return jnp.dot(a, b, preferred_element_type=F32)


def _dot_nt(a, b):
    return lax.dot_general(a, b, (((1,), (1,)), ((), ())), preferred_element_type=F32)


def _dot_tn(a, b):
    return lax.dot_general(a, b, (((0,), (0,)), ((), ())), preferred_element_type=F32)


def _ffn_apply(x, g, w1_ref, w2_ref):
    d_ff = w2_ref.shape[0]
    f_chunk = 256 if d_ff % 256 == 0 else d_ff
    xn = _rmsnorm(x, g).astype(BF16)
    acc = jnp.zeros(x.shape, F32)
    for f0 in range(0, d_ff, f_chunk):
        gate = _dot(xn, w1_ref[:, f0:f0 + f_chunk])
        up = _dot(xn, w1_ref[:, d_ff + f0:d_ff + f0 + f_chunk])
        act = (_silu(gate) * up).astype(BF16)
        acc = acc + _dot(act, w2_ref[f0:f0 + f_chunk, :])
    return x + 0.5 * acc


def _ffn_body(x_ref, g_ref, w1_ref, w2_ref, fg_ref, o_ref, *, final):
    y = _ffn_apply(x_ref[...], g_ref[...], w1_ref, w2_ref)
    if final:
        y = _rmsnorm(y, fg_ref[...])
    o_ref[...] = y


def _ffn(x2, g, w1, w2, fg, *, final, tm):
    t, d = x2.shape
    body = functools.partial(_ffn_body, final=final)
    return pl.pallas_call(
        body,
        grid=(t // tm,),
        in_specs=[pl.BlockSpec((tm, d), lambda i: (i, 0)),
                  _const_spec((1, d)), _const_spec(w1.shape), _const_spec(w2.shape),
                  _const_spec((1, d))],
        out_specs=pl.BlockSpec((tm, d), lambda i: (i, 0)),
        out_shape=jax.ShapeDtypeStruct((t, d), F32),
        compiler_params=_params("parallel"),
        name="ffn",
    )(x2, g.reshape(1, d), w1, w2, fg.reshape(1, d))


def _ab_in_body(x_ref, g_ref, wut_ref, wr_ref, wgk_ref, bgk_ref,
                ut_ref, q_ref, k_ref, v_ref, og_ref, gf_ref, gb_ref, *, hk, hv):
    h = _rmsnorm(x_ref[...], g_ref[...]).astype(BF16)
    ut_ref[0] = _dot_nt(wut_ref[...], h).astype(BF16)
    proj = _dot(h, wr_ref[...])
    q_ref[...] = proj[:, :hk] * (hk // GLA_HEADS) ** -0.5
    k_ref[...] = proj[:, hk:2 * hk]
    v_ref[...] = proj[:, 2 * hk:2 * hk + hv].astype(BF16)
    og_ref[...] = proj[:, 2 * hk + hv:2 * hk + 2 * hv].astype(BF16)
    glo = proj[:, 2 * hk + 2 * hv:].astype(BF16)
    gk = _log_sigmoid(_dot(glo, wgk_ref[...]) + bgk_ref[...]) * (1.0 / GLA_GATE_NORM)
    gf_ref[...] = gk[:, :hk]
    gb_ref[...] = gk[:, hk:]


def _ab_in(x2, g, wut, wr, wgk, bgk, *, b, l, tm, hk, hv):
    t, d = x2.shape
    su = wut.shape[0]
    ni = l // tm
    tok = lambda w: pl.BlockSpec((tm, w), lambda bi, i: (bi * ni + i, 0))
    body = functools.partial(_ab_in_body, hk=hk, hv=hv)
    return pl.pallas_call(
        body,
        grid=(b, ni),
        in_specs=[tok(d), _const_spec((1, d)), _const_spec(wut.shape), _const_spec(wr.shape),
                  _const_spec(wgk.shape), _const_spec(bgk.shape)],
        out_specs=[pl.BlockSpec((1, su, tm), lambda bi, i: (bi, 0, i)),
                   tok(hk), tok(hk), tok(hv), tok(hv), tok(hk), tok(hk)],
        out_shape=[jax.ShapeDtypeStruct((b, su, l), BF16),
                   jax.ShapeDtypeStruct((t, hk), F32), jax.ShapeDtypeStruct((t, hk), F32),
                   jax.ShapeDtypeStruct((t, hv), BF16), jax.ShapeDtypeStruct((t, hv), BF16),
                   jax.ShapeDtypeStruct((t, hk), F32), jax.ShapeDtypeStruct((t, hk), F32)],
        compiler_params=_params("parallel", "parallel"),
        name="ab_in",
    )(x2, g.reshape(1, d), wut, wr, wgk, bgk)


def _cpow_table(ar, ai, count):
    ks = jnp.arange(count)
    pr = jnp.ones(ar.shape + (count,), F32)
    pi = jnp.zeros(ar.shape + (count,), F32)
    sr, si = ar, ai
    for bit in range(max(1, (count - 1).bit_length())):
        on = ((ks >> bit) & 1) == 1
        nr = pr * sr[..., None] - pi * si[..., None]
        ni = pr * si[..., None] + pi * sr[..., None]
        pr = jnp.where(on, nr, pr)
        pi = jnp.where(on, ni, pi)
        sr, si = sr * sr - si * si, 2.0 * sr * si
    return pr, pi


def _s5_tables(lam_re, lam_im, b_re, b_im, c_re, c_im, log_dt, d_skip, n_chunks):
    ch = LANES
    g_, n_ = lam_re.shape[1], lam_re.shape[2]
    p_ = b_re.shape[-1]
    f32 = lambda a: a.astype(F32)
    lr = jnp.minimum(f32(lam_re), -1e-4)
    li = f32(lam_im)
    dt = jnp.exp(f32(log_dt))[..., None]
    mag = jnp.exp(lr * dt)
    ar = mag * jnp.cos(li * dt)
    ai = mag * jnp.sin(li * dt)
    den = lr * lr + li * li
    cr = ((ar - 1.0) * lr + ai * li) / den
    ci = (ai * lr - (ar - 1.0) * li) / den
    br, bi = f32(b_re), f32(b_im)
    bbr = cr[..., None] * br - ci[..., None] * bi
    bbi = cr[..., None] * bi + ci[..., None] * br
    ccr, cci = f32(c_re), f32(c_im)
    pwr, pwi = _cpow_table(ar, ai, ch + 1)

    cnr = ccr.transpose(0, 1, 3, 2)[..., None]
    cni = cci.transpose(0, 1, 3, 2)[..., None]
    cbr = cnr * bbr[:, :, :, None, :] - cni * bbi[:, :, :, None, :]
    cbi = cnr * bbi[:, :, :, None, :] + cni * bbr[:, :, :, None, :]
    taps = (jnp.einsum('dgnpq,dgnl->dgqpl', cbr, pwr[..., :ch], precision=HI)
            - jnp.einsum('dgnpq,dgnl->dgqpl', cbi, pwi[..., :ch], precision=HI))
    kf, kb = taps[0], taps[1]
    dmat = f32(d_skip).reshape(g_, p_)[:, None, :, None] * jnp.eye(p_, dtype=F32)[None, :, :, None]
    center = kf[..., 0:1] + kb[..., 0:1] + dmat
    kfull = jnp.concatenate([kb[..., :0:-1], center, kf[..., 1:]], axis=-1)
    kpad = jnp.pad(kfull, ((0, 0),) * 3 + ((0, 1),))

    def win(d, rev):
        pr = pwr[d][..., :ch]
        pi = pwi[d][..., :ch]
        if rev:
            pr, pi = pr[..., ::-1], pi[..., ::-1]
        pr = pr.transpose(0, 2, 1)[:, None]
        pi = pi.transpose(0, 2, 1)[:, None]
        xr = bbr[d].transpose(0, 2, 1)[:, :, None, :]
        xi = bbi[d].transpose(0, 2, 1)[:, :, None, :]
        return pr * xr - pi * xi, pr * xi + pi * xr
    fre, fim = win(0, True)
    bre, bim = win(1, False)
    w_in = jnp.concatenate([fre, fim, bre, bim], axis=-1).reshape(g_, p_ * ch, 4 * n_).astype(BF16)

    def wout(d, rev):
        pr = pwr[d][..., 1:ch + 1]
        pi = pwi[d][..., 1:ch + 1]
        if rev:
            pr, pi = pr[..., ::-1], pi[..., ::-1]
        pr = pr[:, :, None, :]
        pi = pi[:, :, None, :]
        xr = ccr[d].transpose(0, 2, 1)[..., None]
        xi = cci[d].transpose(0, 2, 1)[..., None]
        return xr * pr - xi * pi, -(xr * pi + xi * pr)
    fre, fim = wout(0, False)
    bre, bim = wout(1, True)
    wc = jnp.concatenate([fre, fim, bre, bim], axis=1).reshape(g_, 4 * n_, p_ * ch).astype(BF16)

    steps = max(1, (n_chunks - 1).bit_length())
    sr, si = pwr[..., ch], pwi[..., ch]
    rows = []
    for _ in range(steps):
        rows.append(jnp.stack([jnp.concatenate([sr, sr], -1), jnp.concatenate([-si, si], -1)], axis=2))
        sr, si = sr * sr - si * si, 2.0 * sr * si
    pq = jnp.stack(rows, axis=2)
    pq = pq.transpose(1, 0, 2, 3, 4)
    return kpad, w_in, wc, pq


def _s5_body(u_ref, taps_ref, win_ref, wc_ref, pq_ref, y_ref, toep_ref, *, nb, nc, p, n, steps):
    ch = LANES
    m = nb * nc
    for q in range(p):
        for pp in range(p):
            row = jnp.broadcast_to(taps_ref[0, q, pp:pp + 1, :], (ch, 2 * ch))
            blk = pltpu.roll(row, ch + 1, 1, stride=1, stride_axis=0)[:, :ch]
            toep_ref[q * ch:(q + 1) * ch, pp * ch:(pp + 1) * ch] = blk.astype(BF16)
    u = jnp.concatenate([u_ref[:, q].reshape(m, ch) for q in range(p)], axis=1)
    y = _dot(u, toep_ref[...])
    r = _dot(u, win_ref[0])
    c = lax.broadcasted_iota(jnp.int32, (m, 2 * n), 0) % nc

    def scan(x, d):
        for k in range(steps):
            s = 1 << k
            if d == 0:
                t = jnp.where(c >= s, pltpu.roll(x, s, axis=0), 0.0)
            else:
                t = jnp.where(c < nc - s, pltpu.roll(x, m - s, axis=0), 0.0)
            x = x + t * pq_ref[0, d, k, 0:1, :] + pltpu.roll(t, n, axis=1) * pq_ref[0, d, k, 1:2, :]
        if d == 0:
            return jnp.where(c >= 1, pltpu.roll(x, 1, axis=0), 0.0)
        return jnp.where(c < nc - 1, pltpu.roll(x, m - 1, axis=0), 0.0)

    xf = scan(r[:, :2 * n], 0)
    xb = scan(r[:, 2 * n:], 1)
    xc = jnp.concatenate([xf, xb], axis=1).astype(BF16)
    y = y + _dot(xc, wc_ref[0])
    for q in range(p):
        y_ref[:, q] = y[:, q * ch:(q + 1) * ch].reshape(nb, nc, ch)


def _s5(u4, taps, w_in, wc, pq):
    nb, su, nc, ch = u4.shape
    g_ = taps.shape[0]
    p = su // g_
    n = wc.shape[1] // 4
    steps = pq.shape[2]
    body = functools.partial(_s5_body, nb=nb, nc=nc, p=p, n=n, steps=steps)
    return pl.pallas_call(
        body,
        grid=(g_,),
        in_specs=[pl.BlockSpec((nb, p, nc, ch), lambda g: (0, g, 0, 0)),
                  pl.BlockSpec((1,) + taps.shape[1:], lambda g: (g, 0, 0, 0)),
                  pl.BlockSpec((1,) + w_in.shape[1:], lambda g: (g, 0, 0)),
                  pl.BlockSpec((1,) + wc.shape[1:], lambda g: (g, 0, 0)),
                  pl.BlockSpec((1,) + pq.shape[1:], lambda g: (g, 0, 0, 0, 0))],
        out_specs=pl.BlockSpec((nb, p, nc, ch), lambda g: (0, g, 0, 0)),
        out_shape=jax.ShapeDtypeStruct((nb, su, nc, ch), F32),
        scratch_shapes=[pltpu.VMEM((p * ch, p * ch), BF16)],
        compiler_params=_params("parallel"),
        name="s5",
    )(u4, taps, w_in, wc, pq)


def _gla_prep(q, k, v, g, tri, cmask, bdk, bd, ones, last_row):
    g_hi = g.astype(BF16)
    g_lo = (g - g_hi.astype(F32)).astype(BF16)
    cum = _dot(tri, g_hi) + _dot(tri, g_lo)
    last = cum[last_row:last_row + 1, :]
    q_dec = (q * jnp.exp(cum)).astype(BF16)
    k_inv = (k * jnp.exp(-cum)).astype(BF16)
    k_dec = (k * jnp.exp(last - cum)).astype(BF16)
    kb = jnp.where(bdk, jnp.concatenate([k_inv] * GLA_HEADS, axis=0), 0.0)
    s = jnp.where(cmask, _dot_nt(q_dec, kb), 0.0).astype(BF16)
    vb = jnp.where(bd, jnp.concatenate([v] * GLA_HEADS, axis=0), 0.0)
    o_intra = _dot(s, vb)
    tot = _dot_tn(g_hi, ones) + _dot_tn(g_lo, ones)
    kv = jnp.where(bd, _dot_tn(k_dec, v), 0.0)
    return q_dec, o_intra, kv, jnp.exp(tot)


def _gla_sweep(preps, order, st_ref, o_ref, c):
    st = st_ref[...]
    for n in order:
        q_dec, o_intra, kv, decay = preps[n]
        o_ref[n * c:(n + 1) * c, :] = o_intra + _dot(q_dec, st.astype(BF16))
        st = jnp.concatenate([decay] * (st.shape[1] // LANES), axis=1) * st + kv
    st_ref[...] = st


def _gla_body(qf_ref, kf_ref, vf_ref, gf_ref, qb_ref, kb_ref, vb_ref, gb_ref,
              of_ref, ob_ref, sf_ref, sb_ref, *, tb, dk, dv):
    c = GLA_CHUNK

    @pl.when(pl.program_id(1) == 0)
    def _():
        sf_ref[...] = jnp.zeros(sf_ref.shape, F32)
        sb_ref[...] = jnp.zeros(sb_ref.shape, F32)

    ii = lax.broadcasted_iota(jnp.int32, (c, c), 0)
    jj = lax.broadcasted_iota(jnp.int32, (c, c), 1)
    tri_f = (jj <= ii).astype(BF16)
    tri_b = (jj >= ii).astype(BF16)
    hc = GLA_HEADS * c
    i2 = lax.broadcasted_iota(jnp.int32, (c, hc), 0)
    j2 = lax.broadcasted_iota(jnp.int32, (c, hc), 1) % c
    cmask_f = j2 <= i2
    cmask_b = j2 > i2
    bdk = (lax.broadcasted_iota(jnp.int32, (hc, GLA_HEADS * dk), 0) // c
           == lax.broadcasted_iota(jnp.int32, (hc, GLA_HEADS * dk), 1) // dk)
    bd = (lax.broadcasted_iota(jnp.int32, (hc, GLA_HEADS * dv), 0) // c
          == lax.broadcasted_iota(jnp.int32, (hc, GLA_HEADS * dv), 1) // dv)
    ones = jnp.ones((c, LANES), BF16)
    nch = tb // c
    preps_f, preps_b = [], []
    for n in range(nch):
        r = slice(n * c, (n + 1) * c)
        preps_f.append(_gla_prep(qf_ref[r, :], kf_ref[r, :], vf_ref[r, :], gf_ref[r, :],
                                 tri_f, cmask_f, bdk, bd, ones, c - 1))
        preps_b.append(_gla_prep(qb_ref[r, :], kb_ref[r, :], vb_ref[r, :], gb_ref[r, :],
                                 tri_b, cmask_b, bdk, bd, ones, 0))
    _gla_sweep(preps_f, range(nch), sf_ref, of_ref, c)
    _gla_sweep(preps_b, range(nch - 1, -1, -1), sb_ref, ob_ref, c)


def _gla(q, k, v, gf, gb, *, b, l, tb):
    t, hk = q.shape
    hv = v.shape[1]
    dk, dv = hk // GLA_HEADS, hv // GLA_HEADS
    ni = l // tb
    fwd = lambda w: pl.BlockSpec((tb, w), lambda bi, i: (bi * ni + i, 0))
    bwd = lambda w: pl.BlockSpec((tb, w), lambda bi, i: (bi * ni + ni - 1 - i, 0))
    assert dk == GLA_CHUNK, "state mask reuse needs chunk == key head width"
    body = functools.partial(_gla_body, tb=tb, dk=dk, dv=dv)
    state = pltpu.VMEM((hk, hv), F32)
    return pl.pallas_call(
        body,
        grid=(b, ni),
        in_specs=[fwd(hk), fwd(hk), fwd(hv), fwd(hk), bwd(hk), bwd(hk), bwd(hv), bwd(hk)],
        out_specs=[fwd(hv), bwd(hv)],
        out_shape=[jax.ShapeDtypeStruct((t, hv), F32), jax.ShapeDtypeStruct((t, hv), F32)],
        scratch_shapes=[state, state],
        compiler_params=_params("parallel", "arbitrary"),
        name="gla",
    )(q, k, v, gf, q, k, v, gb)


def _head_norm(o, g, heads):
    dh = o.shape[1] // heads
    parts = []
    for h in range(heads):
        oh = o[:, h * dh:(h + 1) * dh]
        parts.append(oh * lax.rsqrt(jnp.mean(oh * oh, axis=-1, keepdims=True) + EPS))
    return jnp.concatenate(parts, axis=1) * g


def _ab_out_body(x_ref, yt_ref, wglut_ref, of_ref, ob_ref, og_ref, gn_ref, wo_ref, fg_ref, w1_ref, w2_ref,
                 o_ref, *, su):
    gy = _gelu_tanh(yt_ref[0])
    z = _dot(wglut_ref[...], gy.astype(BF16))
    s5_out = (gy * _sigmoid(z)).T.astype(BF16)
    o = _head_norm(of_ref[...] + ob_ref[...], gn_ref[...], GLA_HEADS)
    gla_out = (o * _silu(og_ref[...].astype(F32))).astype(BF16)
    x1 = x_ref[...] + _dot(s5_out, wo_ref[:su, :]) + _dot(gla_out, wo_ref[su:, :])
    o_ref[...] = _ffn_apply(x1, fg_ref[...], w1_ref, w2_ref)


def _ab_out(x2, yt, wglut, o_f, o_b, og, gn, wo, ffn, *, b, l, tm):
    t, d = x2.shape
    su = yt.shape[1]
    hv = o_f.shape[1]
    ni = l // tm
    fg, w1, w2 = ffn
    tok = lambda w: pl.BlockSpec((tm, w), lambda bi, i: (bi * ni + i, 0))
    body = functools.partial(_ab_out_body, su=su)
    return pl.pallas_call(
        body,
        grid=(b, ni),
        in_specs=[tok(d), pl.BlockSpec((1, su, tm), lambda bi, i: (bi, 0, i)), _const_spec(wglut.shape),
                  tok(hv), tok(hv), tok(hv), _const_spec((1, hv)), _const_spec(wo.shape),
                  _const_spec((1, d)), _const_spec(w1.shape), _const_spec(w2.shape)],
        out_specs=tok(d),
        out_shape=jax.ShapeDtypeStruct((t, d), F32),
        compiler_params=_params("parallel", "parallel"),
        name="ab_out_ffn",
    )(x2, yt, wglut, o_f, o_b, og, gn.reshape(1, hv), wo, fg.reshape(1, d), w1, w2)


def _ret_in_body(x_ref, g_ref, w_ref, wkt_ref, cos_ref, sin_ref, cost_ref, sint_ref,
                 q_ref, kt_ref, v_ref, og_ref, *, hk, hv, dk):
    h = _rmsnorm(x_ref[...], g_ref[...]).astype(BF16)
    cos2 = cos_ref[...]
    sin2 = sin_ref[...]
    q = _dot(h, w_ref[:, :hk])
    q_ref[...] = jnp.concatenate(
        [q[:, s:s + dk] * cos2 + pltpu.roll(q[:, s:s + dk], dk // 2, axis=1) * sin2 for s in range(0, hk, dk)],
        axis=1).astype(BF16)
    kt = _dot_nt(wkt_ref[...], h) * dk ** -0.5
    cost = cost_ref[...]
    sint = sint_ref[...]
    parts = []
    for s in range(0, hk, dk):
        t1 = kt[s:s + dk // 2]
        t2 = kt[s + dk // 2:s + dk]
        parts += [t1 * cost - t2 * sint, t1 * sint + t2 * cost]
    kt_ref[0] = jnp.concatenate(parts, axis=0).astype(BF16)
    v_ref[...] = _dot(h, w_ref[:, hk:hk + hv]).astype(BF16)
    og_ref[...] = _dot(h, w_ref[:, hk + hv:]).astype(BF16)


def _ret_in(x2, g, w, wkt, cos2, sin2, cost, sint, *, b, l, tm, hk, hv):
    t, d = x2.shape
    dk = hk // RET_HEADS
    ni = l // tm
    tok = lambda w_: pl.BlockSpec((tm, w_), lambda bi, i: (bi * ni + i, 0))
    pos = pl.BlockSpec((tm, dk), lambda bi, i: (i, 0))
    post = pl.BlockSpec((dk // 2, tm), lambda bi, i: (0, i))
    body = functools.partial(_ret_in_body, hk=hk, hv=hv, dk=dk)
    return pl.pallas_call(
        body,
        grid=(b, ni),
        in_specs=[tok(d), _const_spec((1, d)), _const_spec(w.shape), _const_spec(wkt.shape), pos, pos, post, post],
        out_specs=[tok(hk), pl.BlockSpec((1, hk, tm), lambda bi, i: (bi, 0, i)), tok(hv), tok(hv)],
        out_shape=[jax.ShapeDtypeStruct((t, hk), BF16), jax.ShapeDtypeStruct((b, hk, l), BF16),
                   jax.ShapeDtypeStruct((t, hv), BF16), jax.ShapeDtypeStruct((t, hv), BF16)],
        compiler_params=_params("parallel", "parallel"),
        name="ret_in",
    )(x2, g.reshape(1, d), w, wkt, cos2, sin2, cost, sint)


def _ret_tables(c, dk, dv):
    lg_f = jnp.log1p(-jnp.exp2(-5.0 - jnp.arange(RET_HEADS, dtype=F32)))
    lg_b = lg_f[::-1]
    idx = jnp.arange(c)
    diff = (idx[:, None] - idx[None, :]).astype(F32)
    dmat = jnp.where(diff >= 0, jnp.exp(jnp.maximum(diff, 0.0)[None] * lg_f[:, None, None]),
                     jnp.exp(jnp.maximum(-diff, 0.0)[None] * lg_b[:, None, None]))
    pos = idx.astype(F32)
    ones_v = jnp.ones((1, 1, dv), F32)
    xi_f = jnp.exp((pos + 1.0)[None, :] * lg_f[:, None])[..., None] * ones_v
    xi_b = jnp.exp((c - pos)[None, :] * lg_b[:, None])[..., None] * ones_v
    zeta_f = jnp.exp((c - 1.0 - pos)[None, :] * lg_f[:, None])[:, None, :]
    zeta_b = jnp.exp(pos[None, :] * lg_b[:, None])[:, None, :]
    cd_f = jnp.exp(c * lg_f)[:, None, None] * ones_v
    cd_b = jnp.exp(c * lg_b)[:, None, None] * ones_v
    return dmat, xi_f, xi_b, zeta_f, zeta_b, cd_f, cd_b


def _ret_state_body(kt_ref, v_ref, zb_ref, cdb_ref, sb_out_ref, sb_ref, *, dk, dv):
    @pl.when(pl.program_id(1) == 0)
    def _():
        sb_ref[...] = jnp.zeros(sb_ref.shape, F32)

    for h in range(RET_HEADS):
        st = sb_ref[h]
        sb_out_ref[0, 0, h] = st.astype(BF16)
        kz = (kt_ref[0, h * dk:(h + 1) * dk, :].astype(F32) * zb_ref[h]).astype(BF16)
        sb_ref[h] = cdb_ref[h] * st + _dot(kz, v_ref[:, h * dv:(h + 1) * dv])


def _ret_state(kt, v, zeta_b, cd_b, *, b, l, tb):
    hk, hv = kt.shape[1], v.shape[1]
    dk, dv = hk // RET_HEADS, hv // RET_HEADS
    ni = l // tb
    bwd = lambda w: pl.BlockSpec((tb, w), lambda bi, i: (bi * ni + ni - 1 - i, 0))
    body = functools.partial(_ret_state_body, dk=dk, dv=dv)
    return pl.pallas_call(
        body,
        grid=(b, ni),
        in_specs=[pl.BlockSpec((1, hk, tb), lambda bi, i: (bi, 0, ni - 1 - i)), bwd(hv),
                  _const_spec(zeta_b.shape), _const_spec(cd_b.shape)],
        out_specs=pl.BlockSpec((1, 1, RET_HEADS, dk, dv), lambda bi, i: (bi, ni - 1 - i, 0, 0, 0)),
        out_shape=jax.ShapeDtypeStruct((b, ni, RET_HEADS, dk, dv), BF16),
        scratch_shapes=[pltpu.VMEM((RET_HEADS, dk, dv), F32)],
        compiler_params=_params("parallel", "arbitrary"),
        name="ret_state",
    )(kt, v, zeta_b, cd_b)


def _ret_main_body(x_ref, q_ref, kt_ref, v_ref, og_ref, sb_ref, dmat_ref, xif_ref, xib_ref, zf_ref, cdf_ref,
                   gn_ref, wo_ref, o_ref, sf_ref, gated_ref, *, dk, dv):
    @pl.when(pl.program_id(1) == 0)
    def _():
        sf_ref[...] = jnp.zeros(sf_ref.shape, F32)

    cb = dmat_ref.shape[1]
    for n in range(q_ref.shape[0] // cb):
        rows = slice(n * cb, (n + 1) * cb)
        for h in range(RET_HEADS):
            ks = slice(h * dk, (h + 1) * dk)
            vs = slice(h * dv, (h + 1) * dv)
            q = q_ref[rows, ks]
            kt = kt_ref[0, ks, rows]
            v = v_ref[rows, vs]
            s = _dot(q, kt) * dmat_ref[h]
            st = sf_ref[h]
            o = (_dot(s.astype(BF16), v) + _dot(q, st.astype(BF16)) * xif_ref[h]
                 + _dot(q, sb_ref[0, n, h]) * xib_ref[h])
            kz = (kt.astype(F32) * zf_ref[h]).astype(BF16)
            sf_ref[h] = cdf_ref[h] * st + _dot(kz, v)
            o = o * lax.rsqrt(jnp.mean(o * o, axis=-1, keepdims=True) + EPS) * gn_ref[:, vs]
            gated_ref[rows, vs] = (o * _silu(og_ref[rows, vs].astype(F32))).astype(BF16)
    o_ref[...] = x_ref[...] + _dot(gated_ref[...], wo_ref[...])


def _ret_main(x2, q, kt, v, og, sb, tabs, gn, wo, *, b, l, tb):
    t, d = x2.shape
    hk, hv = q.shape[1], v.shape[1]
    dk, dv = hk // RET_HEADS, hv // RET_HEADS
    ni = l // tb
    nsub = tb // tabs[0].shape[1]
    tok = lambda w: pl.BlockSpec((tb, w), lambda bi, i: (bi * ni + i, 0))
    body = functools.partial(_ret_main_body, dk=dk, dv=dv)
    return pl.pallas_call(
        body,
        grid=(b, ni),
        in_specs=[tok(d), tok(hk), pl.BlockSpec((1, hk, tb), lambda bi, i: (bi, 0, i)), tok(hv), tok(hv),
                  pl.BlockSpec((1, nsub, RET_HEADS, dk, dv), lambda bi, i: (bi, i, 0, 0, 0))]
                 + [_const_spec(a.shape) for a in tabs] + [_const_spec((1, hv)), _const_spec(wo.shape)],
        out_specs=tok(d),
        out_shape=jax.ShapeDtypeStruct((t, d), F32),
        scratch_shapes=[pltpu.VMEM((RET_HEADS, dk, dv), F32), pltpu.VMEM((tb, hv), BF16)],
        compiler_params=_params("parallel", "arbitrary"),
        name="ret_main",
    )(x2, q, kt, v, og, sb, *tabs, gn.reshape(1, hv), wo)


def _tile(l, want):
    t = min(l, want)
    while l % t:
        t //= 2
    return t


def _s5_gla_layer(x2, mix_g, w_in, lam_re, lam_im, b_re, b_im, c_re, c_im, log_dt, d_skip, w_glu,
                  w_gk, b_gk, gla_norm, w_out, ffn2, *, b, l):
    su = d_skip.shape[0]
    hk = w_gk.shape[-1]
    r = w_gk.shape[1]
    hv = gla_norm.shape[0]
    tm = _tile(l, 1024)
    wut = w_in[:, :su].T.astype(BF16)
    wr = w_in[:, su:].astype(BF16)
    wgk = jnp.zeros((2 * r, 2 * hk), F32)
    wgk = wgk.at[:r, :hk].set(w_gk[0]).at[r:, hk:].set(w_gk[1]).astype(BF16)
    bgk = b_gk.reshape(1, 2 * hk).astype(F32)
    ut, q, k, v, og, gf, gb = _ab_in(x2, mix_g, wut, wr, wgk, bgk, b=b, l=l, tm=tm, hk=hk, hv=hv)

    nc = l // LANES
    taps, w_st, wc, pq = _s5_tables(lam_re, lam_im, b_re, b_im, c_re, c_im, log_dt, d_skip, nc)
    y4 = _s5(ut.reshape(b, su, nc, LANES), taps, w_st, wc, pq)
    yt = y4.reshape(b, su, l)

    o_f, o_b = _gla(q, k, v, gf, gb, b=b, l=l, tb=_tile(l, 512))
    return _ab_out(x2, yt, w_glu.T.astype(BF16), o_f, o_b, og, gla_norm, w_out.astype(BF16), ffn2,
                   b=b, l=l, tm=_tile(l, 512))


def _retention_layer(x2, mix_g, w_in, ret_norm, w_out, *, b, l):
    hv = ret_norm.shape[0]
    hk = (w_in.shape[1] - 2 * hv) // 2
    dk = hk // RET_HEADS
    half = dk // 2
    tm = _tile(l, 512)
    pos = jnp.arange(l, dtype=F32)
    inv = jnp.exp(-math.log(ROPE_BASE) * jnp.arange(half, dtype=F32) / half)
    ang = pos[:, None] * inv[None, :]
    cos, sin = jnp.cos(ang), jnp.sin(ang)
    cos2 = jnp.concatenate([cos, cos], axis=1)
    sin2 = jnp.concatenate([-sin, sin], axis=1)
    w_qvg = jnp.concatenate([w_in[:, :hk], w_in[:, 2 * hk:]], axis=1).astype(BF16)
    w_kt = w_in[:, hk:2 * hk].T.astype(BF16)
    q, kt, v, og = _ret_in(x2, mix_g, w_qvg, w_kt, cos2, sin2, cos.T, sin.T, b=b, l=l, tm=tm, hk=hk, hv=hv)
    tb = _tile(l, 256)
    dmat, xi_f, xi_b, zeta_f, zeta_b, cd_f, cd_b = _ret_tables(tb, dk, hv // RET_HEADS)
    sb = _ret_state(kt, v, zeta_b, cd_b, b=b, l=l, tb=tb)
    return _ret_main(x2, q, kt, v, og, sb, (dmat, xi_f, xi_b, zeta_f, cd_f), ret_norm, w_out.astype(BF16),
                     b=b, l=l, tb=_tile(l, 2 * tb))


def kernel(x, ffn1_norm, ffn1_w1, ffn1_w2, mix_norm, ffn2_norm, ffn2_w1, ffn2_w2, ab_w_in, s5_lambda_re, s5_lambda_im, s5_b_re, s5_b_im, s5_c_re, s5_c_im, s5_log_dt, s5_d, s5_w_glu, gla_w_gk, gla_b_gk, gla_norm, ab_w_out, ret_w_in, ret_norm, ret_w_out, final_norm):
    b, l, d = x.shape
    depth = ffn1_norm.shape[0]
    assert depth % 2 == 0, "the final norm is applied by the retention layer's second FFN"
    tm = _tile(b * l, 1024)
    x2 = x.reshape(b * l, d).astype(F32)
    for i in range(depth):
        j = i // 2
        ffn1 = (ffn1_norm[i], ffn1_w1[i].astype(BF16), ffn1_w2[i].astype(BF16))
        ffn2 = (ffn2_norm[i], ffn2_w1[i].astype(BF16), ffn2_w2[i].astype(BF16))
        x2 = _ffn(x2, *ffn1, final_norm, final=False, tm=tm)
        if i % 2 == 0:
            x2 = _s5_gla_layer(x2, mix_norm[i], ab_w_in[j], s5_lambda_re[j], s5_lambda_im[j], s5_b_re[j],
                               s5_b_im[j], s5_c_re[j], s5_c_im[j], s5_log_dt[j], s5_d[j], s5_w_glu[j],
                               gla_w_gk[j], gla_b_gk[j], gla_norm[j], ab_w_out[j], ffn2, b=b, l=l)
        else:
            x2 = _retention_layer(x2, mix_norm[i], ret_w_in[j], ret_norm[j], ret_w_out[j], b=b, l=l)
            x2 = _ffn(x2, *ffn2, final_norm, final=(i == depth - 1), tm=tm)
    return x2.reshape(b, l, d)
```

```python
import functools
import math

import jax
import jax.numpy as jnp
from jax import lax
from jax.experimental import pallas as pl
from jax.experimental.pallas import tpu as pltpu

F32 = jnp.float32
BF16 = jnp.bfloat16

EPS = 1e-6
S5_GROUP = 16
S5_STATE = 64
GLA_HEADS = 4
GLA_RANK = 16
GLA_GATE_NORM = 16.0
GLA_CHUNK = 64
RET_HEADS = 8
ROPE_BASE = 10000.0

LANES = 128
VMEM_LIMIT = 56 * 1024 * 1024
HI = lax.Precision.HIGHEST


def _const_spec(shape):
    nd = len(shape)
    return pl.BlockSpec(shape, lambda *_: (0,) * nd, pipeline_mode=pl.Buffered(1))


def _params(*sem):
    return pltpu.CompilerParams(dimension_semantics=sem, vmem_limit_bytes=VMEM_LIMIT)


def _rmsnorm(x, g):
    return x * lax.rsqrt(jnp.mean(x * x, axis=-1, keepdims=True) + EPS) * g


def _sigmoid(x):
    return 1.0 / (1.0 + jnp.exp(-x))


def _silu(x):
    return x * _sigmoid(x)


def _gelu_tanh(x):
    return 0.5 * x * (1.0 + jnp.tanh(math.sqrt(2.0 / math.pi) * (x + 0.044715 * (x * x * x))))


def _log_sigmoid(x):
    return jnp.minimum(x, 0.0) - jnp.log1p(jnp.exp(-jnp.abs(x)))


def _dot(a, b):
    return jnp.dot(a, b, preferred_element_type=F32)


def _dot_nt(a, b):
    return lax.dot_general(a, b, (((1,), (1,)), ((), ())), preferred_element_type=F32)


def _dot_tn(a, b):
    return lax.dot_general(a, b, (((0,), (0,)), ((), ())), preferred_element_type=F32)


def _ffn_apply(x, g, w1_ref, w2_ref):
    d_ff = w2_ref.shape[0]
    f_chunk = 256 if d_ff % 256 == 0 else d_ff
    xn = _rmsnorm(x, g).astype(BF16)
    acc = jnp.zeros(x.shape, F32)
    for f0 in range(0, d_ff, f_chunk):
        gate = _dot(xn, w1_ref[:, f0:f0 + f_chunk])
        up = _dot(xn, w1_ref[:, d_ff + f0:d_ff + f0 + f_chunk])
        act = (_silu(gate) * up).astype(BF16)
        acc = acc + _dot(act, w2_ref[f0:f0 + f_chunk, :])
    return x + 0.5 * acc


def _ffn_body(x_ref, g_ref, w1_ref, w2_ref, fg_ref, o_ref, *, final):
    y = _ffn_apply(x_ref[...], g_ref[...], w1_ref, w2_ref)
    if final:
        y = _rmsnorm(y, fg_ref[...])
    o_ref[...] = y


def _ffn(x2, g, w1, w2, fg, *, final, tm):
    t, d = x2.shape
    body = functools.partial(_ffn_body, final=final)
    return pl.pallas_call(
        body,
        grid=(t // tm,),
        in_specs=[pl.BlockSpec((tm, d), lambda i: (i, 0)),
                  _const_spec((1, d)), _const_spec(w1.shape), _const_spec(w2.shape),
                  _const_spec((1, d))],
        out_specs=pl.BlockSpec((tm, d), lambda i: (i, 0)),
        out_shape=jax.ShapeDtypeStruct((t, d), F32),
        compiler_params=_params("parallel"),
        name="ffn",
    )(x2, g.reshape(1, d), w1, w2, fg.reshape(1, d))


def _ab_in_body(x_ref, g_ref, wut_ref, wr_ref, wgk_ref, bgk_ref,
                ut_ref, q_ref, k_ref, v_ref, og_ref, gf_ref, gb_ref, *, hk, hv):
    h = _rmsnorm(x_ref[...], g_ref[...]).astype(BF16)
    ut_ref[0] = _dot_nt(wut_ref[...], h).astype(BF16)
    proj = _dot(h, wr_ref[...])
    q_ref[...] = proj[:, :hk] * (hk // GLA_HEADS) ** -0.5
    k_ref[...] = proj[:, hk:2 * hk]
    v_ref[...] = proj[:, 2 * hk:2 * hk + hv].astype(BF16)
    og_ref[...] = proj[:, 2 * hk + hv:2 * hk + 2 * hv].astype(BF16)
    glo = proj[:, 2 * hk + 2 * hv:].astype(BF16)
    gk = _log_sigmoid(_dot(glo, wgk_ref[...]) + bgk_ref[...]) * (1.0 / GLA_GATE_NORM)
    gf_ref[...] = gk[:, :hk]
    gb_ref[...] = gk[:, hk:]


def _ab_in(x2, g, wut, wr, wgk, bgk, *, b, l, tm, hk, hv):
    t, d = x2.shape
    su = wut.shape[0]
    ni = l // tm
    tok = lambda w: pl.BlockSpec((tm, w), lambda bi, i: (bi * ni + i, 0))
    body = functools.partial(_ab_in_body, hk=hk, hv=hv)
    return pl.pallas_call(
        body,
        grid=(b, ni),
        in_specs=[tok(d), _const_spec((1, d)), _const_spec(wut.shape), _const_spec(wr.shape),
                  _const_spec(wgk.shape), _const_spec(bgk.shape)],
        out_specs=[pl.BlockSpec((1, su, tm), lambda bi, i: (bi, 0, i)),
                   tok(hk), tok(hk), tok(hv), tok(hv), tok(hk), tok(hk)],
        out_shape=[jax.ShapeDtypeStruct((b, su, l), BF16),
                   jax.ShapeDtypeStruct((t, hk), F32), jax.ShapeDtypeStruct((t, hk), F32),
                   jax.ShapeDtypeStruct((t, hv), BF16), jax.ShapeDtypeStruct((t, hv), BF16),
                   jax.ShapeDtypeStruct((t, hk), F32), jax.ShapeDtypeStruct((t, hk), F32)],
        compiler_params=_params("parallel", "parallel"),
        name="ab_in",
    )(x2, g.reshape(1, d), wut, wr, wgk, bgk)


def _cpow_table(ar, ai, count):
    ks = jnp.arange(count)
    pr = jnp.ones(ar.shape + (count,), F32)
    pi = jnp.zeros(ar.shape + (count,), F32)
    sr, si = ar, ai
    for bit in range(max(1, (count - 1).bit_length())):
        on = ((ks >> bit) & 1) == 1
        nr = pr * sr[..., None] - pi * si[..., None]
        ni = pr * si[..., None] + pi * sr[..., None]
        pr = jnp.where(on, nr, pr)
        pi = jnp.where(on, ni, pi)
        sr, si = sr * sr - si * si, 2.0 * sr * si
    return pr, pi


def _s5_tables(lam_re, lam_im, b_re, b_im, c_re, c_im, log_dt, d_skip, n_chunks):
    ch = LANES
    g_, n_ = lam_re.shape[1], lam_re.shape[2]
    p_ = b_re.shape[-1]
    f32 = lambda a: a.astype(F32)
    lr = jnp.minimum(f32(lam_re), -1e-4)
    li = f32(lam_im)
    dt = jnp.exp(f32(log_dt))[..., None]
    mag = jnp.exp(lr * dt)
    ar = mag * jnp.cos(li * dt)
    ai = mag * jnp.sin(li * dt)
    den = lr * lr + li * li
    cr = ((ar - 1.0) * lr + ai * li) / den
    ci = (ai * lr - (ar - 1.0) * li) / den
    br, bi = f32(b_re), f32(b_im)
    bbr = cr[..., None] * br - ci[..., None] * bi
    bbi = cr[..., None] * bi + ci[..., None] * br
    ccr, cci = f32(c_re), f32(c_im)
    pwr, pwi = _cpow_table(ar, ai, ch + 1)

    cnr = ccr.transpose(0, 1, 3, 2)[..., None]
    cni = cci.transpose(0, 1, 3, 2)[..., None]
    cbr = cnr * bbr[:, :, :, None, :] - cni * bbi[:, :, :, None, :]
    cbi = cnr * bbi[:, :, :, None, :] + cni * bbr[:, :, :, None, :]
    taps = (jnp.einsum('dgnpq,dgnl->dgqpl', cbr, pwr[..., :ch], precision=HI)
            - jnp.einsum('dgnpq,dgnl->dgqpl', cbi, pwi[..., :ch], precision=HI))
    kf, kb = taps[0], taps[1]
    dmat = f32(d_skip).reshape(g_, p_)[:, None, :, None] * jnp.eye(p_, dtype=F32)[None, :, :, None]
    center = kf[..., 0:1] + kb[..., 0:1] + dmat
    kfull = jnp.concatenate([kb[..., :0:-1], center, kf[..., 1:]], axis=-1)
    kpad = jnp.pad(kfull, ((0, 0),) * 3 + ((0, 1),))

    def win(d, rev):
        pr = pwr[d][..., :ch]
        pi = pwi[d][..., :ch]
        if rev:
            pr, pi = pr[..., ::-1], pi[..., ::-1]
        pr = pr.transpose(0, 2, 1)[:, None]
        pi = pi.transpose(0, 2, 1)[:, None]
        xr = bbr[d].transpose(0, 2, 1)[:, :, None, :]
        xi = bbi[d].transpose(0, 2, 1)[:, :, None, :]
        return pr * xr - pi * xi, pr * xi + pi * xr
    fre, fim = win(0, True)
    bre, bim = win(1, False)
    w_in = jnp.concatenate([fre, fim, bre, bim], axis=-1).reshape(g_, p_ * ch, 4 * n_).astype(BF16)

    def wout(d, rev):
        pr = pwr[d][..., 1:ch + 1]
        pi = pwi[d][..., 1:ch + 1]
        if rev:
            pr, pi = pr[..., ::-1], pi[..., ::-1]
        pr = pr[:, :, None, :]
        pi = pi[:, :, None, :]
        xr = ccr[d].transpose(0, 2, 1)[..., None]
        xi = cci[d].transpose(0, 2, 1)[..., None]
        return xr * pr - xi * pi, -(xr * pi + xi * pr)
    fre, fim = wout(0, False)
    bre, bim = wout(1, True)
    wc = jnp.concatenate([fre, fim, bre, bim], axis=1).reshape(g_, 4 * n_, p_ * ch).astype(BF16)

    steps = max(1, (n_chunks - 1).bit_length())
    sr, si = pwr[..., ch], pwi[..., ch]
    rows = []
    for _ in range(steps):
        rows.append(jnp.stack([jnp.concatenate([sr, sr], -1), jnp.concatenate([-si, si], -1)], axis=2))
        sr, si = sr * sr - si * si, 2.0 * sr * si
    pq = jnp.stack(rows, axis=2)
    pq = pq.transpose(1, 0, 2, 3, 4)
    return kpad, w_in, wc, pq


def _s5_body(u_ref, taps_ref, win_ref, wc_ref, pq_ref, y_ref, toep_ref, *, nb, nc, p, n, steps):
    ch = LANES
    m = nb * nc
    for q in range(p):
        for pp in range(p):
            row = jnp.broadcast_to(taps_ref[0, q, pp:pp + 1, :], (ch, 2 * ch))
            blk = pltpu.roll(row, ch + 1, 1, stride=1, stride_axis=0)[:, :ch]
            toep_ref[q * ch:(q + 1) * ch, pp * ch:(pp + 1) * ch] = blk.astype(BF16)
    u = jnp.concatenate([u_ref[:, q].reshape(m, ch) for q in range(p)], axis=1)
    y = _dot(u, toep_ref[...])
    r = _dot(u, win_ref[0])
    c = lax.broadcasted_iota(jnp.int32, (m, 2 * n), 0) % nc

    def scan(x, d):
        for k in range(steps):
            s = 1 << k
            if d == 0:
                t = jnp.where(c >= s, pltpu.roll(x, s, axis=0), 0.0)
            else:
                t = jnp.where(c < nc - s, pltpu.roll(x, m - s, axis=0), 0.0)
            x = x + t * pq_ref[0, d, k, 0:1, :] + pltpu.roll(t, n, axis=1) * pq_ref[0, d, k, 1:2, :]
        if d == 0:
            return jnp.where(c >= 1, pltpu.roll(x, 1, axis=0), 0.0)
        return jnp.where(c < nc - 1, pltpu.roll(x, m - 1, axis=0), 0.0)

    xf = scan(r[:, :2 * n], 0)
    xb = scan(r[:, 2 * n:], 1)
    xc = jnp.concatenate([xf, xb], axis=1).astype(BF16)
    y = y + _dot(xc, wc_ref[0])
    for q in range(p):
        y_ref[:, q] = y[:, q * ch:(q + 1) * ch].reshape(nb, nc, ch)


def _s5(u4, taps, w_in, wc, pq):
    nb, su, nc, ch = u4.shape
    g_ = taps.shape[0]
    p = su // g_
    n = wc.shape[1] // 4
    steps = pq.shape[2]
    body = functools.partial(_s5_body, nb=nb, nc=nc, p=p, n=n, steps=steps)
    return pl.pallas_call(
        body,
        grid=(g_,),
        in_specs=[pl.BlockSpec((nb, p, nc, ch), lambda g: (0, g, 0, 0)),
                  pl.BlockSpec((1,) + taps.shape[1:], lambda g: (g, 0, 0, 0)),
                  pl.BlockSpec((1,) + w_in.shape[1:], lambda g: (g, 0, 0)),
                  pl.BlockSpec((1,) + wc.shape[1:], lambda g: (g, 0, 0)),
                  pl.BlockSpec((1,) + pq.shape[1:], lambda g: (g, 0, 0, 0, 0))],
        out_specs=pl.BlockSpec((nb, p, nc, ch), lambda g: (0, g, 0, 0)),
        out_shape=jax.ShapeDtypeStruct((nb, su, nc, ch), F32),
        scratch_shapes=[pltpu.VMEM((p * ch, p * ch), BF16)],
        compiler_params=_params("parallel"),
        name="s5",
    )(u4, taps, w_in, wc, pq)


def _gla_prep(q, k, v, g, tri, cmask, bdk, bd, ones, last_row):
    g_hi = g.astype(BF16)
    g_lo = (g - g_hi.astype(F32)).astype(BF16)
    cum = _dot(tri, g_hi) + _dot(tri, g_lo)
    last = cum[last_row:last_row + 1, :]
    q_dec = (q * jnp.exp(cum)).astype(BF16)
    k_inv = (k * jnp.exp(-cum)).astype(BF16)
    k_dec = (k * jnp.exp(last - cum)).astype(BF16)
    kb = jnp.where(bdk, jnp.concatenate([k_inv] * GLA_HEADS, axis=0), 0.0)
    s = jnp.where(cmask, _dot_nt(q_dec, kb), 0.0).astype(BF16)
    vb = jnp.where(bd, jnp.concatenate([v] * GLA_HEADS, axis=0), 0.0)
    o_intra = _dot(s, vb)
    tot = _dot_tn(g_hi, ones) + _dot_tn(g_lo, ones)
    kv = jnp.where(bd, _dot_tn(k_dec, v), 0.0)
    return q_dec, o_intra, kv, jnp.exp(tot)


def _gla_sweep(preps, order, st_ref, o_ref, c):
    st = st_ref[...]
    for n in order:
        q_dec, o_intra, kv, decay = preps[n]
        o_ref[n * c:(n + 1) * c, :] = o_intra + _dot(q_dec, st.astype(BF16))
        st = jnp.concatenate([decay] * (st.shape[1] // LANES), axis=1) * st + kv
    st_ref[...] = st


def _gla_body(qf_ref, kf_ref, vf_ref, gf_ref, qb_ref, kb_ref, vb_ref, gb_ref,
              of_ref, ob_ref, sf_ref, sb_ref, *, tb, dk, dv):
    c = GLA_CHUNK

    @pl.when(pl.program_id(1) == 0)
    def _():
        sf_ref[...] = jnp.zeros(sf_ref.shape, F32)
        sb_ref[...] = jnp.zeros(sb_ref.shape, F32)

    ii = lax.broadcasted_iota(jnp.int32, (c, c), 0)
    jj = lax.broadcasted_iota(jnp.int32, (c, c), 1)
    tri_f = (jj <= ii).astype(BF16)
    tri_b = (jj >= ii).astype(BF16)
    hc = GLA_HEADS * c
    i2 = lax.broadcasted_iota(jnp.int32, (c, hc), 0)
    j2 = lax.broadcasted_iota(jnp.int32, (c, hc), 1) % c
    cmask_f = j2 <= i2
    cmask_b = j2 > i2
    bdk = (lax.broadcasted_iota(jnp.int32, (hc, GLA_HEADS * dk), 0) // c
           == lax.broadcasted_iota(jnp.int32, (hc, GLA_HEADS * dk), 1) // dk)
    bd = (lax.broadcasted_iota(jnp.int32, (hc, GLA_HEADS * dv), 0) // c
          == lax.broadcasted_iota(jnp.int32, (hc, GLA_HEADS * dv), 1) // dv)
    ones = jnp.ones((c, LANES), BF16)
    nch = tb // c
    preps_f, preps_b = [], []
    for n in range(nch):
        r = slice(n * c, (n + 1) * c)
        preps_f.append(_gla_prep(qf_ref[r, :], kf_ref[r, :], vf_ref[r, :], gf_ref[r, :],
                                 tri_f, cmask_f, bdk, bd, ones, c - 1))
        preps_b.append(_gla_prep(qb_ref[r, :], kb_ref[r, :], vb_ref[r, :], gb_ref[r, :],
                                 tri_b, cmask_b, bdk, bd, ones, 0))
    _gla_sweep(preps_f, range(nch), sf_ref, of_ref, c)
    _gla_sweep(preps_b, range(nch - 1, -1, -1), sb_ref, ob_ref, c)


def _gla(q, k, v, gf, gb, *, b, l, tb):
    t, hk = q.shape
    hv = v.shape[1]
    dk, dv = hk // GLA_HEADS, hv // GLA_HEADS
    ni = l // tb
    fwd = lambda w: pl.BlockSpec((tb, w), lambda bi, i: (bi * ni + i, 0))
    bwd = lambda w: pl.BlockSpec((tb, w), lambda bi, i: (bi * ni + ni - 1 - i, 0))
    assert dk == GLA_CHUNK, "state mask reuse needs chunk == key head width"
    body = functools.partial(_gla_body, tb=tb, dk=dk, dv=dv)
    state = pltpu.VMEM((hk, hv), F32)
    return pl.pallas_call(
        body,
        grid=(b, ni),
        in_specs=[fwd(hk), fwd(hk), fwd(hv), fwd(hk), bwd(hk), bwd(hk), bwd(hv), bwd(hk)],
        out_specs=[fwd(hv), bwd(hv)],
        out_shape=[jax.ShapeDtypeStruct((t, hv), F32), jax.ShapeDtypeStruct((t, hv), F32)],
        scratch_shapes=[state, state],
        compiler_params=_params("parallel", "arbitrary"),
        name="gla",
    )(q, k, v, gf, q, k, v, gb)


def _head_norm(o, g, heads):
    dh = o.shape[1] // heads
    parts = []
    for h in range(heads):
        oh = o[:, h * dh:(h + 1) * dh]
        parts.append(oh * lax.rsqrt(jnp.mean(oh * oh, axis=-1, keepdims=True) + EPS))
    return jnp.concatenate(parts, axis=1) * g


def _ab_out_body(x_ref, yt_ref, wglut_ref, of_ref, ob_ref, og_ref, gn_ref, wo_ref, fg_ref, w1_ref, w2_ref,
                 o_ref, *, su):
    gy = _gelu_tanh(yt_ref[0])
    z = _dot(wglut_ref[...], gy.astype(BF16))
    s5_out = (gy * _sigmoid(z)).T.astype(BF16)
    o = _head_norm(of_ref[...] + ob_ref[...], gn_ref[...], GLA_HEADS)
    gla_out = (o * _silu(og_ref[...].astype(F32))).astype(BF16)
    x1 = x_ref[...] + _dot(s5_out, wo_ref[:su, :]) + _dot(gla_out, wo_ref[su:, :])
    o_ref[...] = _ffn_apply(x1, fg_ref[...], w1_ref, w2_ref)


def _ab_out(x2, yt, wglut, o_f, o_b, og, gn, wo, ffn, *, b, l, tm):
    t, d = x2.shape
    su = yt.shape[1]
    hv = o_f.shape[1]
    ni = l // tm
    fg, w1, w2 = ffn
    tok = lambda w: pl.BlockSpec((tm, w), lambda bi, i: (bi * ni + i, 0))
    body = functools.partial(_ab_out_body, su=su)
    return pl.pallas_call(
        body,
        grid=(b, ni),
        in_specs=[tok(d), pl.BlockSpec((1, su, tm), lambda bi, i: (bi, 0, i)), _const_spec(wglut.shape),
                  tok(hv), tok(hv), tok(hv), _const_spec((1, hv)), _const_spec(wo.shape),
                  _const_spec((1, d)), _const_spec(w1.shape), _const_spec(w2.shape)],
        out_specs=tok(d),
        out_shape=jax.ShapeDtypeStruct((t, d), F32),
        compiler_params=_params("parallel", "parallel"),
        name="ab_out_ffn",
    )(x2, yt, wglut, o_f, o_b, og, gn.reshape(1, hv), wo, fg.reshape(1, d), w1, w2)


def _ret_in_body(x_ref, g_ref, w_ref, wkt_ref, cos_ref, sin_ref, cost_ref, sint_ref,
                 q_ref, kt_ref, v_ref, og_ref, *, hk, hv, dk):
    h = _rmsnorm(x_ref[...], g_ref[...]).astype(BF16)
    cos2 = cos_ref[...]
    sin2 = sin_ref[...]
    q = _dot(h, w_ref[:, :hk])
    q_ref[...] = jnp.concatenate(
        [q[:, s:s + dk] * cos2 + pltpu.roll(q[:, s:s + dk], dk // 2, axis=1) * sin2 for s in range(0, hk, dk)],
        axis=1).astype(BF16)
    kt = _dot_nt(wkt_ref[...], h) * dk ** -0.5
    cost = cost_ref[...]
    sint = sint_ref[...]
    parts = []
    for s in range(0, hk, dk):
        t1 = kt[s:s + dk // 2]
        t2 = kt[s + dk // 2:s + dk]
        parts += [t1 * cost - t2 * sint, t1 * sint + t2 * cost]
    kt_ref[0] = jnp.concatenate(parts, axis=0).astype(BF16)
    v_ref[...] = _dot(h, w_ref[:, hk:hk + hv]).astype(BF16)
    og_ref[...] = _dot(h, w_ref[:, hk + hv:]).astype(BF16)


def _ret_in(x2, g, w, wkt, cos2, sin2, cost, sint, *, b, l, tm, hk, hv):
    t, d = x2.shape
    dk = hk // RET_HEADS
    ni = l // tm
    tok = lambda w_: pl.BlockSpec((tm, w_), lambda bi, i: (bi * ni + i, 0))
    pos = pl.BlockSpec((tm, dk), lambda bi, i: (i, 0))
    post = pl.BlockSpec((dk // 2, tm), lambda bi, i: (0, i))
    body = functools.partial(_ret_in_body, hk=hk, hv=hv, dk=dk)
    return pl.pallas_call(
        body,
        grid=(b, ni),
        in_specs=[tok(d), _const_spec((1, d)), _const_spec(w.shape), _const_spec(wkt.shape), pos, pos, post, post],
        out_specs=[tok(hk), pl.BlockSpec((1, hk, tm), lambda bi, i: (bi, 0, i)), tok(hv), tok(hv)],
        out_shape=[jax.ShapeDtypeStruct((t, hk), BF16), jax.ShapeDtypeStruct((b, hk, l), BF16),
                   jax.ShapeDtypeStruct((t, hv), BF16), jax.ShapeDtypeStruct((t, hv), BF16)],
        compiler_params=_params("parallel", "parallel"),
        name="ret_in",
    )(x2, g.reshape(1, d), w, wkt, cos2, sin2, cost, sint)


def _ret_tables(c, dk, dv):
    lg_f = jnp.log1p(-jnp.exp2(-5.0 - jnp.arange(RET_HEADS, dtype=F32)))
    lg_b = lg_f[::-1]
    idx = jnp.arange(c)
    diff = (idx[:, None] - idx[None, :]).astype(F32)
    dmat = jnp.where(diff >= 0, jnp.exp(jnp.maximum(diff, 0.0)[None] * lg_f[:, None, None]),
                     jnp.exp(jnp.maximum(-diff, 0.0)[None] * lg_b[:, None, None]))
    pos = idx.astype(F32)
    ones_v = jnp.ones((1, 1, dv), F32)
    xi_f = jnp.exp((pos + 1.0)[None, :] * lg_f[:, None])[..., None] * ones_v
    xi_b = jnp.exp((c - pos)[None, :] * lg_b[:, None])[..., None] * ones_v
    zeta_f = jnp.exp((c - 1.0 - pos)[None, :] * lg_f[:, None])[:, None, :]
    zeta_b = jnp.exp(pos[None, :] * lg_b[:, None])[:, None, :]
    cd_f = jnp.exp(c * lg_f)[:, None, None] * ones_v
    cd_b = jnp.exp(c * lg_b)[:, None, None] * ones_v
    return dmat, xi_f, xi_b, zeta_f, zeta_b, cd_f, cd_b


def _ret_state_body(kt_ref, v_ref, zb_ref, cdb_ref, sb_out_ref, sb_ref, *, dk, dv):
    @pl.when(pl.program_id(1) == 0)
    def _():
        sb_ref[...] = jnp.zeros(sb_ref.shape, F32)

    cb = zb_ref.shape[2]
    for n in range(v_ref.shape[0] // cb - 1, -1, -1):
        rows = slice(n * cb, (n + 1) * cb)
        for h in range(RET_HEADS):
            st = sb_ref[h]
            sb_out_ref[0, n, h] = st.astype(BF16)
            kz = (kt_ref[0, h * dk:(h + 1) * dk, rows].astype(F32) * zb_ref[h]).astype(BF16)
            sb_ref[h] = cdb_ref[h] * st + _dot(kz, v_ref[rows, h * dv:(h + 1) * dv])


def _ret_state(kt, v, zeta_b, cd_b, *, b, l, tb):
    hk, hv = kt.shape[1], v.shape[1]
    dk, dv = hk // RET_HEADS, hv // RET_HEADS
    ni = l // tb
    nsub = tb // zeta_b.shape[2]
    bwd = lambda w: pl.BlockSpec((tb, w), lambda bi, i: (bi * ni + ni - 1 - i, 0))
    body = functools.partial(_ret_state_body, dk=dk, dv=dv)
    return pl.pallas_call(
        body,
        grid=(b, ni),
        in_specs=[pl.BlockSpec((1, hk, tb), lambda bi, i: (bi, 0, ni - 1 - i)), bwd(hv),
                  _const_spec(zeta_b.shape), _const_spec(cd_b.shape)],
        out_specs=pl.BlockSpec((1, nsub, RET_HEADS, dk, dv), lambda bi, i: (bi, ni - 1 - i, 0, 0, 0)),
        out_shape=jax.ShapeDtypeStruct((b, ni * nsub, RET_HEADS, dk, dv), BF16),
        scratch_shapes=[pltpu.VMEM((RET_HEADS, dk, dv), F32)],
        compiler_params=_params("parallel", "arbitrary"),
        name="ret_state",
    )(kt, v, zeta_b, cd_b)


def _ret_main_body(x_ref, q_ref, kt_ref, v_ref, og_ref, sb_ref, dmat_ref, xif_ref, xib_ref, zf_ref, cdf_ref,
                   gn_ref, wo_ref, o_ref, sf_ref, gated_ref, *, dk, dv):
    @pl.when(pl.program_id(1) == 0)
    def _():
        sf_ref[...] = jnp.zeros(sf_ref.shape, F32)

    cb = dmat_ref.shape[1]
    for n in range(q_ref.shape[0] // cb):
        rows = slice(n * cb, (n + 1) * cb)
        for h in range(RET_HEADS):
            ks = slice(h * dk, (h + 1) * dk)
            vs = slice(h * dv, (h + 1) * dv)
            q = q_ref[rows, ks]
            kt = kt_ref[0, ks, rows]
            v = v_ref[rows, vs]
            s = _dot(q, kt) * dmat_ref[h]
            st = sf_ref[h]
            o = (_dot(s.astype(BF16), v) + _dot(q, st.astype(BF16)) * xif_ref[h]
                 + _dot(q, sb_ref[0, n, h]) * xib_ref[h])
            kz = (kt.astype(F32) * zf_ref[h]).astype(BF16)
            sf_ref[h] = cdf_ref[h] * st + _dot(kz, v)
            o = o * lax.rsqrt(jnp.mean(o * o, axis=-1, keepdims=True) + EPS) * gn_ref[:, vs]
            gated_ref[rows, vs] = (o * _silu(og_ref[rows, vs].astype(F32))).astype(BF16)
    o_ref[...] = x_ref[...] + _dot(gated_ref[...], wo_ref[...])


def _ret_main(x2, q, kt, v, og, sb, tabs, gn, wo, *, b, l, tb):
    t, d = x2.shape
    hk, hv = q.shape[1], v.shape[1]
    dk, dv = hk // RET_HEADS, hv // RET_HEADS
    ni = l // tb
    nsub = tb // tabs[0].shape[1]
    tok = lambda w: pl.BlockSpec((tb, w), lambda bi, i: (bi * ni + i, 0))
    body = functools.partial(_ret_main_body, dk=dk, dv=dv)
    return pl.pallas_call(
        body,
        grid=(b, ni),
        in_specs=[tok(d), tok(hk), pl.BlockSpec((1, hk, tb), lambda bi, i: (bi, 0, i)), tok(hv), tok(hv),
                  pl.BlockSpec((1, nsub, RET_HEADS, dk, dv), lambda bi, i: (bi, i, 0, 0, 0))]
                 + [_const_spec(a.shape) for a in tabs] + [_const_spec((1, hv)), _const_spec(wo.shape)],
        out_specs=tok(d),
        out_shape=jax.ShapeDtypeStruct((t, d), F32),
        scratch_shapes=[pltpu.VMEM((RET_HEADS, dk, dv), F32), pltpu.VMEM((tb, hv), BF16)],
        compiler_params=_params("parallel", "arbitrary"),
        name="ret_main",
    )(x2, q, kt, v, og, sb, *tabs, gn.reshape(1, hv), wo)


def _tile(l, want):
    t = min(l, want)
    while l % t:
        t //= 2
    return t


def _s5_gla_layer(x2, mix_g, w_in, lam_re, lam_im, b_re, b_im, c_re, c_im, log_dt, d_skip, w_glu,
                  w_gk, b_gk, gla_norm, w_out, ffn2, *, b, l):
    su = d_skip.shape[0]
    hk = w_gk.shape[-1]
    r = w_gk.shape[1]
    hv = gla_norm.shape[0]
    tm = _tile(l, 1024)
    wut = w_in[:, :su].T.astype(BF16)
    wr = w_in[:, su:].astype(BF16)
    wgk = jnp.zeros((2 * r, 2 * hk), F32)
    wgk = wgk.at[:r, :hk].set(w_gk[0]).at[r:, hk:].set(w_gk[1]).astype(BF16)
    bgk = b_gk.reshape(1, 2 * hk).astype(F32)
    ut, q, k, v, og, gf, gb = _ab_in(x2, mix_g, wut, wr, wgk, bgk, b=b, l=l, tm=tm, hk=hk, hv=hv)

    nc = l // LANES
    taps, w_st, wc, pq = _s5_tables(lam_re, lam_im, b_re, b_im, c_re, c_im, log_dt, d_skip, nc)
    y4 = _s5(ut.reshape(b, su, nc, LANES), taps, w_st, wc, pq)
    yt = y4.reshape(b, su, l)

    o_f, o_b = _gla(q, k, v, gf, gb, b=b, l=l, tb=_tile(l, 512))
    return _ab_out(x2, yt, w_glu.T.astype(BF16), o_f, o_b, og, gla_norm, w_out.astype(BF16), ffn2,
                   b=b, l=l, tm=_tile(l, 512))


def _retention_layer(x2, mix_g, w_in, ret_norm, w_out, *, b, l):
    hv = ret_norm.shape[0]
    hk = (w_in.shape[1] - 2 * hv) // 2
    dk = hk // RET_HEADS
    half = dk // 2
    tm = _tile(l, 512)
    pos = jnp.arange(l, dtype=F32)
    inv = jnp.exp(-math.log(ROPE_BASE) * jnp.arange(half, dtype=F32) / half)
    ang = pos[:, None] * inv[None, :]
    cos, sin = jnp.cos(ang), jnp.sin(ang)
    cos2 = jnp.concatenate([cos, cos], axis=1)
    sin2 = jnp.concatenate([-sin, sin], axis=1)
    w_qvg = jnp.concatenate([w_in[:, :hk], w_in[:, 2 * hk:]], axis=1).astype(BF16)
    w_kt = w_in[:, hk:2 * hk].T.astype(BF16)
    q, kt, v, og = _ret_in(x2, mix_g, w_qvg, w_kt, cos2, sin2, cos.T, sin.T, b=b, l=l, tm=tm, hk=hk, hv=hv)
    tb = _tile(l, 256)
    dmat, xi_f, xi_b, zeta_f, zeta_b, cd_f, cd_b = _ret_tables(tb, dk, hv // RET_HEADS)
    sb = _ret_state(kt, v, zeta_b, cd_b, b=b, l=l, tb=_tile(l, 4 * tb))
    return _ret_main(x2, q, kt, v, og, sb, (dmat, xi_f, xi_b, zeta_f, cd_f), ret_norm, w_out.astype(BF16),
                     b=b, l=l, tb=_tile(l, 2 * tb))


def kernel(x, ffn1_norm, ffn1_w1, ffn1_w2, mix_norm, ffn2_norm, ffn2_w1, ffn2_w2, ab_w_in, s5_lambda_re, s5_lambda_im, s5_b_re, s5_b_im, s5_c_re, s5_c_im, s5_log_dt, s5_d, s5_w_glu, gla_w_gk, gla_b_gk, gla_norm, ab_w_out, ret_w_in, ret_norm, ret_w_out, final_norm):
    b, l, d = x.shape
    depth = ffn1_norm.shape[0]
    assert depth % 2 == 0, "the final norm is applied by the retention layer's second FFN"
    tm = _tile(b * l, 1024)
    x2 = x.reshape(b * l, d).astype(F32)
    for i in range(depth):
        j = i // 2
        ffn1 = (ffn1_norm[i], ffn1_w1[i].astype(BF16), ffn1_w2[i].astype(BF16))
        ffn2 = (ffn2_norm[i], ffn2_w1[i].astype(BF16), ffn2_w2[i].astype(BF16))
        x2 = _ffn(x2, *ffn1, final_norm, final=False, tm=tm)
        if i % 2 == 0:
            x2 = _s5_gla_layer(x2, mix_norm[i], ab_w_in[j], s5_lambda_re[j], s5_lambda_im[j], s5_b_re[j],
                               s5_b_im[j], s5_c_re[j], s5_c_im[j], s5_log_dt[j], s5_d[j], s5_w_glu[j],
                               gla_w_gk[j], gla_b_gk[j], gla_norm[j], ab_w_out[j], ffn2, b=b, l=l)
        else:
            x2 = _retention_layer(x2, mix_norm[i], ret_w_in[j], ret_norm[j], ret_w_out[j], b=b, l=l)
            x2 = _ffn(x2, *ffn2, final_norm, final=(i == depth - 1), tm=tm)
    return x2.reshape(b, l, d)
```

```python
import functools
import math

import jax
import jax.numpy as jnp
from jax import lax
from jax.experimental import pallas as pl
from jax.experimental.pallas import tpu as pltpu

F32 = jnp.float32
BF16 = jnp.bfloat16

EPS = 1e-6
S5_GROUP = 16
S5_STATE = 64
GLA_HEADS = 4
GLA_RANK = 16
GLA_GATE_NORM = 16.0
GLA_CHUNK = 64
RET_HEADS = 8
ROPE_BASE = 10000.0

LANES = 128
VMEM_LIMIT = 56 * 1024 * 1024
HI = lax.Precision.HIGHEST


def _const_spec(shape):
    nd = len(shape)
    return pl.BlockSpec(shape, lambda *_: (0,) * nd, pipeline_mode=pl.Buffered(1))


def _params(*sem):
    return pltpu.CompilerParams(dimension_semantics=sem, vmem_limit_bytes=VMEM_LIMIT)


def _rmsnorm(x, g):
    return x * lax.rsqrt(jnp.mean(x * x, axis=-1, keepdims=True) + EPS) * g


def _sigmoid(x):
    return 1.0 / (1.0 + jnp.exp(-x))


def _silu(x):
    return x * _sigmoid(x)


def _gelu_tanh(x):
    return 0.5 * x * (1.0 + jnp.tanh(math.sqrt(2.0 / math.pi) * (x + 0.044715 * (x * x * x))))


def _log_sigmoid(x):
    return jnp.minimum(x, 0.0) - jnp.log1p(jnp.exp(-jnp.abs(x)))


def _dot(a, b):
    return jnp.dot(a, b, preferred_element_type=F32)


def _dot_nt(a, b):
    return lax.dot_general(a, b, (((1,), (1,)), ((), ())), preferred_element_type=F32)


def _dot_tn(a, b):
    return lax.dot_general(a, b, (((0,), (0,)), ((), ())), preferred_element_type=F32)


def _ffn_apply(x, g, w1_ref, w2_ref):
    d_ff = w2_ref.shape[0]
    f_chunk = 256 if d_ff % 256 == 0 else d_ff
    xn = _rmsnorm(x, g).astype(BF16)
    acc = jnp.zeros(x.shape, F32)
    for f0 in range(0, d_ff, f_chunk):
        gate = _dot(xn, w1_ref[:, f0:f0 + f_chunk])
        up = _dot(xn, w1_ref[:, d_ff + f0:d_ff + f0 + f_chunk])
        act = (_silu(gate) * up).astype(BF16)
        acc = acc + _dot(act, w2_ref[f0:f0 + f_chunk, :])
    return x + 0.5 * acc


def _ffn_body(x_ref, g_ref, w1_ref, w2_ref, fg_ref, o_ref, *, final):
    y = _ffn_apply(x_ref[...], g_ref[...], w1_ref, w2_ref)
    if final:
        y = _rmsnorm(y, fg_ref[...])
    o_ref[...] = y


def _ffn(x2, g, w1, w2, fg, *, final, tm):
    t, d = x2.shape
    body = functools.partial(_ffn_body, final=final)
    return pl.pallas_call(
        body,
        grid=(t // tm,),
        in_specs=[pl.BlockSpec((tm, d), lambda i: (i, 0)),
                  _const_spec((1, d)), _const_spec(w1.shape), _const_spec(w2.shape),
                  _const_spec((1, d))],
        out_specs=pl.BlockSpec((tm, d), lambda i: (i, 0)),
        out_shape=jax.ShapeDtypeStruct((t, d), F32),
        compiler_params=_params("parallel"),
        name="ffn",
    )(x2, g.reshape(1, d), w1, w2, fg.reshape(1, d))


def _ab_in_body(x_ref, g_ref, wut_ref, wr_ref, wgk_ref, bgk_ref,
                ut_ref, q_ref, k_ref, v_ref, og_ref, gf_ref, gb_ref, *, hk, hv):
    h = _rmsnorm(x_ref[...], g_ref[...]).astype(BF16)
    ut_ref[0] = _dot_nt(wut_ref[...], h).astype(BF16)
    proj = _dot(h, wr_ref[...])
    q_ref[...] = proj[:, :hk] * (hk // GLA_HEADS) ** -0.5
    k_ref[...] = proj[:, hk:2 * hk]
    v_ref[...] = proj[:, 2 * hk:2 * hk + hv].astype(BF16)
    og_ref[...] = proj[:, 2 * hk + hv:2 * hk + 2 * hv].astype(BF16)
    glo = proj[:, 2 * hk + 2 * hv:].astype(BF16)
    gk = _log_sigmoid(_dot(glo, wgk_ref[...]) + bgk_ref[...]) * (1.0 / GLA_GATE_NORM)
    gf_ref[...] = gk[:, :hk]
    gb_ref[...] = gk[:, hk:]


def _ab_in(x2, g, wut, wr, wgk, bgk, *, b, l, tm, hk, hv):
    t, d = x2.shape
    su = wut.shape[0]
    ni = l // tm
    tok = lambda w: pl.BlockSpec((tm, w), lambda bi, i: (bi * ni + i, 0))
    body = functools.partial(_ab_in_body, hk=hk, hv=hv)
    return pl.pallas_call(
        body,
        grid=(b, ni),
        in_specs=[tok(d), _const_spec((1, d)), _const_spec(wut.shape), _const_spec(wr.shape),
                  _const_spec(wgk.shape), _const_spec(bgk.shape)],
        out_specs=[pl.BlockSpec((1, su, tm), lambda bi, i: (bi, 0, i)),
                   tok(hk), tok(hk), tok(hv), tok(hv), tok(hk), tok(hk)],
        out_shape=[jax.ShapeDtypeStruct((b, su, l), BF16),
                   jax.ShapeDtypeStruct((t, hk), F32), jax.ShapeDtypeStruct((t, hk), F32),
                   jax.ShapeDtypeStruct((t, hv), BF16), jax.ShapeDtypeStruct((t, hv), BF16),
                   jax.ShapeDtypeStruct((t, hk), F32), jax.ShapeDtypeStruct((t, hk), F32)],
        compiler_params=_params("parallel", "parallel"),
        name="ab_in",
    )(x2, g.reshape(1, d), wut, wr, wgk, bgk)


def _cpow_table(ar, ai, count):
    ks = jnp.arange(count)
    pr = jnp.ones(ar.shape + (count,), F32)
    pi = jnp.zeros(ar.shape + (count,), F32)
    sr, si = ar, ai
    for bit in range(max(1, (count - 1).bit_length())):
        on = ((ks >> bit) & 1) == 1
        nr = pr * sr[..., None] - pi * si[..., None]
        ni = pr * si[..., None] + pi * sr[..., None]
        pr = jnp.where(on, nr, pr)
        pi = jnp.where(on, ni, pi)
        sr, si = sr * sr - si * si, 2.0 * sr * si
    return pr, pi


def _s5_tables(lam_re, lam_im, b_re, b_im, c_re, c_im, log_dt, d_skip, n_chunks):
    ch = LANES
    g_, n_ = lam_re.shape[1], lam_re.shape[2]
    p_ = b_re.shape[-1]
    f32 = lambda a: a.astype(F32)
    lr = jnp.minimum(f32(lam_re), -1e-4)
    li = f32(lam_im)
    dt = jnp.exp(f32(log_dt))[..., None]
    mag = jnp.exp(lr * dt)
    ar = mag * jnp.cos(li * dt)
    ai = mag * jnp.sin(li * dt)
    den = lr * lr + li * li
    cr = ((ar - 1.0) * lr + ai * li) / den
    ci = (ai * lr - (ar - 1.0) * li) / den
    br, bi = f32(b_re), f32(b_im)
    bbr = cr[..., None] * br - ci[..., None] * bi
    bbi = cr[..., None] * bi + ci[..., None] * br
    ccr, cci = f32(c_re), f32(c_im)
    pwr, pwi = _cpow_table(ar, ai, ch + 1)

    cnr = ccr.transpose(0, 1, 3, 2)[..., None]
    cni = cci.transpose(0, 1, 3, 2)[..., None]
    cbr = cnr * bbr[:, :, :, None, :] - cni * bbi[:, :, :, None, :]
    cbi = cnr * bbi[:, :, :, None, :] + cni * bbr[:, :, :, None, :]
    taps = (jnp.einsum('dgnpq,dgnl->dgqpl', cbr, pwr[..., :ch], precision=HI)
            - jnp.einsum('dgnpq,dgnl->dgqpl', cbi, pwi[..., :ch], precision=HI))
    kf, kb = taps[0], taps[1]
    dmat = f32(d_skip).reshape(g_, p_)[:, None, :, None] * jnp.eye(p_, dtype=F32)[None, :, :, None]
    center = kf[..., 0:1] + kb[..., 0:1] + dmat
    kfull = jnp.concatenate([kb[..., :0:-1], center, kf[..., 1:]], axis=-1)
    kpad = jnp.pad(kfull, ((0, 0),) * 3 + ((0, 1),))

    def win(d, rev):
        pr = pwr[d][..., :ch]
        pi = pwi[d][..., :ch]
        if rev:
            pr, pi = pr[..., ::-1], pi[..., ::-1]
        pr = pr.transpose(0, 2, 1)[:, None]
        pi = pi.transpose(0, 2, 1)[:, None]
        xr = bbr[d].transpose(0, 2, 1)[:, :, None, :]
        xi = bbi[d].transpose(0, 2, 1)[:, :, None, :]
        return pr * xr - pi * xi, pr * xi + pi * xr
    fre, fim = win(0, True)
    bre, bim = win(1, False)
    w_in = jnp.concatenate([fre, fim, bre, bim], axis=-1).reshape(g_, p_ * ch, 4 * n_).astype(BF16)

    def wout(d, rev):
        pr = pwr[d][..., 1:ch + 1]
        pi = pwi[d][..., 1:ch + 1]
        if rev:
            pr, pi = pr[..., ::-1], pi[..., ::-1]
        pr = pr[:, :, None, :]
        pi = pi[:, :, None, :]
        xr = ccr[d].transpose(0, 2, 1)[..., None]
        xi = cci[d].transpose(0, 2, 1)[..., None]
        return xr * pr - xi * pi, -(xr * pi + xi * pr)
    fre, fim = wout(0, False)
    bre, bim = wout(1, True)
    wc = jnp.concatenate([fre, fim, bre, bim], axis=1).reshape(g_, 4 * n_, p_ * ch).astype(BF16)

    steps = max(1, (n_chunks - 1).bit_length())
    sr, si = pwr[..., ch], pwi[..., ch]
    rows = []
    for _ in range(steps):
        rows.append(jnp.stack([jnp.concatenate([sr, sr], -1), jnp.concatenate([-si, si], -1)], axis=2))
        sr, si = sr * sr - si * si, 2.0 * sr * si
    pq = jnp.stack(rows, axis=2)
    pq = pq.transpose(1, 0, 2, 3, 4)
    return kpad, w_in, wc, pq


def _s5_body(u_ref, taps_ref, win_ref, wc_ref, pq_ref, y_ref, toep_ref, *, nb, nc, p, n, steps):
    ch = LANES
    m = nb * nc
    u = jnp.concatenate([u_ref[:, q].reshape(m, ch) for q in range(p)], axis=1)
    r = _dot(u, win_ref[0])
    c = lax.broadcasted_iota(jnp.int32, (m, 2 * n), 0) % nc

    def scan(x, d):
        for k in range(steps):
            s = 1 << k
            if d == 0:
                t = jnp.where(c >= s, pltpu.roll(x, s, axis=0), 0.0)
            else:
                t = jnp.where(c < nc - s, pltpu.roll(x, m - s, axis=0), 0.0)
            x = x + t * pq_ref[0, d, k, 0:1, :] + pltpu.roll(t, n, axis=1) * pq_ref[0, d, k, 1:2, :]
        if d == 0:
            return jnp.where(c >= 1, pltpu.roll(x, 1, axis=0), 0.0)
        return jnp.where(c < nc - 1, pltpu.roll(x, m - 1, axis=0), 0.0)

    xf = scan(r[:, :2 * n], 0)
    xb = scan(r[:, 2 * n:], 1)
    xc = jnp.concatenate([xf, xb], axis=1).astype(BF16)
    for p0 in range(0, p, 2):
        cols = slice(p0 * ch, (p0 + 2) * ch)
        for q in range(p):
            for pp in (p0, p0 + 1):
                row = jnp.broadcast_to(taps_ref[0, q, pp:pp + 1, :], (ch, 2 * ch))
                blk = pltpu.roll(row, ch + 1, 1, stride=1, stride_axis=0)[:, :ch]
                toep_ref[q * ch:(q + 1) * ch, pp * ch:(pp + 1) * ch] = blk.astype(BF16)
        y = _dot(u, toep_ref[:, cols]) + _dot(xc, wc_ref[0, :, cols])
        y_ref[:, p0] = y[:, :ch].reshape(nb, nc, ch)
        y_ref[:, p0 + 1] = y[:, ch:].reshape(nb, nc, ch)


def _s5(u4, taps, w_in, wc, pq):
    nb, su, nc, ch = u4.shape
    g_ = taps.shape[0]
    p = su // g_
    n = wc.shape[1] // 4
    steps = pq.shape[2]
    body = functools.partial(_s5_body, nb=nb, nc=nc, p=p, n=n, steps=steps)
    return pl.pallas_call(
        body,
        grid=(g_,),
        in_specs=[pl.BlockSpec((nb, p, nc, ch), lambda g: (0, g, 0, 0)),
                  pl.BlockSpec((1,) + taps.shape[1:], lambda g: (g, 0, 0, 0)),
                  pl.BlockSpec((1,) + w_in.shape[1:], lambda g: (g, 0, 0)),
                  pl.BlockSpec((1,) + wc.shape[1:], lambda g: (g, 0, 0)),
                  pl.BlockSpec((1,) + pq.shape[1:], lambda g: (g, 0, 0, 0, 0))],
        out_specs=pl.BlockSpec((nb, p, nc, ch), lambda g: (0, g, 0, 0)),
        out_shape=jax.ShapeDtypeStruct((nb, su, nc, ch), F32),
        scratch_shapes=[pltpu.VMEM((p * ch, p * ch), BF16)],
        compiler_params=_params("parallel"),
        name="s5",
    )(u4, taps, w_in, wc, pq)


def _gla_prep(q, k, v, g, tri, cmask, bdk, bd, ones, last_row):
    g_hi = g.astype(BF16)
    g_lo = (g - g_hi.astype(F32)).astype(BF16)
    cum = _dot(tri, g_hi) + _dot(tri, g_lo)
    last = cum[last_row:last_row + 1, :]
    q_dec = (q * jnp.exp(cum)).astype(BF16)
    k_inv = (k * jnp.exp(-cum)).astype(BF16)
    k_dec = (k * jnp.exp(last - cum)).astype(BF16)
    kb = jnp.where(bdk, jnp.concatenate([k_inv] * GLA_HEADS, axis=0), 0.0)
    s = jnp.where(cmask, _dot_nt(q_dec, kb), 0.0).astype(BF16)
    vb = jnp.where(bd, jnp.concatenate([v] * GLA_HEADS, axis=0), 0.0)
    o_intra = _dot(s, vb)
    tot = _dot_tn(g_hi, ones) + _dot_tn(g_lo, ones)
    kv = jnp.where(bd, _dot_tn(k_dec, v), 0.0)
    return q_dec, o_intra, kv, jnp.exp(tot)


def _gla_sweep(preps, order, st_ref, o_ref, c):
    st = st_ref[...]
    for n in order:
        q_dec, o_intra, kv, decay = preps[n]
        o_ref[n * c:(n + 1) * c, :] = o_intra + _dot(q_dec, st.astype(BF16))
        st = jnp.concatenate([decay] * (st.shape[1] // LANES), axis=1) * st + kv
    st_ref[...] = st


def _gla_body(qf_ref, kf_ref, vf_ref, gf_ref, qb_ref, kb_ref, vb_ref, gb_ref,
              of_ref, ob_ref, sf_ref, sb_ref, *, tb, dk, dv):
    c = GLA_CHUNK

    @pl.when(pl.program_id(1) == 0)
    def _():
        sf_ref[...] = jnp.zeros(sf_ref.shape, F32)
        sb_ref[...] = jnp.zeros(sb_ref.shape, F32)

    ii = lax.broadcasted_iota(jnp.int32, (c, c), 0)
    jj = lax.broadcasted_iota(jnp.int32, (c, c), 1)
    tri_f = (jj <= ii).astype(BF16)
    tri_b = (jj >= ii).astype(BF16)
    hc = GLA_HEADS * c
    i2 = lax.broadcasted_iota(jnp.int32, (c, hc), 0)
    j2 = lax.broadcasted_iota(jnp.int32, (c, hc), 1) % c
    cmask_f = j2 <= i2
    cmask_b = j2 > i2
    bdk = (lax.broadcasted_iota(jnp.int32, (hc, GLA_HEADS * dk), 0) // c
           == lax.broadcasted_iota(jnp.int32, (hc, GLA_HEADS * dk), 1) // dk)
    bd = (lax.broadcasted_iota(jnp.int32, (hc, GLA_HEADS * dv), 0) // c
          == lax.broadcasted_iota(jnp.int32, (hc, GLA_HEADS * dv), 1) // dv)
    ones = jnp.ones((c, LANES), BF16)
    nch = tb // c
    preps_f, preps_b = [], []
    for n in range(nch):
        r = slice(n * c, (n + 1) * c)
        preps_f.append(_gla_prep(qf_ref[r, :], kf_ref[r, :], vf_ref[r, :], gf_ref[r, :],
                                 tri_f, cmask_f, bdk, bd, ones, c - 1))
        preps_b.append(_gla_prep(qb_ref[r, :], kb_ref[r, :], vb_ref[r, :], gb_ref[r, :],
                                 tri_b, cmask_b, bdk, bd, ones, 0))
    _gla_sweep(preps_f, range(nch), sf_ref, of_ref, c)
    _gla_sweep(preps_b, range(nch - 1, -1, -1), sb_ref, ob_ref, c)


def _gla(q, k, v, gf, gb, *, b, l, tb):
    t, hk = q.shape
    hv = v.shape[1]
    dk, dv = hk // GLA_HEADS, hv // GLA_HEADS
    ni = l // tb
    fwd = lambda w: pl.BlockSpec((tb, w), lambda bi, i: (bi * ni + i, 0))
    bwd = lambda w: pl.BlockSpec((tb, w), lambda bi, i: (bi * ni + ni - 1 - i, 0))
    assert dk == GLA_CHUNK, "state mask reuse needs chunk == key head width"
    body = functools.partial(_gla_body, tb=tb, dk=dk, dv=dv)
    state = pltpu.VMEM((hk, hv), F32)
    return pl.pallas_call(
        body,
        grid=(b, ni),
        in_specs=[fwd(hk), fwd(hk), fwd(hv), fwd(hk), bwd(hk), bwd(hk), bwd(hv), bwd(hk)],
        out_specs=[fwd(hv), bwd(hv)],
        out_shape=[jax.ShapeDtypeStruct((t, hv), F32), jax.ShapeDtypeStruct((t, hv), F32)],
        scratch_shapes=[state, state],
        compiler_params=_params("parallel", "arbitrary"),
        name="gla",
    )(q, k, v, gf, q, k, v, gb)


def _head_norm(o, g, heads):
    dh = o.shape[1] // heads
    parts = []
    for h in range(heads):
        oh = o[:, h * dh:(h + 1) * dh]
        parts.append(oh * lax.rsqrt(jnp.mean(oh * oh, axis=-1, keepdims=True) + EPS))
    return jnp.concatenate(parts, axis=1) * g


def _ab_out_body(x_ref, yt_ref, wglut_ref, of_ref, ob_ref, og_ref, gn_ref, wo_ref, fg_ref, w1_ref, w2_ref,
                 o_ref, *, su):
    gy = _gelu_tanh(yt_ref[0])
    z = _dot(wglut_ref[...], gy.astype(BF16))
    s5_out = (gy * _sigmoid(z)).T.astype(BF16)
    o = _head_norm(of_ref[...] + ob_ref[...], gn_ref[...], GLA_HEADS)
    gla_out = (o * _silu(og_ref[...].astype(F32))).astype(BF16)
    x1 = x_ref[...] + _dot(s5_out, wo_ref[:su, :]) + _dot(gla_out, wo_ref[su:, :])
    o_ref[...] = _ffn_apply(x1, fg_ref[...], w1_ref, w2_ref)


def _ab_out(x2, yt, wglut, o_f, o_b, og, gn, wo, ffn, *, b, l, tm):
    t, d = x2.shape
    su = yt.shape[1]
    hv = o_f.shape[1]
    ni = l // tm
    fg, w1, w2 = ffn
    tok = lambda w: pl.BlockSpec((tm, w), lambda bi, i: (bi * ni + i, 0))
    body = functools.partial(_ab_out_body, su=su)
    return pl.pallas_call(
        body,
        grid=(b, ni),
        in_specs=[tok(d), pl.BlockSpec((1, su, tm), lambda bi, i: (bi, 0, i)), _const_spec(wglut.shape),
                  tok(hv), tok(hv), tok(hv), _const_spec((1, hv)), _const_spec(wo.shape),
                  _const_spec((1, d)), _const_spec(w1.shape), _const_spec(w2.shape)],
        out_specs=tok(d),
        out_shape=jax.ShapeDtypeStruct((t, d), F32),
        compiler_params=_params("parallel", "parallel"),
        name="ab_out_ffn",
    )(x2, yt, wglut, o_f, o_b, og, gn.reshape(1, hv), wo, fg.reshape(1, d), w1, w2)


def _ret_in_body(x_ref, g_ref, w_ref, wkt_ref, cos_ref, sin_ref, cost_ref, sint_ref,
                 q_ref, kt_ref, v_ref, og_ref, *, hk, hv, dk):
    h = _rmsnorm(x_ref[...], g_ref[...]).astype(BF16)
    cos2 = cos_ref[...]
    sin2 = sin_ref[...]
    q = _dot(h, w_ref[:, :hk])
    q_ref[...] = jnp.concatenate(
        [q[:, s:s + dk] * cos2 + pltpu.roll(q[:, s:s + dk], dk // 2, axis=1) * sin2 for s in range(0, hk, dk)],
        axis=1).astype(BF16)
    kt = _dot_nt(wkt_ref[...], h) * dk ** -0.5
    cost = cost_ref[...]
    sint = sint_ref[...]
    parts = []
    for s in range(0, hk, dk):
        t1 = kt[s:s + dk // 2]
        t2 = kt[s + dk // 2:s + dk]
        parts += [t1 * cost - t2 * sint, t1 * sint + t2 * cost]
    kt_ref[0] = jnp.concatenate(parts, axis=0).astype(BF16)
    v_ref[...] = _dot(h, w_ref[:, hk:hk + hv]).astype(BF16)
    og_ref[...] = _dot(h, w_ref[:, hk + hv:]).astype(BF16)


def _ret_in(x2, g, w, wkt, cos2, sin2, cost, sint, *, b, l, tm, hk, hv):
    t, d = x2.shape
    dk = hk // RET_HEADS
    ni = l // tm
    tok = lambda w_: pl.BlockSpec((tm, w_), lambda bi, i: (bi * ni + i, 0))
    pos = pl.BlockSpec((tm, dk), lambda bi, i: (i, 0))
    post = pl.BlockSpec((dk // 2, tm), lambda bi, i: (0, i))
    body = functools.partial(_ret_in_body, hk=hk, hv=hv, dk=dk)
    return pl.pallas_call(
        body,
        grid=(b, ni),
        in_specs=[tok(d), _const_spec((1, d)), _const_spec(w.shape), _const_spec(wkt.shape), pos, pos, post, post],
        out_specs=[tok(hk), pl.BlockSpec((1, hk, tm), lambda bi, i: (bi, 0, i)), tok(hv), tok(hv)],
        out_shape=[jax.ShapeDtypeStruct((t, hk), BF16), jax.ShapeDtypeStruct((b, hk, l), BF16),
                   jax.ShapeDtypeStruct((t, hv), BF16), jax.ShapeDtypeStruct((t, hv), BF16)],
        compiler_params=_params("parallel", "parallel"),
        name="ret_in",
    )(x2, g.reshape(1, d), w, wkt, cos2, sin2, cost, sint)


def _ret_tables(c, dk, dv):
    lg_f = jnp.log1p(-jnp.exp2(-5.0 - jnp.arange(RET_HEADS, dtype=F32)))
    lg_b = lg_f[::-1]
    idx = jnp.arange(c)
    diff = (idx[:, None] - idx[None, :]).astype(F32)
    dmat = jnp.where(diff >= 0, jnp.exp(jnp.maximum(diff, 0.0)[None] * lg_f[:, None, None]),
                     jnp.exp(jnp.maximum(-diff, 0.0)[None] * lg_b[:, None, None]))
    pos = idx.astype(F32)
    ones_v = jnp.ones((1, 1, dv), F32)
    xi_f = jnp.exp((pos + 1.0)[None, :] * lg_f[:, None])[..., None] * ones_v
    xi_b = jnp.exp((c - pos)[None, :] * lg_b[:, None])[..., None] * ones_v
    zeta_f = jnp.exp((c - 1.0 - pos)[None, :] * lg_f[:, None])[:, None, :]
    zeta_b = jnp.exp(pos[None, :] * lg_b[:, None])[:, None, :]
    cd_f = jnp.exp(c * lg_f)[:, None, None] * ones_v
    cd_b = jnp.exp(c * lg_b)[:, None, None] * ones_v
    return dmat, xi_f, xi_b, zeta_f, zeta_b, cd_f, cd_b


def _ret_state_body(kt_ref, v_ref, zb_ref, cdb_ref, sb_out_ref, sb_ref, *, dk, dv):
    @pl.when(pl.program_id(1) == 0)
    def _():
        sb_ref[...] = jnp.zeros(sb_ref.shape, F32)

    cb = zb_ref.shape[2]
    for n in range(v_ref.shape[0] // cb - 1, -1, -1):
        rows = slice(n * cb, (n + 1) * cb)
        for h in range(RET_HEADS):
            st = sb_ref[h]
            sb_out_ref[0, n, h] = st.astype(BF16)
            kz = (kt_ref[0, h * dk:(h + 1) * dk, rows].astype(F32) * zb_ref[h]).astype(BF16)
            sb_ref[h] = cdb_ref[h] * st + _dot(kz, v_ref[rows, h * dv:(h + 1) * dv])


def _ret_state(kt, v, zeta_b, cd_b, *, b, l, tb):
    hk, hv = kt.shape[1], v.shape[1]
    dk, dv = hk // RET_HEADS, hv // RET_HEADS
    ni = l // tb
    nsub = tb // zeta_b.shape[2]
    bwd = lambda w: pl.BlockSpec((tb, w), lambda bi, i: (bi * ni + ni - 1 - i, 0))
    body = functools.partial(_ret_state_body, dk=dk, dv=dv)
    return pl.pallas_call(
        body,
        grid=(b, ni),
        in_specs=[pl.BlockSpec((1, hk, tb), lambda bi, i: (bi, 0, ni - 1 - i)), bwd(hv),
                  _const_spec(zeta_b.shape), _const_spec(cd_b.shape)],
        out_specs=pl.BlockSpec((1, nsub, RET_HEADS, dk, dv), lambda bi, i: (bi, ni - 1 - i, 0, 0, 0)),
        out_shape=jax.ShapeDtypeStruct((b, ni * nsub, RET_HEADS, dk, dv), BF16),
        scratch_shapes=[pltpu.VMEM((RET_HEADS, dk, dv), F32)],
        compiler_params=_params("parallel", "arbitrary"),
        name="ret_state",
    )(kt, v, zeta_b, cd_b)


def _ret_main_body(x_ref, q_ref, kt_ref, v_ref, og_ref, sb_ref, dmat_ref, xif_ref, xib_ref, zf_ref, cdf_ref,
                   gn_ref, wo_ref, o_ref, sf_ref, gated_ref, *, dk, dv):
    @pl.when(pl.program_id(1) == 0)
    def _():
        sf_ref[...] = jnp.zeros(sf_ref.shape, F32)

    cb = dmat_ref.shape[1]
    for n in range(q_ref.shape[0] // cb):
        rows = slice(n * cb, (n + 1) * cb)
        for h in range(RET_HEADS):
            ks = slice(h * dk, (h + 1) * dk)
            vs = slice(h * dv, (h + 1) * dv)
            q = q_ref[rows, ks]
            kt = kt_ref[0, ks, rows]
            v = v_ref[rows, vs]
            s = _dot(q, kt) * dmat_ref[h]
            st = sf_ref[h]
            o = (_dot(s.astype(BF16), v) + _dot(q, st.astype(BF16)) * xif_ref[h]
                 + _dot(q, sb_ref[0, n, h]) * xib_ref[h])
            kz = (kt.astype(F32) * zf_ref[h]).astype(BF16)
            sf_ref[h] = cdf_ref[h] * st + _dot(kz, v)
            o = o * lax.rsqrt(jnp.mean(o * o, axis=-1, keepdims=True) + EPS) * gn_ref[:, vs]
            gated_ref[rows, vs] = (o * _silu(og_ref[rows, vs].astype(F32))).astype(BF16)
    o_ref[...] = x_ref[...] + _dot(gated_ref[...], wo_ref[...])


def _ret_main(x2, q, kt, v, og, sb, tabs, gn, wo, *, b, l, tb):
    t, d = x2.shape
    hk, hv = q.shape[1], v.shape[1]
    dk, dv = hk // RET_HEADS, hv // RET_HEADS
    ni = l // tb
    nsub = tb // tabs[0].shape[1]
    tok = lambda w: pl.BlockSpec((tb, w), lambda bi, i: (bi * ni + i, 0))
    body = functools.partial(_ret_main_body, dk=dk, dv=dv)
    return pl.pallas_call(
        body,
        grid=(b, ni),
        in_specs=[tok(d), tok(hk), pl.BlockSpec((1, hk, tb), lambda bi, i: (bi, 0, i)), tok(hv), tok(hv),
                  pl.BlockSpec((1, nsub, RET_HEADS, dk, dv), lambda bi, i: (bi, i, 0, 0, 0))]
                 + [_const_spec(a.shape) for a in tabs] + [_const_spec((1, hv)), _const_spec(wo.shape)],
        out_specs=tok(d),
        out_shape=jax.ShapeDtypeStruct((t, d), F32),
        scratch_shapes=[pltpu.VMEM((RET_HEADS, dk, dv), F32), pltpu.VMEM((tb, hv), BF16)],
        compiler_params=_params("parallel", "arbitrary"),
        name="ret_main",
    )(x2, q, kt, v, og, sb, *tabs, gn.reshape(1, hv), wo)


def _tile(l, want):
    t = min(l, want)
    while l % t:
        t //= 2
    return t


def _s5_gla_layer(x2, mix_g, w_in, lam_re, lam_im, b_re, b_im, c_re, c_im, log_dt, d_skip, w_glu,
                  w_gk, b_gk, gla_norm, w_out, ffn2, *, b, l):
    su = d_skip.shape[0]
    hk = w_gk.shape[-1]
    r = w_gk.shape[1]
    hv = gla_norm.shape[0]
    tm = _tile(l, 1024)
    wut = w_in[:, :su].T.astype(BF16)
    wr = w_in[:, su:].astype(BF16)
    wgk = jnp.zeros((2 * r, 2 * hk), F32)
    wgk = wgk.at[:r, :hk].set(w_gk[0]).at[r:, hk:].set(w_gk[1]).astype(BF16)
    bgk = b_gk.reshape(1, 2 * hk).astype(F32)
    ut, q, k, v, og, gf, gb = _ab_in(x2, mix_g, wut, wr, wgk, bgk, b=b, l=l, tm=tm, hk=hk, hv=hv)

    nc = l // LANES
    taps, w_st, wc, pq = _s5_tables(lam_re, lam_im, b_re, b_im, c_re, c_im, log_dt, d_skip, nc)
    y4 = _s5(ut.reshape(b, su, nc, LANES), taps, w_st, wc, pq)
    yt = y4.reshape(b, su, l)

    o_f, o_b = _gla(q, k, v, gf, gb, b=b, l=l, tb=_tile(l, 512))
    return _ab_out(x2, yt, w_glu.T.astype(BF16), o_f, o_b, og, gla_norm, w_out.astype(BF16), ffn2,
                   b=b, l=l, tm=_tile(l, 512))


def _retention_layer(x2, mix_g, w_in, ret_norm, w_out, *, b, l):
    hv = ret_norm.shape[0]
    hk = (w_in.shape[1] - 2 * hv) // 2
    dk = hk // RET_HEADS
    half = dk // 2
    tm = _tile(l, 512)
    pos = jnp.arange(l, dtype=F32)
    inv = jnp.exp(-math.log(ROPE_BASE) * jnp.arange(half, dtype=F32) / half)
    ang = pos[:, None] * inv[None, :]
    cos, sin = jnp.cos(ang), jnp.sin(ang)
    cos2 = jnp.concatenate([cos, cos], axis=1)
    sin2 = jnp.concatenate([-sin, sin], axis=1)
    w_qvg = jnp.concatenate([w_in[:, :hk], w_in[:, 2 * hk:]], axis=1).astype(BF16)
    w_kt = w_in[:, hk:2 * hk].T.astype(BF16)
    q, kt, v, og = _ret_in(x2, mix_g, w_qvg, w_kt, cos2, sin2, cos.T, sin.T, b=b, l=l, tm=tm, hk=hk, hv=hv)
    tb = _tile(l, 256)
    dmat, xi_f, xi_b, zeta_f, zeta_b, cd_f, cd_b = _ret_tables(tb, dk, hv // RET_HEADS)
    sb = _ret_state(kt, v, zeta_b, cd_b, b=b, l=l, tb=_tile(l, 4 * tb))
    return _ret_main(x2, q, kt, v, og, sb, (dmat, xi_f, xi_b, zeta_f, cd_f), ret_norm, w_out.astype(BF16),
                     b=b, l=l, tb=_tile(l, 2 * tb))


def kernel(x, ffn1_norm, ffn1_w1, ffn1_w2, mix_norm, ffn2_norm, ffn2_w1, ffn2_w2, ab_w_in, s5_lambda_re, s5_lambda_im, s5_b_re, s5_b_im, s5_c_re, s5_c_im, s5_log_dt, s5_d, s5_w_glu, gla_w_gk, gla_b_gk, gla_norm, ab_w_out, ret_w_in, ret_norm, ret_w_out, final_norm):
    b, l, d = x.shape
    depth = ffn1_norm.shape[0]
    assert depth % 2 == 0, "the final norm is applied by the retention layer's second FFN"
    tm = _tile(b * l, 1024)
    x2 = x.reshape(b * l, d).astype(F32)
    for i in range(depth):
        j = i // 2
        ffn1 = (ffn1_norm[i], ffn1_w1[i].astype(BF16), ffn1_w2[i].astype(BF16))
        ffn2 = (ffn2_norm[i], ffn2_w1[i].astype(BF16), ffn2_w2[i].astype(BF16))
        x2 = _ffn(x2, *ffn1, final_norm, final=False, tm=tm)
        if i % 2 == 0:
            x2 = _s5_gla_layer(x2, mix_norm[i], ab_w_in[j], s5_lambda_re[j], s5_lambda_im[j], s5_b_re[j],
                               s5_b_im[j], s5_c_re[j], s5_c_im[j], s5_log_dt[j], s5_d[j], s5_w_glu[j],
                               gla_w_gk[j], gla_b_gk[j], gla_norm[j], ab_w_out[j], ffn2, b=b, l=l)
        else:
            x2 = _retention_layer(x2, mix_norm[i], ret_w_in[j], ret_norm[j], ret_w_out[j], b=b, l=l)
            x2 = _ffn(x2, *ffn2, final_norm, final=(i == depth - 1), tm=tm)
    return x2.reshape(b, l, d)
```

```python
import functools
import math

import jax
import jax.numpy as jnp
from jax import lax
from jax.experimental import pallas as pl
from jax.experimental.pallas import tpu as pltpu

F32 = jnp.float32
BF16 = jnp.bfloat16

EPS = 1e-6
S5_GROUP = 16
S5_STATE = 64
GLA_HEADS = 4
GLA_RANK = 16
GLA_GATE_NORM = 16.0
GLA_CHUNK = 64
RET_HEADS = 8
ROPE_BASE = 10000.0

LANES = 128
VMEM_LIMIT = 56 * 1024 * 1024
HI = lax.Precision.HIGHEST


def _const_spec(shape):
    nd = len(shape)
    return pl.BlockSpec(shape, lambda *_: (0,) * nd, pipeline_mode=pl.Buffered(1))


def _params(*sem):
    return pltpu.CompilerParams(dimension_semantics=sem, vmem_limit_bytes=VMEM_LIMIT)


def _rmsnorm(x, g):
    return x * lax.rsqrt(jnp.mean(x * x, axis=-1, keepdims=True) + EPS) * g


def _sigmoid(x):
    return 1.0 / (1.0 + jnp.exp(-x))


def _silu(x):
    return x * _sigmoid(x)


def _gelu_tanh(x):
    return 0.5 * x * (1.0 + jnp.tanh(math.sqrt(2.0 / math.pi) * (x + 0.044715 * (x * x * x))))


def _log_sigmoid(x):
    return jnp.minimum(x, 0.0) - jnp.log1p(jnp.exp(-jnp.abs(x)))


def _dot(a, b):
    return jnp.dot(a, b, preferred_element_type=F32)


def _dot_nt(a, b):
    return lax.dot_general(a, b, (((1,), (1,)), ((), ())), preferred_element_type=F32)


def _dot_tn(a, b):
    return lax.dot_general(a, b, (((0,), (0,)), ((), ())), preferred_element_type=F32)


def _ffn_apply(x, g, w1_ref, w2_ref):
    d_ff = w2_ref.shape[0]
    f_chunk = 256 if d_ff % 256 == 0 else d_ff
    xn = _rmsnorm(x, g).astype(BF16)
    acc = jnp.zeros(x.shape, F32)
    for f0 in range(0, d_ff, f_chunk):
        gate = _dot(xn, w1_ref[:, f0:f0 + f_chunk])
        up = _dot(xn, w1_ref[:, d_ff + f0:d_ff + f0 + f_chunk])
        act = (_silu(gate) * up).astype(BF16)
        acc = acc + _dot(act, w2_ref[f0:f0 + f_chunk, :])
    return x + 0.5 * acc


def _ffn_body(x_ref, g_ref, w1_ref, w2_ref, fg_ref, o_ref, *, final):
    half = x_ref.shape[0] // 2
    for rows in (slice(0, half), slice(half, 2 * half)):
        y = _ffn_apply(x_ref[rows, :], g_ref[...], w1_ref, w2_ref)
        if final:
            y = _rmsnorm(y, fg_ref[...])
        o_ref[rows, :] = y


def _ffn(x2, g, w1, w2, fg, *, final, tm):
    t, d = x2.shape
    body = functools.partial(_ffn_body, final=final)
    return pl.pallas_call(
        body,
        grid=(t // tm,),
        in_specs=[pl.BlockSpec((tm, d), lambda i: (i, 0)),
                  _const_spec((1, d)), _const_spec(w1.shape), _const_spec(w2.shape),
                  _const_spec((1, d))],
        out_specs=pl.BlockSpec((tm, d), lambda i: (i, 0)),
        out_shape=jax.ShapeDtypeStruct((t, d), F32),
        compiler_params=_params("parallel"),
        name="ffn",
    )(x2, g.reshape(1, d), w1, w2, fg.reshape(1, d))


def _ab_in_body(x_ref, g_ref, wut_ref, wr_ref, wgk_ref, bgk_ref,
                ut_ref, q_ref, k_ref, v_ref, og_ref, gf_ref, gb_ref, *, hk, hv):
    h = _rmsnorm(x_ref[...], g_ref[...]).astype(BF16)
    ut_ref[0] = _dot_nt(wut_ref[...], h).astype(BF16)
    proj = _dot(h, wr_ref[...])
    q_ref[...] = proj[:, :hk] * (hk // GLA_HEADS) ** -0.5
    k_ref[...] = proj[:, hk:2 * hk]
    v_ref[...] = proj[:, 2 * hk:2 * hk + hv].astype(BF16)
    og_ref[...] = proj[:, 2 * hk + hv:2 * hk + 2 * hv].astype(BF16)
    glo = proj[:, 2 * hk + 2 * hv:].astype(BF16)
    gk = _log_sigmoid(_dot(glo, wgk_ref[...]) + bgk_ref[...]) * (1.0 / GLA_GATE_NORM)
    gf_ref[...] = gk[:, :hk]
    gb_ref[...] = gk[:, hk:]


def _ab_in(x2, g, wut, wr, wgk, bgk, *, b, l, tm, hk, hv):
    t, d = x2.shape
    su = wut.shape[0]
    ni = l // tm
    tok = lambda w: pl.BlockSpec((tm, w), lambda bi, i: (bi * ni + i, 0))
    body = functools.partial(_ab_in_body, hk=hk, hv=hv)
    return pl.pallas_call(
        body,
        grid=(b, ni),
        in_specs=[tok(d), _const_spec((1, d)), _const_spec(wut.shape), _const_spec(wr.shape),
                  _const_spec(wgk.shape), _const_spec(bgk.shape)],
        out_specs=[pl.BlockSpec((1, su, tm), lambda bi, i: (bi, 0, i)),
                   tok(hk), tok(hk), tok(hv), tok(hv), tok(hk), tok(hk)],
        out_shape=[jax.ShapeDtypeStruct((b, su, l), BF16),
                   jax.ShapeDtypeStruct((t, hk), F32), jax.ShapeDtypeStruct((t, hk), F32),
                   jax.ShapeDtypeStruct((t, hv), BF16), jax.ShapeDtypeStruct((t, hv), BF16),
                   jax.ShapeDtypeStruct((t, hk), F32), jax.ShapeDtypeStruct((t, hk), F32)],
        compiler_params=_params("parallel", "parallel"),
        name="ab_in",
    )(x2, g.reshape(1, d), wut, wr, wgk, bgk)


def _cpow_table(ar, ai, count):
    ks = jnp.arange(count)
    pr = jnp.ones(ar.shape + (count,), F32)
    pi = jnp.zeros(ar.shape + (count,), F32)
    sr, si = ar, ai
    for bit in range(max(1, (count - 1).bit_length())):
        on = ((ks >> bit) & 1) == 1
        nr = pr * sr[..., None] - pi * si[..., None]
        ni = pr * si[..., None] + pi * sr[..., None]
        pr = jnp.where(on, nr, pr)
        pi = jnp.where(on, ni, pi)
        sr, si = sr * sr - si * si, 2.0 * sr * si
    return pr, pi


def _s5_tables(lam_re, lam_im, b_re, b_im, c_re, c_im, log_dt, d_skip, n_chunks):
    ch = LANES
    g_, n_ = lam_re.shape[1], lam_re.shape[2]
    p_ = b_re.shape[-1]
    f32 = lambda a: a.astype(F32)
    lr = jnp.minimum(f32(lam_re), -1e-4)
    li = f32(lam_im)
    dt = jnp.exp(f32(log_dt))[..., None]
    mag = jnp.exp(lr * dt)
    ar = mag * jnp.cos(li * dt)
    ai = mag * jnp.sin(li * dt)
    den = lr * lr + li * li
    cr = ((ar - 1.0) * lr + ai * li) / den
    ci = (ai * lr - (ar - 1.0) * li) / den
    br, bi = f32(b_re), f32(b_im)
    bbr = cr[..., None] * br - ci[..., None] * bi
    bbi = cr[..., None] * bi + ci[..., None] * br
    ccr, cci = f32(c_re), f32(c_im)
    pwr, pwi = _cpow_table(ar, ai, ch + 1)

    cnr = ccr.transpose(0, 1, 3, 2)[..., None]
    cni = cci.transpose(0, 1, 3, 2)[..., None]
    cbr = cnr * bbr[:, :, :, None, :] - cni * bbi[:, :, :, None, :]
    cbi = cnr * bbi[:, :, :, None, :] + cni * bbr[:, :, :, None, :]
    taps = (jnp.einsum('dgnpq,dgnl->dgqpl', cbr, pwr[..., :ch], precision=HI)
            - jnp.einsum('dgnpq,dgnl->dgqpl', cbi, pwi[..., :ch], precision=HI))
    kf, kb = taps[0], taps[1]
    dmat = f32(d_skip).reshape(g_, p_)[:, None, :, None] * jnp.eye(p_, dtype=F32)[None, :, :, None]
    center = kf[..., 0:1] + kb[..., 0:1] + dmat
    kfull = jnp.concatenate([kb[..., :0:-1], center, kf[..., 1:]], axis=-1)
    kpad = jnp.pad(kfull, ((0, 0),) * 3 + ((0, 1),))

    def win(d, rev):
        pr = pwr[d][..., :ch]
        pi = pwi[d][..., :ch]
        if rev:
            pr, pi = pr[..., ::-1], pi[..., ::-1]
        pr = pr.transpose(0, 2, 1)[:, None]
        pi = pi.transpose(0, 2, 1)[:, None]
        xr = bbr[d].transpose(0, 2, 1)[:, :, None, :]
        xi = bbi[d].transpose(0, 2, 1)[:, :, None, :]
        return pr * xr - pi * xi, pr * xi + pi * xr
    fre, fim = win(0, True)
    bre, bim = win(1, False)
    w_in = jnp.concatenate([fre, fim, bre, bim], axis=-1).reshape(g_, p_ * ch, 4 * n_).astype(BF16)

    def wout(d, rev):
        pr = pwr[d][..., 1:ch + 1]
        pi = pwi[d][..., 1:ch + 1]
        if rev:
            pr, pi = pr[..., ::-1], pi[..., ::-1]
        pr = pr[:, :, None, :]
        pi = pi[:, :, None, :]
        xr = ccr[d].transpose(0, 2, 1)[..., None]
        xi = cci[d].transpose(0, 2, 1)[..., None]
        return xr * pr - xi * pi, -(xr * pi + xi * pr)
    fre, fim = wout(0, False)
    bre, bim = wout(1, True)
    wc = jnp.concatenate([fre, fim, bre, bim], axis=1).reshape(g_, 4 * n_, p_ * ch).astype(BF16)

    steps = max(1, (n_chunks - 1).bit_length())
    sr, si = pwr[..., ch], pwi[..., ch]
    rows = []
    for _ in range(steps):
        rows.append(jnp.stack([jnp.concatenate([sr, sr], -1), jnp.concatenate([-si, si], -1)], axis=2))
        sr, si = sr * sr - si * si, 2.0 * sr * si
    pq = jnp.stack(rows, axis=2)
    pq = pq.transpose(1, 0, 2, 3, 4)
    return kpad, w_in, wc, pq


def _s5_body(u_ref, taps_ref, win_ref, wc_ref, pq_ref, y_ref, toep_ref, *, nb, nc, p, n, steps):
    ch = LANES
    m = nb * nc
    u = jnp.concatenate([u_ref[:, q].reshape(m, ch) for q in range(p)], axis=1)
    r = _dot(u, win_ref[0])
    c = lax.broadcasted_iota(jnp.int32, (m, 2 * n), 0) % nc

    def scan(x, d):
        for k in range(steps):
            s = 1 << k
            if d == 0:
                t = jnp.where(c >= s, pltpu.roll(x, s, axis=0), 0.0)
            else:
                t = jnp.where(c < nc - s, pltpu.roll(x, m - s, axis=0), 0.0)
            x = x + t * pq_ref[0, d, k, 0:1, :] + pltpu.roll(t, n, axis=1) * pq_ref[0, d, k, 1:2, :]
        if d == 0:
            return jnp.where(c >= 1, pltpu.roll(x, 1, axis=0), 0.0)
        return jnp.where(c < nc - 1, pltpu.roll(x, m - 1, axis=0), 0.0)

    xf = scan(r[:, :2 * n], 0)
    xb = scan(r[:, 2 * n:], 1)
    xc = jnp.concatenate([xf, xb], axis=1).astype(BF16)
    for p0 in range(0, p, 2):
        cols = slice(p0 * ch, (p0 + 2) * ch)
        for q in range(p):
            for pp in (p0, p0 + 1):
                row = jnp.broadcast_to(taps_ref[0, q, pp:pp + 1, :], (ch, 2 * ch))
                blk = pltpu.roll(row, ch + 1, 1, stride=1, stride_axis=0)[:, :ch]
                toep_ref[q * ch:(q + 1) * ch, pp * ch:(pp + 1) * ch] = blk.astype(BF16)
        y = _dot(u, toep_ref[:, cols]) + _dot(xc, wc_ref[0, :, cols])
        y_ref[:, p0] = y[:, :ch].reshape(nb, nc, ch)
        y_ref[:, p0 + 1] = y[:, ch:].reshape(nb, nc, ch)


def _s5(u4, taps, w_in, wc, pq):
    nb, su, nc, ch = u4.shape
    g_ = taps.shape[0]
    p = su // g_
    n = wc.shape[1] // 4
    steps = pq.shape[2]
    body = functools.partial(_s5_body, nb=nb, nc=nc, p=p, n=n, steps=steps)
    return pl.pallas_call(
        body,
        grid=(g_,),
        in_specs=[pl.BlockSpec((nb, p, nc, ch), lambda g: (0, g, 0, 0)),
                  pl.BlockSpec((1,) + taps.shape[1:], lambda g: (g, 0, 0, 0)),
                  pl.BlockSpec((1,) + w_in.shape[1:], lambda g: (g, 0, 0)),
                  pl.BlockSpec((1,) + wc.shape[1:], lambda g: (g, 0, 0)),
                  pl.BlockSpec((1,) + pq.shape[1:], lambda g: (g, 0, 0, 0, 0))],
        out_specs=pl.BlockSpec((nb, p, nc, ch), lambda g: (0, g, 0, 0)),
        out_shape=jax.ShapeDtypeStruct((nb, su, nc, ch), F32),
        scratch_shapes=[pltpu.VMEM((p * ch, p * ch), BF16)],
        compiler_params=_params("parallel"),
        name="s5",
    )(u4, taps, w_in, wc, pq)


def _gla_prep(q, k, v, g, tri, cmask, bdk, bd, ones, last_row):
    g_hi = g.astype(BF16)
    g_lo = (g - g_hi.astype(F32)).astype(BF16)
    cum = _dot(tri, g_hi) + _dot(tri, g_lo)
    last = cum[last_row:last_row + 1, :]
    q_dec = (q * jnp.exp(cum)).astype(BF16)
    k_inv = (k * jnp.exp(-cum)).astype(BF16)
    k_dec = (k * jnp.exp(last - cum)).astype(BF16)
    kb = jnp.where(bdk, jnp.concatenate([k_inv] * GLA_HEADS, axis=0), 0.0)
    s = jnp.where(cmask, _dot_nt(q_dec, kb), 0.0).astype(BF16)
    vb = jnp.where(bd, jnp.concatenate([v] * GLA_HEADS, axis=0), 0.0)
    o_intra = _dot(s, vb)
    tot = _dot_tn(g_hi, ones) + _dot_tn(g_lo, ones)
    kv = jnp.where(bd, _dot_tn(k_dec, v), 0.0)
    return q_dec, o_intra, kv, jnp.exp(tot)


def _gla_sweep(preps, order, st_ref, o_ref, c):
    st = st_ref[...]
    for n in order:
        q_dec, o_intra, kv, decay = preps[n]
        o_ref[n * c:(n + 1) * c, :] = o_intra + _dot(q_dec, st.astype(BF16))
        st = jnp.concatenate([decay] * (st.shape[1] // LANES), axis=1) * st + kv
    st_ref[...] = st


def _gla_body(qf_ref, kf_ref, vf_ref, gf_ref, qb_ref, kb_ref, vb_ref, gb_ref,
              of_ref, ob_ref, sf_ref, sb_ref, *, tb, dk, dv):
    c = GLA_CHUNK

    @pl.when(pl.program_id(1) == 0)
    def _():
        sf_ref[...] = jnp.zeros(sf_ref.shape, F32)
        sb_ref[...] = jnp.zeros(sb_ref.shape, F32)

    ii = lax.broadcasted_iota(jnp.int32, (c, c), 0)
    jj = lax.broadcasted_iota(jnp.int32, (c, c), 1)
    tri_f = (jj <= ii).astype(BF16)
    tri_b = (jj >= ii).astype(BF16)
    hc = GLA_HEADS * c
    i2 = lax.broadcasted_iota(jnp.int32, (c, hc), 0)
    j2 = lax.broadcasted_iota(jnp.int32, (c, hc), 1) % c
    cmask_f = j2 <= i2
    cmask_b = j2 > i2
    bdk = (lax.broadcasted_iota(jnp.int32, (hc, GLA_HEADS * dk), 0) // c
           == lax.broadcasted_iota(jnp.int32, (hc, GLA_HEADS * dk), 1) // dk)
    bd = (lax.broadcasted_iota(jnp.int32, (hc, GLA_HEADS * dv), 0) // c
          == lax.broadcasted_iota(jnp.int32, (hc, GLA_HEADS * dv), 1) // dv)
    ones = jnp.ones((c, LANES), BF16)
    nch = tb // c
    preps_f, preps_b = [], []
    for n in range(nch):
        r = slice(n * c, (n + 1) * c)
        preps_f.append(_gla_prep(qf_ref[r, :], kf_ref[r, :], vf_ref[r, :], gf_ref[r, :],
                                 tri_f, cmask_f, bdk, bd, ones, c - 1))
        preps_b.append(_gla_prep(qb_ref[r, :], kb_ref[r, :], vb_ref[r, :], gb_ref[r, :],
                                 tri_b, cmask_b, bdk, bd, ones, 0))
    _gla_sweep(preps_f, range(nch), sf_ref, of_ref, c)
    _gla_sweep(preps_b, range(nch - 1, -1, -1), sb_ref, ob_ref, c)


def _gla(q, k, v, gf, gb, *, b, l, tb):
    t, hk = q.shape
    hv = v.shape[1]
    dk, dv = hk // GLA_HEADS, hv // GLA_HEADS
    ni = l // tb
    fwd = lambda w: pl.BlockSpec((tb, w), lambda bi, i: (bi * ni + i, 0))
    bwd = lambda w: pl.BlockSpec((tb, w), lambda bi, i: (bi * ni + ni - 1 - i, 0))
    assert dk == GLA_CHUNK, "state mask reuse needs chunk == key head width"
    body = functools.partial(_gla_body, tb=tb, dk=dk, dv=dv)
    state = pltpu.VMEM((hk, hv), F32)
    return pl.pallas_call(
        body,
        grid=(b, ni),
        in_specs=[fwd(hk), fwd(hk), fwd(hv), fwd(hk), bwd(hk), bwd(hk), bwd(hv), bwd(hk)],
        out_specs=[fwd(hv), bwd(hv)],
        out_shape=[jax.ShapeDtypeStruct((t, hv), F32), jax.ShapeDtypeStruct((t, hv), F32)],
        scratch_shapes=[state, state],
        compiler_params=_params("parallel", "arbitrary"),
        name="gla",
    )(q, k, v, gf, q, k, v, gb)


def _head_norm(o, g, heads):
    dh = o.shape[1] // heads
    parts = []
    for h in range(heads):
        oh = o[:, h * dh:(h + 1) * dh]
        parts.append(oh * lax.rsqrt(jnp.mean(oh * oh, axis=-1, keepdims=True) + EPS))
    return jnp.concatenate(parts, axis=1) * g


def _ab_out_body(x_ref, yt_ref, wglut_ref, of_ref, ob_ref, og_ref, gn_ref, wo_ref, fg_ref, w1_ref, w2_ref,
                 o_ref, *, su):
    gy = _gelu_tanh(yt_ref[0])
    z = _dot(wglut_ref[...], gy.astype(BF16))
    s5_out = (gy * _sigmoid(z)).T.astype(BF16)
    o = _head_norm(of_ref[...] + ob_ref[...], gn_ref[...], GLA_HEADS)
    gla_out = (o * _silu(og_ref[...].astype(F32))).astype(BF16)
    x1 = x_ref[...] + _dot(s5_out, wo_ref[:su, :]) + _dot(gla_out, wo_ref[su:, :])
    o_ref[...] = _ffn_apply(x1, fg_ref[...], w1_ref, w2_ref)


def _ab_out(x2, yt, wglut, o_f, o_b, og, gn, wo, ffn, *, b, l, tm):
    t, d = x2.shape
    su = yt.shape[1]
    hv = o_f.shape[1]
    ni = l // tm
    fg, w1, w2 = ffn
    tok = lambda w: pl.BlockSpec((tm, w), lambda bi, i: (bi * ni + i, 0))
    body = functools.partial(_ab_out_body, su=su)
    return pl.pallas_call(
        body,
        grid=(b, ni),
        in_specs=[tok(d), pl.BlockSpec((1, su, tm), lambda bi, i: (bi, 0, i)), _const_spec(wglut.shape),
                  tok(hv), tok(hv), tok(hv), _const_spec((1, hv)), _const_spec(wo.shape),
                  _const_spec((1, d)), _const_spec(w1.shape), _const_spec(w2.shape)],
        out_specs=tok(d),
        out_shape=jax.ShapeDtypeStruct((t, d), F32),
        compiler_params=_params("parallel", "parallel"),
        name="ab_out_ffn",
    )(x2, yt, wglut, o_f, o_b, og, gn.reshape(1, hv), wo, fg.reshape(1, d), w1, w2)


def _ret_in_body(x_ref, g_ref, w_ref, wkt_ref, cos_ref, sin_ref, cost_ref, sint_ref,
                 q_ref, kt_ref, v_ref, og_ref, *, hk, hv, dk):
    h = _rmsnorm(x_ref[...], g_ref[...]).astype(BF16)
    cos2 = cos_ref[...]
    sin2 = sin_ref[...]
    q = _dot(h, w_ref[:, :hk])
    q_ref[...] = jnp.concatenate(
        [q[:, s:s + dk] * cos2 + pltpu.roll(q[:, s:s + dk], dk // 2, axis=1) * sin2 for s in range(0, hk, dk)],
        axis=1).astype(BF16)
    kt = _dot_nt(wkt_ref[...], h) * dk ** -0.5
    cost = cost_ref[...]
    sint = sint_ref[...]
    parts = []
    for s in range(0, hk, dk):
        t1 = kt[s:s + dk // 2]
        t2 = kt[s + dk // 2:s + dk]
        parts += [t1 * cost - t2 * sint, t1 * sint + t2 * cost]
    kt_ref[0] = jnp.concatenate(parts, axis=0).astype(BF16)
    v_ref[...] = _dot(h, w_ref[:, hk:hk + hv]).astype(BF16)
    og_ref[...] = _dot(h, w_ref[:, hk + hv:]).astype(BF16)


def _ret_in(x2, g, w, wkt, cos2, sin2, cost, sint, *, b, l, tm, hk, hv):
    t, d = x2.shape
    dk = hk // RET_HEADS
    ni = l // tm
    tok = lambda w_: pl.BlockSpec((tm, w_), lambda bi, i: (bi * ni + i, 0))
    pos = pl.BlockSpec((tm, dk), lambda bi, i: (i, 0))
    post = pl.BlockSpec((dk // 2, tm), lambda bi, i: (0, i))
    body = functools.partial(_ret_in_body, hk=hk, hv=hv, dk=dk)
    return pl.pallas_call(
        body,
        grid=(b, ni),
        in_specs=[tok(d), _const_spec((1, d)), _const_spec(w.shape), _const_spec(wkt.shape), pos, pos, post, post],
        out_specs=[tok(hk), pl.BlockSpec((1, hk, tm), lambda bi, i: (bi, 0, i)), tok(hv), tok(hv)],
        out_shape=[jax.ShapeDtypeStruct((t, hk), BF16), jax.ShapeDtypeStruct((b, hk, l), BF16),
                   jax.ShapeDtypeStruct((t, hv), BF16), jax.ShapeDtypeStruct((t, hv), BF16)],
        compiler_params=_params("parallel", "parallel"),
        name="ret_in",
    )(x2, g.reshape(1, d), w, wkt, cos2, sin2, cost, sint)


def _ret_tables(c, dk, dv):
    lg_f = jnp.log1p(-jnp.exp2(-5.0 - jnp.arange(RET_HEADS, dtype=F32)))
    lg_b = lg_f[::-1]
    idx = jnp.arange(c)
    diff = (idx[:, None] - idx[None, :]).astype(F32)
    dmat = jnp.where(diff >= 0, jnp.exp(jnp.maximum(diff, 0.0)[None] * lg_f[:, None, None]),
                     jnp.exp(jnp.maximum(-diff, 0.0)[None] * lg_b[:, None, None]))
    pos = idx.astype(F32)
    ones_v = jnp.ones((1, 1, dv), F32)
    xi_f = jnp.exp((pos + 1.0)[None, :] * lg_f[:, None])[..., None] * ones_v
    xi_b = jnp.exp((c - pos)[None, :] * lg_b[:, None])[..., None] * ones_v
    zeta_f = jnp.exp((c - 1.0 - pos)[None, :] * lg_f[:, None])[:, None, :]
    zeta_b = jnp.exp(pos[None, :] * lg_b[:, None])[:, None, :]
    cd_f = jnp.exp(c * lg_f)[:, None, None] * ones_v
    cd_b = jnp.exp(c * lg_b)[:, None, None] * ones_v
    return dmat, xi_f, xi_b, zeta_f, zeta_b, cd_f, cd_b


def _ret_state_body(kt_ref, v_ref, zb_ref, cdb_ref, sb_out_ref, sb_ref, *, dk, dv):
    @pl.when(pl.program_id(1) == 0)
    def _():
        sb_ref[...] = jnp.zeros(sb_ref.shape, F32)

    cb = zb_ref.shape[2]
    for n in range(v_ref.shape[0] // cb - 1, -1, -1):
        rows = slice(n * cb, (n + 1) * cb)
        for h in range(RET_HEADS):
            st = sb_ref[h]
            sb_out_ref[0, n, h] = st.astype(BF16)
            kz = (kt_ref[0, h * dk:(h + 1) * dk, rows].astype(F32) * zb_ref[h]).astype(BF16)
            sb_ref[h] = cdb_ref[h] * st + _dot(kz, v_ref[rows, h * dv:(h + 1) * dv])


def _ret_state(kt, v, zeta_b, cd_b, *, b, l, tb):
    hk, hv = kt.shape[1], v.shape[1]
    dk, dv = hk // RET_HEADS, hv // RET_HEADS
    ni = l // tb
    nsub = tb // zeta_b.shape[2]
    bwd = lambda w: pl.BlockSpec((tb, w), lambda bi, i: (bi * ni + ni - 1 - i, 0))
    body = functools.partial(_ret_state_body, dk=dk, dv=dv)
    return pl.pallas_call(
        body,
        grid=(b, ni),
        in_specs=[pl.BlockSpec((1, hk, tb), lambda bi, i: (bi, 0, ni - 1 - i)), bwd(hv),
                  _const_spec(zeta_b.shape), _const_spec(cd_b.shape)],
        out_specs=pl.BlockSpec((1, nsub, RET_HEADS, dk, dv), lambda bi, i: (bi, ni - 1 - i, 0, 0, 0)),
        out_shape=jax.ShapeDtypeStruct((b, ni * nsub, RET_HEADS, dk, dv), BF16),
        scratch_shapes=[pltpu.VMEM((RET_HEADS, dk, dv), F32)],
        compiler_params=_params("parallel", "arbitrary"),
        name="ret_state",
    )(kt, v, zeta_b, cd_b)


def _ret_main_body(x_ref, q_ref, kt_ref, v_ref, og_ref, sb_ref, dmat_ref, xif_ref, xib_ref, zf_ref, cdf_ref,
                   gn_ref, wo_ref, o_ref, sf_ref, gated_ref, *, dk, dv):
    @pl.when(pl.program_id(1) == 0)
    def _():
        sf_ref[...] = jnp.zeros(sf_ref.shape, F32)

    cb = dmat_ref.shape[1]
    for n in range(q_ref.shape[0] // cb):
        rows = slice(n * cb, (n + 1) * cb)
        for h in range(RET_HEADS):
            ks = slice(h * dk, (h + 1) * dk)
            vs = slice(h * dv, (h + 1) * dv)
            q = q_ref[rows, ks]
            kt = kt_ref[0, ks, rows]
            v = v_ref[rows, vs]
            s = _dot(q, kt) * dmat_ref[h]
            st = sf_ref[h]
            o = (_dot(s.astype(BF16), v) + _dot(q, st.astype(BF16)) * xif_ref[h]
                 + _dot(q, sb_ref[0, n, h]) * xib_ref[h])
            kz = (kt.astype(F32) * zf_ref[h]).astype(BF16)
            sf_ref[h] = cdf_ref[h] * st + _dot(kz, v)
            o = o * lax.rsqrt(jnp.mean(o * o, axis=-1, keepdims=True) + EPS) * gn_ref[:, vs]
            gated_ref[rows, vs] = (o * _silu(og_ref[rows, vs].astype(F32))).astype(BF16)
    o_ref[...] = x_ref[...] + _dot(gated_ref[...], wo_ref[...])


def _ret_main(x2, q, kt, v, og, sb, tabs, gn, wo, *, b, l, tb):
    t, d = x2.shape
    hk, hv = q.shape[1], v.shape[1]
    dk, dv = hk // RET_HEADS, hv // RET_HEADS
    ni = l // tb
    nsub = tb // tabs[0].shape[1]
    tok = lambda w: pl.BlockSpec((tb, w), lambda bi, i: (bi * ni + i, 0))
    body = functools.partial(_ret_main_body, dk=dk, dv=dv)
    return pl.pallas_call(
        body,
        grid=(b, ni),
        in_specs=[tok(d), tok(hk), pl.BlockSpec((1, hk, tb), lambda bi, i: (bi, 0, i)), tok(hv), tok(hv),
                  pl.BlockSpec((1, nsub, RET_HEADS, dk, dv), lambda bi, i: (bi, i, 0, 0, 0))]
                 + [_const_spec(a.shape) for a in tabs] + [_const_spec((1, hv)), _const_spec(wo.shape)],
        out_specs=tok(d),
        out_shape=jax.ShapeDtypeStruct((t, d), F32),
        scratch_shapes=[pltpu.VMEM((RET_HEADS, dk, dv), F32), pltpu.VMEM((tb, hv), BF16)],
        compiler_params=_params("parallel", "arbitrary"),
        name="ret_main",
    )(x2, q, kt, v, og, sb, *tabs, gn.reshape(1, hv), wo)


def _tile(l, want):
    t = min(l, want)
    while l % t:
        t //= 2
    return t


def _s5_gla_layer(x2, mix_g, w_in, lam_re, lam_im, b_re, b_im, c_re, c_im, log_dt, d_skip, w_glu,
                  w_gk, b_gk, gla_norm, w_out, ffn2, *, b, l):
    su = d_skip.shape[0]
    hk = w_gk.shape[-1]
    r = w_gk.shape[1]
    hv = gla_norm.shape[0]
    tm = _tile(l, 1024)
    wut = w_in[:, :su].T.astype(BF16)
    wr = w_in[:, su:].astype(BF16)
    wgk = jnp.zeros((2 * r, 2 * hk), F32)
    wgk = wgk.at[:r, :hk].set(w_gk[0]).at[r:, hk:].set(w_gk[1]).astype(BF16)
    bgk = b_gk.reshape(1, 2 * hk).astype(F32)
    ut, q, k, v, og, gf, gb = _ab_in(x2, mix_g, wut, wr, wgk, bgk, b=b, l=l, tm=tm, hk=hk, hv=hv)

    nc = l // LANES
    taps, w_st, wc, pq = _s5_tables(lam_re, lam_im, b_re, b_im, c_re, c_im, log_dt, d_skip, nc)
    y4 = _s5(ut.reshape(b, su, nc, LANES), taps, w_st, wc, pq)
    yt = y4.reshape(b, su, l)

    o_f, o_b = _gla(q, k, v, gf, gb, b=b, l=l, tb=_tile(l, 1024))
    return _ab_out(x2, yt, w_glu.T.astype(BF16), o_f, o_b, og, gla_norm, w_out.astype(BF16), ffn2,
                   b=b, l=l, tm=_tile(l, 512))


def _retention_layer(x2, mix_g, w_in, ret_norm, w_out, *, b, l):
    hv = ret_norm.shape[0]
    hk = (w_in.shape[1] - 2 * hv) // 2
    dk = hk // RET_HEADS
    half = dk // 2
    tm = _tile(l, 512)
    pos = jnp.arange(l, dtype=F32)
    inv = jnp.exp(-math.log(ROPE_BASE) * jnp.arange(half, dtype=F32) / half)
    ang = pos[:, None] * inv[None, :]
    cos, sin = jnp.cos(ang), jnp.sin(ang)
    cos2 = jnp.concatenate([cos, cos], axis=1)
    sin2 = jnp.concatenate([-sin, sin], axis=1)
    w_qvg = jnp.concatenate([w_in[:, :hk], w_in[:, 2 * hk:]], axis=1).astype(BF16)
    w_kt = w_in[:, hk:2 * hk].T.astype(BF16)
    q, kt, v, og = _ret_in(x2, mix_g, w_qvg, w_kt, cos2, sin2, cos.T, sin.T, b=b, l=l, tm=tm, hk=hk, hv=hv)
    tb = _tile(l, 256)
    dmat, xi_f, xi_b, zeta_f, zeta_b, cd_f, cd_b = _ret_tables(tb, dk, hv // RET_HEADS)
    sb = _ret_state(kt, v, zeta_b, cd_b, b=b, l=l, tb=_tile(l, 4 * tb))
    return _ret_main(x2, q, kt, v, og, sb, (dmat, xi_f, xi_b, zeta_f, cd_f), ret_norm, w_out.astype(BF16),
                     b=b, l=l, tb=_tile(l, 2 * tb))


def kernel(x, ffn1_norm, ffn1_w1, ffn1_w2, mix_norm, ffn2_norm, ffn2_w1, ffn2_w2, ab_w_in, s5_lambda_re, s5_lambda_im, s5_b_re, s5_b_im, s5_c_re, s5_c_im, s5_log_dt, s5_d, s5_w_glu, gla_w_gk, gla_b_gk, gla_norm, ab_w_out, ret_w_in, ret_norm, ret_w_out, final_norm):
    b, l, d = x.shape
    depth = ffn1_norm.shape[0]
    assert depth % 2 == 0, "the final norm is applied by the retention layer's second FFN"
    tm = _tile(b * l, 1024)
    x2 = x.reshape(b * l, d).astype(F32)
    for i in range(depth):
        j = i // 2
        ffn1 = (ffn1_norm[i], ffn1_w1[i].astype(BF16), ffn1_w2[i].astype(BF16))
        ffn2 = (ffn2_norm[i], ffn2_w1[i].astype(BF16), ffn2_w2[i].astype(BF16))
        x2 = _ffn(x2, *ffn1, final_norm, final=False, tm=tm)
        if i % 2 == 0:
            x2 = _s5_gla_layer(x2, mix_norm[i], ab_w_in[j], s5_lambda_re[j], s5_lambda_im[j], s5_b_re[j],
                               s5_b_im[j], s5_c_re[j], s5_c_im[j], s5_log_dt[j], s5_d[j], s5_w_glu[j],
                               gla_w_gk[j], gla_b_gk[j], gla_norm[j], ab_w_out[j], ffn2, b=b, l=l)
        else:
            x2 = _retention_layer(x2, mix_norm[i], ret_w_in[j], ret_norm[j], ret_w_out[j], b=b, l=l)
            x2 = _ffn(x2, *ffn2, final_norm, final=(i == depth - 1), tm=tm)
    return x2.reshape(b, l, d)
```

```python
import functools
import math

import jax
import jax.numpy as jnp
from jax import lax
from jax.experimental import pallas as pl
from jax.experimental.pallas import tpu as pltpu

F32 = jnp.float32
BF16 = jnp.bfloat16

EPS = 1e-6
GLA_HEADS = 4
GLA_GATE_NORM = 16.0
GLA_CHUNK = 64
RET_HEADS = 8
ROPE_BASE = 10000.0

LANES = 128
VMEM_LIMIT = 56 * 1024 * 1024
HI = lax.Precision.HIGHEST

TM_FFN = 1024
TM_AB_IN = 1024
TM_AB_OUT = 512
TM_RET_IN = 512
TB_GLA = 1024
RET_BLOCK = 256
RET_STATE_BLOCKS = 4
RET_MAIN_BLOCKS = 2


def _const_spec(shape):
    nd = len(shape)
    return pl.BlockSpec(shape, lambda *_: (0,) * nd, pipeline_mode=pl.Buffered(1))


def _params(*sem):
    return pltpu.CompilerParams(dimension_semantics=sem, vmem_limit_bytes=VMEM_LIMIT)


def _rmsnorm(x, g):
    return x * lax.rsqrt(jnp.mean(x * x, axis=-1, keepdims=True) + EPS) * g


def _sigmoid(x):
    return 1.0 / (1.0 + jnp.exp(-x))


def _silu(x):
    return x * _sigmoid(x)


def _gelu_tanh(x):
    return 0.5 * x * (1.0 + jnp.tanh(math.sqrt(2.0 / math.pi) * (x + 0.044715 * (x * x * x))))


def _log_sigmoid(x):
    return jnp.minimum(x, 0.0) - jnp.log1p(jnp.exp(-jnp.abs(x)))


def _dot(a, b):
    return jnp.dot(a, b, preferred_element_type=F32)


def _dot_nt(a, b):
    return lax.dot_general(a, b, (((1,), (1,)), ((), ())), preferred_element_type=F32)


def _dot_tn(a, b):
    return lax.dot_general(a, b, (((0,), (0,)), ((), ())), preferred_element_type=F32)


def _ffn_apply(x, g, w1_ref, w2_ref):
    d_ff = w2_ref.shape[0]
    f_chunk = 256 if d_ff % 256 == 0 else d_ff
    xn = _rmsnorm(x, g).astype(BF16)
    acc = jnp.zeros(x.shape, F32)
    for f0 in range(0, d_ff, f_chunk):
        gate = _dot(xn, w1_ref[:, f0:f0 + f_chunk])
        up = _dot(xn, w1_ref[:, d_ff + f0:d_ff + f0 + f_chunk])
        act = (_silu(gate) * up).astype(BF16)
        acc = acc + _dot(act, w2_ref[f0:f0 + f_chunk, :])
    return x + 0.5 * acc


def _ffn_body(x_ref, g_ref, w1_ref, w2_ref, fg_ref, o_ref, *, final):
    y = _ffn_apply(x_ref[...], g_ref[...], w1_ref, w2_ref)
    if final:
        y = _rmsnorm(y, fg_ref[...])
    o_ref[...] = y


def _ffn(x2, g, w1, w2, fg, *, final, tm):
    t, d = x2.shape
    body = functools.partial(_ffn_body, final=final)
    return pl.pallas_call(
        body,
        grid=(t // tm,),
        in_specs=[pl.BlockSpec((tm, d), lambda i: (i, 0)),
                  _const_spec((1, d)), _const_spec(w1.shape), _const_spec(w2.shape),
                  _const_spec((1, d))],
        out_specs=pl.BlockSpec((tm, d), lambda i: (i, 0)),
        out_shape=jax.ShapeDtypeStruct((t, d), F32),
        compiler_params=_params("parallel"),
        name="ffn",
    )(x2, g.reshape(1, d), w1, w2, fg.reshape(1, d))


def _ab_in_body(x_ref, g_ref, wut_ref, wr_ref, wgk_ref, bgk_ref,
                ut_ref, q_ref, k_ref, v_ref, og_ref, gf_ref, gb_ref, *, hk, hv):
    h = _rmsnorm(x_ref[...], g_ref[...]).astype(BF16)
    ut_ref[0] = _dot_nt(wut_ref[...], h).astype(BF16)
    proj = _dot(h, wr_ref[...])
    q_ref[...] = proj[:, :hk] * (hk // GLA_HEADS) ** -0.5
    k_ref[...] = proj[:, hk:2 * hk]
    v_ref[...] = proj[:, 2 * hk:2 * hk + hv].astype(BF16)
    og_ref[...] = proj[:, 2 * hk + hv:2 * hk + 2 * hv].astype(BF16)
    glo = proj[:, 2 * hk + 2 * hv:].astype(BF16)
    gk = _log_sigmoid(_dot(glo, wgk_ref[...]) + bgk_ref[...]) * (1.0 / GLA_GATE_NORM)
    gf_ref[...] = gk[:, :hk]
    gb_ref[...] = gk[:, hk:]


def _ab_in(x2, g, wut, wr, wgk, bgk, *, b, l, tm, hk, hv):
    t, d = x2.shape
    su = wut.shape[0]
    ni = l // tm
    tok = lambda w: pl.BlockSpec((tm, w), lambda bi, i: (bi * ni + i, 0))
    body = functools.partial(_ab_in_body, hk=hk, hv=hv)
    return pl.pallas_call(
        body,
        grid=(b, ni),
        in_specs=[tok(d), _const_spec((1, d)), _const_spec(wut.shape), _const_spec(wr.shape),
                  _const_spec(wgk.shape), _const_spec(bgk.shape)],
        out_specs=[pl.BlockSpec((1, su, tm), lambda bi, i: (bi, 0, i)),
                   tok(hk), tok(hk), tok(hv), tok(hv), tok(hk), tok(hk)],
        out_shape=[jax.ShapeDtypeStruct((b, su, l), BF16),
                   jax.ShapeDtypeStruct((t, hk), F32), jax.ShapeDtypeStruct((t, hk), F32),
                   jax.ShapeDtypeStruct((t, hv), BF16), jax.ShapeDtypeStruct((t, hv), BF16),
                   jax.ShapeDtypeStruct((t, hk), F32), jax.ShapeDtypeStruct((t, hk), F32)],
        compiler_params=_params("parallel", "parallel"),
        name="ab_in",
    )(x2, g.reshape(1, d), wut, wr, wgk, bgk)


def _cpow_table(ar, ai, count):
    ks = jnp.arange(count)
    pr = jnp.ones(ar.shape + (count,), F32)
    pi = jnp.zeros(ar.shape + (count,), F32)
    sr, si = ar, ai
    for bit in range(max(1, (count - 1).bit_length())):
        on = ((ks >> bit) & 1) == 1
        nr = pr * sr[..., None] - pi * si[..., None]
        ni = pr * si[..., None] + pi * sr[..., None]
        pr = jnp.where(on, nr, pr)
        pi = jnp.where(on, ni, pi)
        sr, si = sr * sr - si * si, 2.0 * sr * si
    return pr, pi


def _s5_tables(lam_re, lam_im, b_re, b_im, c_re, c_im, log_dt, d_skip, n_chunks):
    ch = LANES
    g_, n_ = lam_re.shape[1], lam_re.shape[2]
    p_ = b_re.shape[-1]
    f32 = lambda a: a.astype(F32)
    lr = jnp.minimum(f32(lam_re), -1e-4)
    li = f32(lam_im)
    dt = jnp.exp(f32(log_dt))[..., None]
    mag = jnp.exp(lr * dt)
    ar = mag * jnp.cos(li * dt)
    ai = mag * jnp.sin(li * dt)
    den = lr * lr + li * li
    cr = ((ar - 1.0) * lr + ai * li) / den
    ci = (ai * lr - (ar - 1.0) * li) / den
    br, bi = f32(b_re), f32(b_im)
    bbr = cr[..., None] * br - ci[..., None] * bi
    bbi = cr[..., None] * bi + ci[..., None] * br
    ccr, cci = f32(c_re), f32(c_im)
    pwr, pwi = _cpow_table(ar, ai, ch + 1)

    cnr = ccr.transpose(0, 1, 3, 2)[..., None]
    cni = cci.transpose(0, 1, 3, 2)[..., None]
    cbr = cnr * bbr[:, :, :, None, :] - cni * bbi[:, :, :, None, :]
    cbi = cnr * bbi[:, :, :, None, :] + cni * bbr[:, :, :, None, :]
    taps = (jnp.einsum('dgnpq,dgnl->dgqpl', cbr, pwr[..., :ch], precision=HI)
            - jnp.einsum('dgnpq,dgnl->dgqpl', cbi, pwi[..., :ch], precision=HI))
    kf, kb = taps[0], taps[1]
    dmat = f32(d_skip).reshape(g_, p_)[:, None, :, None] * jnp.eye(p_, dtype=F32)[None, :, :, None]
    center = kf[..., 0:1] + kb[..., 0:1] + dmat
    kfull = jnp.concatenate([kb[..., :0:-1], center, kf[..., 1:]], axis=-1)
    kpad = jnp.pad(kfull, ((0, 0),) * 3 + ((0, 1),))

    def win(d, rev):
        pr = pwr[d][..., :ch]
        pi = pwi[d][..., :ch]
        if rev:
            pr, pi = pr[..., ::-1], pi[..., ::-1]
        pr = pr.transpose(0, 2, 1)[:, None]
        pi = pi.transpose(0, 2, 1)[:, None]
        xr = bbr[d].transpose(0, 2, 1)[:, :, None, :]
        xi = bbi[d].transpose(0, 2, 1)[:, :, None, :]
        return pr * xr - pi * xi, pr * xi + pi * xr
    fre, fim = win(0, True)
    bre, bim = win(1, False)
    w_in = jnp.concatenate([fre, fim, bre, bim], axis=-1).reshape(g_, p_ * ch, 4 * n_).astype(BF16)

    def wout(d, rev):
        pr = pwr[d][..., 1:ch + 1]
        pi = pwi[d][..., 1:ch + 1]
        if rev:
            pr, pi = pr[..., ::-1], pi[..., ::-1]
        pr = pr[:, :, None, :]
        pi = pi[:, :, None, :]
        xr = ccr[d].transpose(0, 2, 1)[..., None]
        xi = cci[d].transpose(0, 2, 1)[..., None]
        return xr * pr - xi * pi, -(xr * pi + xi * pr)
    fre, fim = wout(0, False)
    bre, bim = wout(1, True)
    wc = jnp.concatenate([fre, fim, bre, bim], axis=1).reshape(g_, 4 * n_, p_ * ch).astype(BF16)

    steps = max(1, (n_chunks - 1).bit_length())
    sr, si = pwr[..., ch], pwi[..., ch]
    rows = []
    for _ in range(steps):
        rows.append(jnp.stack([jnp.concatenate([sr, sr], -1), jnp.concatenate([-si, si], -1)], axis=2))
        sr, si = sr * sr - si * si, 2.0 * sr * si
    pq = jnp.stack(rows, axis=2)
    pq = pq.transpose(1, 0, 2, 3, 4)
    return kpad, w_in, wc, pq


def _s5_body(u_ref, taps_ref, win_ref, wc_ref, pq_ref, y_ref, toep_ref, *, nb, nc, p, n, steps):
    ch = LANES
    m = nb * nc
    u = jnp.concatenate([u_ref[:, q].reshape(m, ch) for q in range(p)], axis=1)
    r = _dot(u, win_ref[0])
    c = lax.broadcasted_iota(jnp.int32, (m, 2 * n), 0) % nc

    def scan(x, d):
        for k in range(steps):
            s = 1 << k
            if d == 0:
                t = jnp.where(c >= s, pltpu.roll(x, s, axis=0), 0.0)
            else:
                t = jnp.where(c < nc - s, pltpu.roll(x, m - s, axis=0), 0.0)
            x = x + t * pq_ref[0, d, k, 0:1, :] + pltpu.roll(t, n, axis=1) * pq_ref[0, d, k, 1:2, :]
        if d == 0:
            return jnp.where(c >= 1, pltpu.roll(x, 1, axis=0), 0.0)
        return jnp.where(c < nc - 1, pltpu.roll(x, m - 1, axis=0), 0.0)

    xf = scan(r[:, :2 * n], 0)
    xb = scan(r[:, 2 * n:], 1)
    xc = jnp.concatenate([xf, xb], axis=1).astype(BF16)
    for p0 in range(0, p, 2):
        cols = slice(p0 * ch, (p0 + 2) * ch)
        for q in range(p):
            for pp in (p0, p0 + 1):
                row = jnp.broadcast_to(taps_ref[0, q, pp:pp + 1, :], (ch, 2 * ch))
                blk = pltpu.roll(row, ch + 1, 1, stride=1, stride_axis=0)[:, :ch]
                toep_ref[q * ch:(q + 1) * ch, pp * ch:(pp + 1) * ch] = blk.astype(BF16)
        y = _dot(u, toep_ref[:, cols]) + _dot(xc, wc_ref[0, :, cols])
        y_ref[:, p0] = y[:, :ch].reshape(nb, nc, ch)
        y_ref[:, p0 + 1] = y[:, ch:].reshape(nb, nc, ch)


def _s5(u4, taps, w_in, wc, pq):
    nb, su, nc, ch = u4.shape
    g_ = taps.shape[0]
    p = su // g_
    n = wc.shape[1] // 4
    steps = pq.shape[2]
    body = functools.partial(_s5_body, nb=nb, nc=nc, p=p, n=n, steps=steps)
    return pl.pallas_call(
        body,
        grid=(g_,),
        in_specs=[pl.BlockSpec((nb, p, nc, ch), lambda g: (0, g, 0, 0)),
                  pl.BlockSpec((1,) + taps.shape[1:], lambda g: (g, 0, 0, 0)),
                  pl.BlockSpec((1,) + w_in.shape[1:], lambda g: (g, 0, 0)),
                  pl.BlockSpec((1,) + wc.shape[1:], lambda g: (g, 0, 0)),
                  pl.BlockSpec((1,) + pq.shape[1:], lambda g: (g, 0, 0, 0, 0))],
        out_specs=pl.BlockSpec((nb, p, nc, ch), lambda g: (0, g, 0, 0)),
        out_shape=jax.ShapeDtypeStruct((nb, su, nc, ch), F32),
        scratch_shapes=[pltpu.VMEM((p * ch, p * ch), BF16)],
        compiler_params=_params("parallel"),
        name="s5",
    )(u4, taps, w_in, wc, pq)


def _gla_prep(q, k, v, g, tri, cmask, bdk, bd, ones, last_row):
    g_hi = g.astype(BF16)
    g_lo = (g - g_hi.astype(F32)).astype(BF16)
    cum = _dot(tri, g_hi) + _dot(tri, g_lo)
    last = cum[last_row:last_row + 1, :]
    q_dec = (q * jnp.exp(cum)).astype(BF16)
    k_inv = (k * jnp.exp(-cum)).astype(BF16)
    k_dec = (k * jnp.exp(last - cum)).astype(BF16)
    kb = jnp.where(bdk, jnp.concatenate([k_inv] * GLA_HEADS, axis=0), 0.0)
    s = jnp.where(cmask, _dot_nt(q_dec, kb), 0.0).astype(BF16)
    vb = jnp.where(bd, jnp.concatenate([v] * GLA_HEADS, axis=0), 0.0)
    o_intra = _dot(s, vb)
    tot = _dot_tn(g_hi, ones) + _dot_tn(g_lo, ones)
    kv = jnp.where(bd, _dot_tn(k_dec, v), 0.0)
    return q_dec, o_intra, kv, jnp.exp(tot)


def _gla_sweep(preps, order, st_ref, o_ref, c):
    st = st_ref[...]
    for n in order:
        q_dec, o_intra, kv, decay = preps[n]
        o_ref[n * c:(n + 1) * c, :] = o_intra + _dot(q_dec, st.astype(BF16))
        st = jnp.concatenate([decay] * (st.shape[1] // LANES), axis=1) * st + kv
    st_ref[...] = st


def _gla_body(qf_ref, kf_ref, vf_ref, gf_ref, qb_ref, kb_ref, vb_ref, gb_ref,
              of_ref, ob_ref, sf_ref, sb_ref, *, tb, dk, dv):
    c = GLA_CHUNK

    @pl.when(pl.program_id(1) == 0)
    def _():
        sf_ref[...] = jnp.zeros(sf_ref.shape, F32)
        sb_ref[...] = jnp.zeros(sb_ref.shape, F32)

    ii = lax.broadcasted_iota(jnp.int32, (c, c), 0)
    jj = lax.broadcasted_iota(jnp.int32, (c, c), 1)
    tri_f = (jj <= ii).astype(BF16)
    tri_b = (jj >= ii).astype(BF16)
    hc = GLA_HEADS * c
    i2 = lax.broadcasted_iota(jnp.int32, (c, hc), 0)
    j2 = lax.broadcasted_iota(jnp.int32, (c, hc), 1) % c
    cmask_f = j2 <= i2
    cmask_b = j2 > i2
    bdk = (lax.broadcasted_iota(jnp.int32, (hc, GLA_HEADS * dk), 0) // c
           == lax.broadcasted_iota(jnp.int32, (hc, GLA_HEADS * dk), 1) // dk)
    bd = (lax.broadcasted_iota(jnp.int32, (hc, GLA_HEADS * dv), 0) // c
          == lax.broadcasted_iota(jnp.int32, (hc, GLA_HEADS * dv), 1) // dv)
    ones = jnp.ones((c, LANES), BF16)
    nch = tb // c
    preps_f, preps_b = [], []
    for n in range(nch):
        r = slice(n * c, (n + 1) * c)
        preps_f.append(_gla_prep(qf_ref[r, :], kf_ref[r, :], vf_ref[r, :], gf_ref[r, :],
                                 tri_f, cmask_f, bdk, bd, ones, c - 1))
        preps_b.append(_gla_prep(qb_ref[r, :], kb_ref[r, :], vb_ref[r, :], gb_ref[r, :],
                                 tri_b, cmask_b, bdk, bd, ones, 0))
    _gla_sweep(preps_f, range(nch), sf_ref, of_ref, c)
    _gla_sweep(preps_b, range(nch - 1, -1, -1), sb_ref, ob_ref, c)


def _gla(q, k, v, gf, gb, *, b, l, tb):
    t, hk = q.shape
    hv = v.shape[1]
    dk, dv = hk // GLA_HEADS, hv // GLA_HEADS
    ni = l // tb
    fwd = lambda w: pl.BlockSpec((tb, w), lambda bi, i: (bi * ni + i, 0))
    bwd = lambda w: pl.BlockSpec((tb, w), lambda bi, i: (bi * ni + ni - 1 - i, 0))
    assert dk == GLA_CHUNK, "state mask reuse needs chunk == key head width"
    body = functools.partial(_gla_body, tb=tb, dk=dk, dv=dv)
    state = pltpu.VMEM((hk, hv), F32)
    return pl.pallas_call(
        body,
        grid=(b, ni),
        in_specs=[fwd(hk), fwd(hk), fwd(hv), fwd(hk), bwd(hk), bwd(hk), bwd(hv), bwd(hk)],
        out_specs=[fwd(hv), bwd(hv)],
        out_shape=[jax.ShapeDtypeStruct((t, hv), F32), jax.ShapeDtypeStruct((t, hv), F32)],
        scratch_shapes=[state, state],
        compiler_params=_params("parallel", "arbitrary"),
        name="gla",
    )(q, k, v, gf, q, k, v, gb)


def _head_norm(o, g, heads):
    dh = o.shape[1] // heads
    parts = []
    for h in range(heads):
        oh = o[:, h * dh:(h + 1) * dh]
        parts.append(oh * lax.rsqrt(jnp.mean(oh * oh, axis=-1, keepdims=True) + EPS))
    return jnp.concatenate(parts, axis=1) * g


def _ab_out_body(x_ref, yt_ref, wglut_ref, of_ref, ob_ref, og_ref, gn_ref, wo_ref, fg_ref, w1_ref, w2_ref,
                 o_ref, *, su):
    gy = _gelu_tanh(yt_ref[0])
    z = _dot(wglut_ref[...], gy.astype(BF16))
    s5_out = (gy * _sigmoid(z)).T.astype(BF16)
    o = _head_norm(of_ref[...] + ob_ref[...], gn_ref[...], GLA_HEADS)
    gla_out = (o * _silu(og_ref[...].astype(F32))).astype(BF16)
    x1 = x_ref[...] + _dot(s5_out, wo_ref[:su, :]) + _dot(gla_out, wo_ref[su:, :])
    o_ref[...] = _ffn_apply(x1, fg_ref[...], w1_ref, w2_ref)


def _ab_out(x2, yt, wglut, o_f, o_b, og, gn, wo, ffn, *, b, l, tm):
    t, d = x2.shape
    su = yt.shape[1]
    hv = o_f.shape[1]
    ni = l // tm
    fg, w1, w2 = ffn
    tok = lambda w: pl.BlockSpec((tm, w), lambda bi, i: (bi * ni + i, 0))
    body = functools.partial(_ab_out_body, su=su)
    return pl.pallas_call(
        body,
        grid=(b, ni),
        in_specs=[tok(d), pl.BlockSpec((1, su, tm), lambda bi, i: (bi, 0, i)), _const_spec(wglut.shape),
                  tok(hv), tok(hv), tok(hv), _const_spec((1, hv)), _const_spec(wo.shape),
                  _const_spec((1, d)), _const_spec(w1.shape), _const_spec(w2.shape)],
        out_specs=tok(d),
        out_shape=jax.ShapeDtypeStruct((t, d), F32),
        compiler_params=_params("parallel", "parallel"),
        name="ab_out_ffn",
    )(x2, yt, wglut, o_f, o_b, og, gn.reshape(1, hv), wo, fg.reshape(1, d), w1, w2)


def _ret_in_body(x_ref, g_ref, w_ref, wkt_ref, cos_ref, sin_ref, cost_ref, sint_ref,
                 q_ref, kt_ref, v_ref, og_ref, *, hk, hv, dk):
    h = _rmsnorm(x_ref[...], g_ref[...]).astype(BF16)
    cos2 = cos_ref[...]
    sin2 = sin_ref[...]
    q = _dot(h, w_ref[:, :hk])
    q_ref[...] = jnp.concatenate(
        [q[:, s:s + dk] * cos2 + pltpu.roll(q[:, s:s + dk], dk // 2, axis=1) * sin2 for s in range(0, hk, dk)],
        axis=1).astype(BF16)
    kt = _dot_nt(wkt_ref[...], h) * dk ** -0.5
    cost = cost_ref[...]
    sint = sint_ref[...]
    parts = []
    for s in range(0, hk, dk):
        t1 = kt[s:s + dk // 2]
        t2 = kt[s + dk // 2:s + dk]
        parts += [t1 * cost - t2 * sint, t1 * sint + t2 * cost]
    kt_ref[0] = jnp.concatenate(parts, axis=0).astype(BF16)
    v_ref[...] = _dot(h, w_ref[:, hk:hk + hv]).astype(BF16)
    og_ref[...] = _dot(h, w_ref[:, hk + hv:]).astype(BF16)


def _ret_in(x2, g, w, wkt, cos2, sin2, cost, sint, *, b, l, tm, hk, hv):
    t, d = x2.shape
    dk = hk // RET_HEADS
    ni = l // tm
    tok = lambda w_: pl.BlockSpec((tm, w_), lambda bi, i: (bi * ni + i, 0))
    pos = pl.BlockSpec((tm, dk), lambda bi, i: (i, 0))
    post = pl.BlockSpec((dk // 2, tm), lambda bi, i: (0, i))
    body = functools.partial(_ret_in_body, hk=hk, hv=hv, dk=dk)
    return pl.pallas_call(
        body,
        grid=(b, ni),
        in_specs=[tok(d), _const_spec((1, d)), _const_spec(w.shape), _const_spec(wkt.shape), pos, pos, post, post],
        out_specs=[tok(hk), pl.BlockSpec((1, hk, tm), lambda bi, i: (bi, 0, i)), tok(hv), tok(hv)],
        out_shape=[jax.ShapeDtypeStruct((t, hk), BF16), jax.ShapeDtypeStruct((b, hk, l), BF16),
                   jax.ShapeDtypeStruct((t, hv), BF16), jax.ShapeDtypeStruct((t, hv), BF16)],
        compiler_params=_params("parallel", "parallel"),
        name="ret_in",
    )(x2, g.reshape(1, d), w, wkt, cos2, sin2, cost, sint)


def _ret_tables(c, dk, dv):
    lg_f = jnp.log1p(-jnp.exp2(-5.0 - jnp.arange(RET_HEADS, dtype=F32)))
    lg_b = lg_f[::-1]
    idx = jnp.arange(c)
    diff = (idx[:, None] - idx[None, :]).astype(F32)
    dmat = jnp.where(diff >= 0, jnp.exp(jnp.maximum(diff, 0.0)[None] * lg_f[:, None, None]),
                     jnp.exp(jnp.maximum(-diff, 0.0)[None] * lg_b[:, None, None]))
    pos = idx.astype(F32)
    ones_v = jnp.ones((1, 1, dv), F32)
    xi_f = jnp.exp((pos + 1.0)[None, :] * lg_f[:, None])[..., None] * ones_v
    xi_b = jnp.exp((c - pos)[None, :] * lg_b[:, None])[..., None] * ones_v
    zeta_f = jnp.exp((c - 1.0 - pos)[None, :] * lg_f[:, None])[:, None, :]
    zeta_b = jnp.exp(pos[None, :] * lg_b[:, None])[:, None, :]
    cd_f = jnp.exp(c * lg_f)[:, None, None] * ones_v
    cd_b = jnp.exp(c * lg_b)[:, None, None] * ones_v
    return dmat, xi_f, xi_b, zeta_f, zeta_b, cd_f, cd_b


def _ret_state_body(kt_ref, v_ref, zb_ref, cdb_ref, sb_out_ref, sb_ref, *, dk, dv):
    @pl.when(pl.program_id(1) == 0)
    def _():
        sb_ref[...] = jnp.zeros(sb_ref.shape, F32)

    cb = zb_ref.shape[2]
    for n in range(v_ref.shape[0] // cb - 1, -1, -1):
        rows = slice(n * cb, (n + 1) * cb)
        for h in range(RET_HEADS):
            st = sb_ref[h]
            sb_out_ref[0, n, h] = st.astype(BF16)
            kz = (kt_ref[0, h * dk:(h + 1) * dk, rows].astype(F32) * zb_ref[h]).astype(BF16)
            sb_ref[h] = cdb_ref[h] * st + _dot(kz, v_ref[rows, h * dv:(h + 1) * dv])


def _ret_state(kt, v, zeta_b, cd_b, *, b, l, tb):
    hk, hv = kt.shape[1], v.shape[1]
    dk, dv = hk // RET_HEADS, hv // RET_HEADS
    ni = l // tb
    nsub = tb // zeta_b.shape[2]
    bwd = lambda w: pl.BlockSpec((tb, w), lambda bi, i: (bi * ni + ni - 1 - i, 0))
    body = functools.partial(_ret_state_body, dk=dk, dv=dv)
    return pl.pallas_call(
        body,
        grid=(b, ni),
        in_specs=[pl.BlockSpec((1, hk, tb), lambda bi, i: (bi, 0, ni - 1 - i)), bwd(hv),
                  _const_spec(zeta_b.shape), _const_spec(cd_b.shape)],
        out_specs=pl.BlockSpec((1, nsub, RET_HEADS, dk, dv), lambda bi, i: (bi, ni - 1 - i, 0, 0, 0)),
        out_shape=jax.ShapeDtypeStruct((b, ni * nsub, RET_HEADS, dk, dv), BF16),
        scratch_shapes=[pltpu.VMEM((RET_HEADS, dk, dv), F32)],
        compiler_params=_params("parallel", "arbitrary"),
        name="ret_state",
    )(kt, v, zeta_b, cd_b)


def _ret_main_body(x_ref, q_ref, kt_ref, v_ref, og_ref, sb_ref, dmat_ref, xif_ref, xib_ref, zf_ref, cdf_ref,
                   gn_ref, wo_ref, o_ref, sf_ref, gated_ref, *, dk, dv):
    @pl.when(pl.program_id(1) == 0)
    def _():
        sf_ref[...] = jnp.zeros(sf_ref.shape, F32)

    cb = dmat_ref.shape[1]
    for n in range(q_ref.shape[0] // cb):
        rows = slice(n * cb, (n + 1) * cb)
        for h in range(RET_HEADS):
            ks = slice(h * dk, (h + 1) * dk)
            vs = slice(h * dv, (h + 1) * dv)
            q = q_ref[rows, ks]
            kt = kt_ref[0, ks, rows]
            v = v_ref[rows, vs]
            s = _dot(q, kt) * dmat_ref[h]
            st = sf_ref[h]
            o = (_dot(s.astype(BF16), v) + _dot(q, st.astype(BF16)) * xif_ref[h]
                 + _dot(q, sb_ref[0, n, h]) * xib_ref[h])
            kz = (kt.astype(F32) * zf_ref[h]).astype(BF16)
            sf_ref[h] = cdf_ref[h] * st + _dot(kz, v)
            o = o * lax.rsqrt(jnp.mean(o * o, axis=-1, keepdims=True) + EPS) * gn_ref[:, vs]
            gated_ref[rows, vs] = (o * _silu(og_ref[rows, vs].astype(F32))).astype(BF16)
    o_ref[...] = x_ref[...] + _dot(gated_ref[...], wo_ref[...])


def _ret_main(x2, q, kt, v, og, sb, tabs, gn, wo, *, b, l, tb):
    t, d = x2.shape
    hk, hv = q.shape[1], v.shape[1]
    dk, dv = hk // RET_HEADS, hv // RET_HEADS
    ni = l // tb
    nsub = tb // tabs[0].shape[1]
    tok = lambda w: pl.BlockSpec((tb, w), lambda bi, i: (bi * ni + i, 0))
    body = functools.partial(_ret_main_body, dk=dk, dv=dv)
    return pl.pallas_call(
        body,
        grid=(b, ni),
        in_specs=[tok(d), tok(hk), pl.BlockSpec((1, hk, tb), lambda bi, i: (bi, 0, i)), tok(hv), tok(hv),
                  pl.BlockSpec((1, nsub, RET_HEADS, dk, dv), lambda bi, i: (bi, i, 0, 0, 0))]
                 + [_const_spec(a.shape) for a in tabs] + [_const_spec((1, hv)), _const_spec(wo.shape)],
        out_specs=tok(d),
        out_shape=jax.ShapeDtypeStruct((t, d), F32),
        scratch_shapes=[pltpu.VMEM((RET_HEADS, dk, dv), F32), pltpu.VMEM((tb, hv), BF16)],
        compiler_params=_params("parallel", "arbitrary"),
        name="ret_main",
    )(x2, q, kt, v, og, sb, *tabs, gn.reshape(1, hv), wo)


def _tile(l, want):
    t = min(l, want)
    while l % t:
        t //= 2
    return t


def _s5_gla_layer(x2, mix_g, w_in, lam_re, lam_im, b_re, b_im, c_re, c_im, log_dt, d_skip, w_glu,
                  w_gk, b_gk, gla_norm, w_out, ffn2, *, b, l):
    su = d_skip.shape[0]
    hk = w_gk.shape[-1]
    r = w_gk.shape[1]
    hv = gla_norm.shape[0]
    tm = _tile(l, TM_AB_IN)
    wut = w_in[:, :su].T.astype(BF16)
    wr = w_in[:, su:].astype(BF16)
    wgk = jnp.zeros((2 * r, 2 * hk), F32)
    wgk = wgk.at[:r, :hk].set(w_gk[0]).at[r:, hk:].set(w_gk[1]).astype(BF16)
    bgk = b_gk.reshape(1, 2 * hk).astype(F32)
    ut, q, k, v, og, gf, gb = _ab_in(x2, mix_g, wut, wr, wgk, bgk, b=b, l=l, tm=tm, hk=hk, hv=hv)

    nc = l // LANES
    taps, w_st, wc, pq = _s5_tables(lam_re, lam_im, b_re, b_im, c_re, c_im, log_dt, d_skip, nc)
    y4 = _s5(ut.reshape(b, su, nc, LANES), taps, w_st, wc, pq)
    yt = y4.reshape(b, su, l)

    o_f, o_b = _gla(q, k, v, gf, gb, b=b, l=l, tb=_tile(l, TB_GLA))
    return _ab_out(x2, yt, w_glu.T.astype(BF16), o_f, o_b, og, gla_norm, w_out.astype(BF16), ffn2,
                   b=b, l=l, tm=_tile(l, TM_AB_OUT))


def _retention_layer(x2, mix_g, w_in, ret_norm, w_out, *, b, l):
    hv = ret_norm.shape[0]
    hk = (w_in.shape[1] - 2 * hv) // 2
    dk = hk // RET_HEADS
    half = dk // 2
    tm = _tile(l, TM_RET_IN)
    pos = jnp.arange(l, dtype=F32)
    inv = jnp.exp(-math.log(ROPE_BASE) * jnp.arange(half, dtype=F32) / half)
    ang = pos[:, None] * inv[None, :]
    cos, sin = jnp.cos(ang), jnp.sin(ang)
    cos2 = jnp.concatenate([cos, cos], axis=1)
    sin2 = jnp.concatenate([-sin, sin], axis=1)
    w_qvg = jnp.concatenate([w_in[:, :hk], w_in[:, 2 * hk:]], axis=1).astype(BF16)
    w_kt = w_in[:, hk:2 * hk].T.astype(BF16)
    q, kt, v, og = _ret_in(x2, mix_g, w_qvg, w_kt, cos2, sin2, cos.T, sin.T, b=b, l=l, tm=tm, hk=hk, hv=hv)
    tb = _tile(l, RET_BLOCK)
    dmat, xi_f, xi_b, zeta_f, zeta_b, cd_f, cd_b = _ret_tables(tb, dk, hv // RET_HEADS)
    sb = _ret_state(kt, v, zeta_b, cd_b, b=b, l=l, tb=_tile(l, RET_STATE_BLOCKS * tb))
    return _ret_main(x2, q, kt, v, og, sb, (dmat, xi_f, xi_b, zeta_f, cd_f), ret_norm, w_out.astype(BF16),
                     b=b, l=l, tb=_tile(l, RET_MAIN_BLOCKS * tb))


def kernel(x, ffn1_norm, ffn1_w1, ffn1_w2, mix_norm, ffn2_norm, ffn2_w1, ffn2_w2, ab_w_in, s5_lambda_re, s5_lambda_im, s5_b_re, s5_b_im, s5_c_re, s5_c_im, s5_log_dt, s5_d, s5_w_glu, gla_w_gk, gla_b_gk, gla_norm, ab_w_out, ret_w_in, ret_norm, ret_w_out, final_norm):
    b, l, d = x.shape
    depth = ffn1_norm.shape[0]
    assert depth % 2 == 0, "the final norm is applied by the retention layer's second FFN"
    tm = _tile(b * l, TM_FFN)
    x2 = x.reshape(b * l, d).astype(F32)
    for i in range(depth):
        j = i // 2
        ffn1 = (ffn1_norm[i], ffn1_w1[i].astype(BF16), ffn1_w2[i].astype(BF16))
        ffn2 = (ffn2_norm[i], ffn2_w1[i].astype(BF16), ffn2_w2[i].astype(BF16))
        x2 = _ffn(x2, *ffn1, final_norm, final=False, tm=tm)
        if i % 2 == 0:
            x2 = _s5_gla_layer(x2, mix_norm[i], ab_w_in[j], s5_lambda_re[j], s5_lambda_im[j], s5_b_re[j],
                               s5_b_im[j], s5_c_re[j], s5_c_im[j], s5_log_dt[j], s5_d[j], s5_w_glu[j],
                               gla_w_gk[j], gla_b_gk[j], gla_norm[j], ab_w_out[j], ffn2, b=b, l=l)
        else:
            x2 = _retention_layer(x2, mix_norm[i], ret_w_in[j], ret_norm[j], ret_w_out[j], b=b, l=l)
            x2 = _ffn(x2, *ffn2, final_norm, final=(i == depth - 1), tm=tm)
    return x2.reshape(b, l, d)
```

```python
import functools
import math

import jax
import jax.numpy as jnp
from jax import lax
from jax.experimental import pallas as pl
from jax.experimental.pallas import tpu as pltpu

F32 = jnp.float32
BF16 = jnp.bfloat16

EPS = 1e-6
GLA_HEADS = 4
GLA_GATE_NORM = 16.0
GLA_CHUNK = 64
RET_HEADS = 8
ROPE_BASE = 10000.0

LANES = 128
VMEM_LIMIT = 56 * 1024 * 1024
HI = lax.Precision.HIGHEST

TM_FFN = 1024
TM_AB_IN = 1024
TM_AB_OUT = 512
TM_RET_IN = 512
TB_GLA = 1024
RET_BLOCK = 256
RET_STATE_BLOCKS = 4
RET_MAIN_BLOCKS = 2


def _const_spec(shape):
    nd = len(shape)
    return pl.BlockSpec(shape, lambda *_: (0,) * nd, pipeline_mode=pl.Buffered(1))


def _params(*sem):
    return pltpu.CompilerParams(dimension_semantics=sem, vmem_limit_bytes=VMEM_LIMIT)


def _rmsnorm(x, g):
    return x * lax.rsqrt(jnp.mean(x * x, axis=-1, keepdims=True) + EPS) * g


def _sigmoid(x):
    return 1.0 / (1.0 + jnp.exp(-x))


def _silu(x):
    return x * _sigmoid(x)


def _gelu_tanh(x):
    return 0.5 * x * (1.0 + jnp.tanh(math.sqrt(2.0 / math.pi) * (x + 0.044715 * (x * x * x))))


def _log_sigmoid(x):
    return jnp.minimum(x, 0.0) - jnp.log1p(jnp.exp(-jnp.abs(x)))


def _dot(a, b):
    return jnp.dot(a, b, preferred_element_type=F32)


def _dot_nt(a, b):
    return lax.dot_general(a, b, (((1,), (1,)), ((), ())), preferred_element_type=F32)


def _dot_tn(a, b):
    return lax.dot_general(a, b, (((0,), (0,)), ((), ())), preferred_element_type=F32)


def _ffn_apply(x, g, w1_ref, w2_ref):
    d_ff = w2_ref.shape[0]
    f_chunk = 256 if d_ff % 256 == 0 else d_ff
    xn = _rmsnorm(x, g).astype(BF16)
    acc = jnp.zeros(x.shape, F32)
    for f0 in range(0, d_ff, f_chunk):
        gate = _dot(xn, w1_ref[:, f0:f0 + f_chunk])
        up = _dot(xn, w1_ref[:, d_ff + f0:d_ff + f0 + f_chunk])
        act = (_silu(gate) * up).astype(BF16)
        acc = acc + _dot(act, w2_ref[f0:f0 + f_chunk, :])
    return x + 0.5 * acc


def _ffn_body(x_ref, g_ref, w1_ref, w2_ref, fg_ref, o_ref, *, final):
    y = _ffn_apply(x_ref[...], g_ref[...], w1_ref, w2_ref)
    if final:
        y = _rmsnorm(y, fg_ref[...])
    o_ref[...] = y


def _ffn(x2, g, w1, w2, fg, *, final, tm):
    t, d = x2.shape
    body = functools.partial(_ffn_body, final=final)
    return pl.pallas_call(
        body,
        grid=(t // tm,),
        in_specs=[pl.BlockSpec((tm, d), lambda i: (i, 0)),
                  _const_spec((1, d)), _const_spec(w1.shape), _const_spec(w2.shape),
                  _const_spec((1, d))],
        out_specs=pl.BlockSpec((tm, d), lambda i: (i, 0)),
        out_shape=jax.ShapeDtypeStruct((t, d), F32),
        compiler_params=_params("parallel"),
        name="ffn",
    )(x2, g.reshape(1, d), w1, w2, fg.reshape(1, d))


def _ab_in_body(x_ref, g_ref, wut_ref, wr_ref, wgk_ref, bgk_ref,
                ut_ref, q_ref, k_ref, v_ref, og_ref, gf_ref, gb_ref, *, hk, hv):
    h = _rmsnorm(x_ref[...], g_ref[...]).astype(BF16)
    ut_ref[0] = _dot_nt(wut_ref[...], h).astype(BF16)
    proj = _dot(h, wr_ref[...])
    q_ref[...] = proj[:, :hk] * (hk // GLA_HEADS) ** -0.5
    k_ref[...] = proj[:, hk:2 * hk]
    v_ref[...] = proj[:, 2 * hk:2 * hk + hv].astype(BF16)
    og_ref[...] = proj[:, 2 * hk + hv:2 * hk + 2 * hv].astype(BF16)
    glo = proj[:, 2 * hk + 2 * hv:].astype(BF16)
    gk = _log_sigmoid(_dot(glo, wgk_ref[...]) + bgk_ref[...]) * (1.0 / GLA_GATE_NORM)
    gf_ref[...] = gk[:, :hk]
    gb_ref[...] = gk[:, hk:]


def _ab_in(x2, g, wut, wr, wgk, bgk, *, b, l, tm, hk, hv):
    t, d = x2.shape
    su = wut.shape[0]
    ni = l // tm
    tok = lambda w: pl.BlockSpec((tm, w), lambda bi, i: (bi * ni + i, 0))
    body = functools.partial(_ab_in_body, hk=hk, hv=hv)
    return pl.pallas_call(
        body,
        grid=(b, ni),
        in_specs=[tok(d), _const_spec((1, d)), _const_spec(wut.shape), _const_spec(wr.shape),
                  _const_spec(wgk.shape), _const_spec(bgk.shape)],
        out_specs=[pl.BlockSpec((1, su, tm), lambda bi, i: (bi, 0, i)),
                   tok(hk), tok(hk), tok(hv), tok(hv), tok(hk), tok(hk)],
        out_shape=[jax.ShapeDtypeStruct((b, su, l), BF16),
                   jax.ShapeDtypeStruct((t, hk), F32), jax.ShapeDtypeStruct((t, hk), F32),
                   jax.ShapeDtypeStruct((t, hv), BF16), jax.ShapeDtypeStruct((t, hv), BF16),
                   jax.ShapeDtypeStruct((t, hk), F32), jax.ShapeDtypeStruct((t, hk), F32)],
        compiler_params=_params("parallel", "parallel"),
        name="ab_in",
    )(x2, g.reshape(1, d), wut, wr, wgk, bgk)


def _cpow_table(ar, ai, count):
    ks = jnp.arange(count)
    pr = jnp.ones(ar.shape + (count,), F32)
    pi = jnp.zeros(ar.shape + (count,), F32)
    sr, si = ar, ai
    for bit in range(max(1, (count - 1).bit_length())):
        on = ((ks >> bit) & 1) == 1
        nr = pr * sr[..., None] - pi * si[..., None]
        ni = pr * si[..., None] + pi * sr[..., None]
        pr = jnp.where(on, nr, pr)
        pi = jnp.where(on, ni, pi)
        sr, si = sr * sr - si * si, 2.0 * sr * si
    return pr, pi


def _s5_tables(lam_re, lam_im, b_re, b_im, c_re, c_im, log_dt, d_skip, n_chunks):
    ch = LANES
    g_, n_ = lam_re.shape[1], lam_re.shape[2]
    p_ = b_re.shape[-1]
    f32 = lambda a: a.astype(F32)
    lr = jnp.minimum(f32(lam_re), -1e-4)
    li = f32(lam_im)
    dt = jnp.exp(f32(log_dt))[..., None]
    mag = jnp.exp(lr * dt)
    ar = mag * jnp.cos(li * dt)
    ai = mag * jnp.sin(li * dt)
    den = lr * lr + li * li
    cr = ((ar - 1.0) * lr + ai * li) / den
    ci = (ai * lr - (ar - 1.0) * li) / den
    br, bi = f32(b_re), f32(b_im)
    bbr = cr[..., None] * br - ci[..., None] * bi
    bbi = cr[..., None] * bi + ci[..., None] * br
    ccr, cci = f32(c_re), f32(c_im)
    pwr, pwi = _cpow_table(ar, ai, ch + 1)

    cnr = ccr.transpose(0, 1, 3, 2)[..., None]
    cni = cci.transpose(0, 1, 3, 2)[..., None]
    cbr = cnr * bbr[:, :, :, None, :] - cni * bbi[:, :, :, None, :]
    cbi = cnr * bbi[:, :, :, None, :] + cni * bbr[:, :, :, None, :]
    taps = (jnp.einsum('dgnpq,dgnl->dgqpl', cbr, pwr[..., :ch], precision=HI)
            - jnp.einsum('dgnpq,dgnl->dgqpl', cbi, pwi[..., :ch], precision=HI))
    kf, kb = taps[0], taps[1]
    dmat = f32(d_skip).reshape(g_, p_)[:, None, :, None] * jnp.eye(p_, dtype=F32)[None, :, :, None]
    center = kf[..., 0:1] + kb[..., 0:1] + dmat
    kfull = jnp.concatenate([kb[..., :0:-1], center, kf[..., 1:]], axis=-1)
    kpad = jnp.pad(kfull, ((0, 0),) * 3 + ((0, 1),))

    def win(d, rev):
        pr = pwr[d][..., :ch]
        pi = pwi[d][..., :ch]
        if rev:
            pr, pi = pr[..., ::-1], pi[..., ::-1]
        pr = pr.transpose(0, 2, 1)[:, None]
        pi = pi.transpose(0, 2, 1)[:, None]
        xr = bbr[d].transpose(0, 2, 1)[:, :, None, :]
        xi = bbi[d].transpose(0, 2, 1)[:, :, None, :]
        return pr * xr - pi * xi, pr * xi + pi * xr
    fre, fim = win(0, True)
    bre, bim = win(1, False)
    w_in = jnp.concatenate([fre, fim, bre, bim], axis=-1).reshape(g_, p_ * ch, 4 * n_).astype(BF16)

    def wout(d, rev):
        pr = pwr[d][..., 1:ch + 1]
        pi = pwi[d][..., 1:ch + 1]
        if rev:
            pr, pi = pr[..., ::-1], pi[..., ::-1]
        pr = pr[:, :, None, :]
        pi = pi[:, :, None, :]
        xr = ccr[d].transpose(0, 2, 1)[..., None]
        xi = cci[d].transpose(0, 2, 1)[..., None]
        return xr * pr - xi * pi, -(xr * pi + xi * pr)
    fre, fim = wout(0, False)
    bre, bim = wout(1, True)
    wc = jnp.concatenate([fre, fim, bre, bim], axis=1).reshape(g_, 4 * n_, p_ * ch).astype(BF16)

    steps = max(1, (n_chunks - 1).bit_length())
    sr, si = pwr[..., ch], pwi[..., ch]
    rows = []
    for _ in range(steps):
        rows.append(jnp.stack([jnp.concatenate([sr, sr], -1), jnp.concatenate([-si, si], -1)], axis=2))
        sr, si = sr * sr - si * si, 2.0 * sr * si
    pq = jnp.stack(rows, axis=2)
    pq = pq.transpose(1, 0, 2, 3, 4)
    return kpad, w_in, wc, pq


def _s5_body(u_ref, taps_ref, win_ref, wc_ref, pq_ref, y_ref, toep_ref, *, nb, nc, p, n, steps):
    ch = LANES
    m = nb * nc
    u = jnp.concatenate([u_ref[:, q].reshape(m, ch) for q in range(p)], axis=1)
    r = _dot(u, win_ref[0])
    c = lax.broadcasted_iota(jnp.int32, (m, 2 * n), 0) % nc

    def scan(x, d):
        for k in range(steps):
            s = 1 << k
            if d == 0:
                t = jnp.where(c >= s, pltpu.roll(x, s, axis=0), 0.0)
            else:
                t = jnp.where(c < nc - s, pltpu.roll(x, m - s, axis=0), 0.0)
            x = x + t * pq_ref[0, d, k, 0:1, :] + pltpu.roll(t, n, axis=1) * pq_ref[0, d, k, 1:2, :]
        if d == 0:
            return jnp.where(c >= 1, pltpu.roll(x, 1, axis=0), 0.0)
        return jnp.where(c < nc - 1, pltpu.roll(x, m - 1, axis=0), 0.0)

    xf = scan(r[:, :2 * n], 0)
    xb = scan(r[:, 2 * n:], 1)
    xc = jnp.concatenate([xf, xb], axis=1).astype(BF16)
    for p0 in range(0, p, 2):
        cols = slice(p0 * ch, (p0 + 2) * ch)
        for q in range(p):
            for pp in (p0, p0 + 1):
                row = jnp.broadcast_to(taps_ref[0, q, pp:pp + 1, :], (ch, 2 * ch))
                blk = pltpu.roll(row, ch + 1, 1, stride=1, stride_axis=0)[:, :ch]
                toep_ref[q * ch:(q + 1) * ch, pp * ch:(pp + 1) * ch] = blk.astype(BF16)
        y = _dot(u, toep_ref[:, cols]) + _dot(xc, wc_ref[0, :, cols])
        y_ref[:, p0] = y[:, :ch].reshape(nb, nc, ch)
        y_ref[:, p0 + 1] = y[:, ch:].reshape(nb, nc, ch)


def _s5(u4, taps, w_in, wc, pq):
    nb, su, nc, ch = u4.shape
    g_ = taps.shape[0]
    p = su // g_
    n = wc.shape[1] // 4
    steps = pq.shape[2]
    body = functools.partial(_s5_body, nb=nb, nc=nc, p=p, n=n, steps=steps)
    return pl.pallas_call(
        body,
        grid=(g_,),
        in_specs=[pl.BlockSpec((nb, p, nc, ch), lambda g: (0, g, 0, 0)),
                  pl.BlockSpec((1,) + taps.shape[1:], lambda g: (g, 0, 0, 0)),
                  pl.BlockSpec((1,) + w_in.shape[1:], lambda g: (g, 0, 0)),
                  pl.BlockSpec((1,) + wc.shape[1:], lambda g: (g, 0, 0)),
                  pl.BlockSpec((1,) + pq.shape[1:], lambda g: (g, 0, 0, 0, 0))],
        out_specs=pl.BlockSpec((nb, p, nc, ch), lambda g: (0, g, 0, 0)),
        out_shape=jax.ShapeDtypeStruct((nb, su, nc, ch), F32),
        scratch_shapes=[pltpu.VMEM((p * ch, p * ch), BF16)],
        compiler_params=_params("parallel"),
        name="s5",
    )(u4, taps, w_in, wc, pq)


def _gla_prep(q, k, v, g, tri, cmask, bdk, bd, ones, last_row):
    g_hi = g.astype(BF16)
    g_lo = (g - g_hi.astype(F32)).astype(BF16)
    cum = _dot(tri, g_hi) + _dot(tri, g_lo)
    last = cum[last_row:last_row + 1, :]
    q_dec = (q * jnp.exp(cum)).astype(BF16)
    k_inv = (k * jnp.exp(-cum)).astype(BF16)
    k_dec = (k * jnp.exp(last - cum)).astype(BF16)
    kb = jnp.where(bdk, jnp.concatenate([k_inv] * GLA_HEADS, axis=0), 0.0)
    s = jnp.where(cmask, _dot_nt(q_dec, kb), 0.0).astype(BF16)
    vb = jnp.where(bd, jnp.concatenate([v] * GLA_HEADS, axis=0), 0.0)
    o_intra = _dot(s, vb)
    tot = _dot_tn(g_hi, ones) + _dot_tn(g_lo, ones)
    kv = jnp.where(bd, _dot_tn(k_dec, v), 0.0)
    return q_dec, o_intra, kv, jnp.exp(tot)


def _gla_sweep(preps, order, st_ref, o_ref, c):
    st = st_ref[...]
    for n in order:
        q_dec, o_intra, kv, decay = preps[n]
        o_ref[n * c:(n + 1) * c, :] = o_intra + _dot(q_dec, st.astype(BF16))
        st = jnp.concatenate([decay] * (st.shape[1] // LANES), axis=1) * st + kv
    st_ref[...] = st


def _gla_body(qf_ref, kf_ref, vf_ref, gf_ref, qb_ref, kb_ref, vb_ref, gb_ref,
              of_ref, ob_ref, sf_ref, sb_ref, *, tb, dk, dv):
    c = GLA_CHUNK

    @pl.when(pl.program_id(1) == 0)
    def _():
        sf_ref[...] = jnp.zeros(sf_ref.shape, F32)
        sb_ref[...] = jnp.zeros(sb_ref.shape, F32)

    ii = lax.broadcasted_iota(jnp.int32, (c, c), 0)
    jj = lax.broadcasted_iota(jnp.int32, (c, c), 1)
    tri_f = (jj <= ii).astype(BF16)
    tri_b = (jj >= ii).astype(BF16)
    hc = GLA_HEADS * c
    i2 = lax.broadcasted_iota(jnp.int32, (c, hc), 0)
    j2 = lax.broadcasted_iota(jnp.int32, (c, hc), 1) % c
    cmask_f = j2 <= i2
    cmask_b = j2 > i2
    bdk = (lax.broadcasted_iota(jnp.int32, (hc, GLA_HEADS * dk), 0) // c
           == lax.broadcasted_iota(jnp.int32, (hc, GLA_HEADS * dk), 1) // dk)
    bd = (lax.broadcasted_iota(jnp.int32, (hc, GLA_HEADS * dv), 0) // c
          == lax.broadcasted_iota(jnp.int32, (hc, GLA_HEADS * dv), 1) // dv)
    ones = jnp.ones((c, LANES), BF16)
    nch = tb // c
    preps_f, preps_b = [], []
    for n in range(nch):
        r = slice(n * c, (n + 1) * c)
        preps_f.append(_gla_prep(qf_ref[r, :], kf_ref[r, :], vf_ref[r, :], gf_ref[r, :],
                                 tri_f, cmask_f, bdk, bd, ones, c - 1))
        preps_b.append(_gla_prep(qb_ref[r, :], kb_ref[r, :], vb_ref[r, :], gb_ref[r, :],
                                 tri_b, cmask_b, bdk, bd, ones, 0))
    _gla_sweep(preps_f, range(nch), sf_ref, of_ref, c)
    _gla_sweep(preps_b, range(nch - 1, -1, -1), sb_ref, ob_ref, c)


def _gla(q, k, v, gf, gb, *, b, l, tb):
    t, hk = q.shape
    hv = v.shape[1]
    dk, dv = hk // GLA_HEADS, hv // GLA_HEADS
    ni = l // tb
    fwd = lambda w: pl.BlockSpec((tb, w), lambda bi, i: (bi * ni + i, 0))
    bwd = lambda w: pl.BlockSpec((tb, w), lambda bi, i: (bi * ni + ni - 1 - i, 0))
    assert dk == GLA_CHUNK, "state mask reuse needs chunk == key head width"
    body = functools.partial(_gla_body, tb=tb, dk=dk, dv=dv)
    state = pltpu.VMEM((hk, hv), F32)
    return pl.pallas_call(
        body,
        grid=(b, ni),
        in_specs=[fwd(hk), fwd(hk), fwd(hv), fwd(hk), bwd(hk), bwd(hk), bwd(hv), bwd(hk)],
        out_specs=[fwd(hv), bwd(hv)],
        out_shape=[jax.ShapeDtypeStruct((t, hv), F32), jax.ShapeDtypeStruct((t, hv), F32)],
        scratch_shapes=[state, state],
        compiler_params=_params("parallel", "arbitrary"),
        name="gla",
    )(q, k, v, gf, q, k, v, gb)


def _head_norm(o, g, heads):
    dh = o.shape[1] // heads
    parts = []
    for h in range(heads):
        oh = o[:, h * dh:(h + 1) * dh]
        parts.append(oh * lax.rsqrt(jnp.mean(oh * oh, axis=-1, keepdims=True) + EPS))
    return jnp.concatenate(parts, axis=1) * g


def _ab_out_body(x_ref, yt_ref, wglut_ref, of_ref, ob_ref, og_ref, gn_ref, wo_ref, fg_ref, w1_ref, w2_ref,
                 o_ref, *, su):
    gy = _gelu_tanh(yt_ref[0])
    z = _dot(wglut_ref[...], gy.astype(BF16))
    s5_out = (gy * _sigmoid(z)).T.astype(BF16)
    o = _head_norm(of_ref[...] + ob_ref[...], gn_ref[...], GLA_HEADS)
    gla_out = (o * _silu(og_ref[...].astype(F32))).astype(BF16)
    x1 = x_ref[...] + _dot(s5_out, wo_ref[:su, :]) + _dot(gla_out, wo_ref[su:, :])
    o_ref[...] = _ffn_apply(x1, fg_ref[...], w1_ref, w2_ref)


def _ab_out(x2, yt, wglut, o_f, o_b, og, gn, wo, ffn, *, b, l, tm):
    t, d = x2.shape
    su = yt.shape[1]
    hv = o_f.shape[1]
    ni = l // tm
    fg, w1, w2 = ffn
    tok = lambda w: pl.BlockSpec((tm, w), lambda bi, i: (bi * ni + i, 0))
    body = functools.partial(_ab_out_body, su=su)
    return pl.pallas_call(
        body,
        grid=(b, ni),
        in_specs=[tok(d), pl.BlockSpec((1, su, tm), lambda bi, i: (bi, 0, i)), _const_spec(wglut.shape),
                  tok(hv), tok(hv), tok(hv), _const_spec((1, hv)), _const_spec(wo.shape),
                  _const_spec((1, d)), _const_spec(w1.shape), _const_spec(w2.shape)],
        out_specs=tok(d),
        out_shape=jax.ShapeDtypeStruct((t, d), F32),
        compiler_params=_params("parallel", "parallel"),
        name="ab_out_ffn",
    )(x2, yt, wglut, o_f, o_b, og, gn.reshape(1, hv), wo, fg.reshape(1, d), w1, w2)


def _ret_in_body(x_ref, g_ref, w_ref, wkt_ref, cos_ref, sin_ref, cost_ref, sint_ref,
                 q_ref, kt_ref, v_ref, og_ref, *, hk, hv, dk):
    h = _rmsnorm(x_ref[...], g_ref[...]).astype(BF16)
    cos2 = cos_ref[...]
    sin2 = sin_ref[...]
    q = _dot(h, w_ref[:, :hk])
    q_ref[...] = jnp.concatenate(
        [q[:, s:s + dk] * cos2 + pltpu.roll(q[:, s:s + dk], dk // 2, axis=1) * sin2 for s in range(0, hk, dk)],
        axis=1).astype(BF16)
    kt = _dot_nt(wkt_ref[...], h) * dk ** -0.5
    cost = cost_ref[...]
    sint = sint_ref[...]
    parts = []
    for s in range(0, hk, dk):
        t1 = kt[s:s + dk // 2]
        t2 = kt[s + dk // 2:s + dk]
        parts += [t1 * cost - t2 * sint, t1 * sint + t2 * cost]
    kt_ref[0] = jnp.concatenate(parts, axis=0).astype(BF16)
    v_ref[...] = _dot(h, w_ref[:, hk:hk + hv]).astype(BF16)
    og_ref[...] = _dot(h, w_ref[:, hk + hv:]).astype(BF16)


def _ret_in(x2, g, w, wkt, cos2, sin2, cost, sint, *, b, l, tm, hk, hv):
    t, d = x2.shape
    dk = hk // RET_HEADS
    ni = l // tm
    tok = lambda w_: pl.BlockSpec((tm, w_), lambda bi, i: (bi * ni + i, 0))
    pos = pl.BlockSpec((tm, dk), lambda bi, i: (i, 0))
    post = pl.BlockSpec((dk // 2, tm), lambda bi, i: (0, i))
    body = functools.partial(_ret_in_body, hk=hk, hv=hv, dk=dk)
    return pl.pallas_call(
        body,
        grid=(b, ni),
        in_specs=[tok(d), _const_spec((1, d)), _const_spec(w.shape), _const_spec(wkt.shape), pos, pos, post, post],
        out_specs=[tok(hk), pl.BlockSpec((1, hk, tm), lambda bi, i: (bi, 0, i)), tok(hv), tok(hv)],
        out_shape=[jax.ShapeDtypeStruct((t, hk), BF16), jax.ShapeDtypeStruct((b, hk, l), BF16),
                   jax.ShapeDtypeStruct((t, hv), BF16), jax.ShapeDtypeStruct((t, hv), BF16)],
        compiler_params=_params("parallel", "parallel"),
        name="ret_in",
    )(x2, g.reshape(1, d), w, wkt, cos2, sin2, cost, sint)


def _ret_tables(c, dk, dv):
    lg_f = jnp.log1p(-jnp.exp2(-5.0 - jnp.arange(RET_HEADS, dtype=F32)))
    lg_b = lg_f[::-1]
    idx = jnp.arange(c)
    diff = (idx[:, None] - idx[None, :]).astype(F32)
    dmat = jnp.where(diff >= 0, jnp.exp(jnp.maximum(diff, 0.0)[None] * lg_f[:, None, None]),
                     jnp.exp(jnp.maximum(-diff, 0.0)[None] * lg_b[:, None, None]))
    pos = idx.astype(F32)
    ones_v = jnp.ones((1, 1, dv), F32)
    xi_f = jnp.exp((pos + 1.0)[None, :] * lg_f[:, None])[..., None] * ones_v
    xi_b = jnp.exp((c - pos)[None, :] * lg_b[:, None])[..., None] * ones_v
    zeta_f = jnp.exp((c - 1.0 - pos)[None, :] * lg_f[:, None])[:, None, :]
    zeta_b = jnp.exp(pos[None, :] * lg_b[:, None])[:, None, :]
    cd_f = jnp.exp(c * lg_f)[:, None, None] * ones_v
    cd_b = jnp.exp(c * lg_b)[:, None, None] * ones_v
    return dmat, xi_f, xi_b, zeta_f, zeta_b, cd_f, cd_b


def _ret_state_body(kt_ref, v_ref, zb_ref, cdb_ref, sb_out_ref, sb_ref, *, dk, dv):
    @pl.when(pl.program_id(1) == 0)
    def _():
        sb_ref[...] = jnp.zeros(sb_ref.shape, F32)

    cb = zb_ref.shape[2]
    for n in range(v_ref.shape[0] // cb - 1, -1, -1):
        rows = slice(n * cb, (n + 1) * cb)
        for h in range(RET_HEADS):
            st = sb_ref[h]
            sb_out_ref[0, n, h] = st.astype(BF16)
            kz = (kt_ref[0, h * dk:(h + 1) * dk, rows].astype(F32) * zb_ref[h]).astype(BF16)
            sb_ref[h] = cdb_ref[h] * st + _dot(kz, v_ref[rows, h * dv:(h + 1) * dv])


def _ret_state(kt, v, zeta_b, cd_b, *, b, l, tb):
    hk, hv = kt.shape[1], v.shape[1]
    dk, dv = hk // RET_HEADS, hv // RET_HEADS
    ni = l // tb
    nsub = tb // zeta_b.shape[2]
    bwd = lambda w: pl.BlockSpec((tb, w), lambda bi, i: (bi * ni + ni - 1 - i, 0))
    body = functools.partial(_ret_state_body, dk=dk, dv=dv)
    return pl.pallas_call(
        body,
        grid=(b, ni),
        in_specs=[pl.BlockSpec((1, hk, tb), lambda bi, i: (bi, 0, ni - 1 - i)), bwd(hv),
                  _const_spec(zeta_b.shape), _const_spec(cd_b.shape)],
        out_specs=pl.BlockSpec((1, nsub, RET_HEADS, dk, dv), lambda bi, i: (bi, ni - 1 - i, 0, 0, 0)),
        out_shape=jax.ShapeDtypeStruct((b, ni * nsub, RET_HEADS, dk, dv), BF16),
        scratch_shapes=[pltpu.VMEM((RET_HEADS, dk, dv), F32)],
        compiler_params=_params("parallel", "arbitrary"),
        name="ret_state",
    )(kt, v, zeta_b, cd_b)


def _ret_main_body(x_ref, q_ref, kt_ref, v_ref, og_ref, sb_ref, dmat_ref, xif_ref, xib_ref, zf_ref, cdf_ref,
                   gn_ref, wo_ref, o_ref, sf_ref, gated_ref, *, dk, dv):
    @pl.when(pl.program_id(1) == 0)
    def _():
        sf_ref[...] = jnp.zeros(sf_ref.shape, F32)

    cb = dmat_ref.shape[1]
    for n in range(q_ref.shape[0] // cb):
        rows = slice(n * cb, (n + 1) * cb)
        for h in range(RET_HEADS):
            ks = slice(h * dk, (h + 1) * dk)
            vs = slice(h * dv, (h + 1) * dv)
            q = q_ref[rows, ks]
            kt = kt_ref[0, ks, rows]
            v = v_ref[rows, vs]
            s = _dot(q, kt) * dmat_ref[h]
            st = sf_ref[h]
            qf = (q.astype(F32) * xif_ref[h][:, :dk]).astype(BF16)
            qb = (q.astype(F32) * xib_ref[h][:, :dk]).astype(BF16)
            o = _dot(jnp.concatenate([s.astype(BF16), qf, qb], axis=1),
                     jnp.concatenate([v, st.astype(BF16), sb_ref[0, n, h]], axis=0))
            kz = (kt.astype(F32) * zf_ref[h]).astype(BF16)
            sf_ref[h] = cdf_ref[h] * st + _dot(kz, v)
            o = o * lax.rsqrt(jnp.mean(o * o, axis=-1, keepdims=True) + EPS) * gn_ref[:, vs]
            gated_ref[rows, vs] = (o * _silu(og_ref[rows, vs].astype(F32))).astype(BF16)
    o_ref[...] = x_ref[...] + _dot(gated_ref[...], wo_ref[...])


def _ret_main(x2, q, kt, v, og, sb, tabs, gn, wo, *, b, l, tb):
    t, d = x2.shape
    hk, hv = q.shape[1], v.shape[1]
    dk, dv = hk // RET_HEADS, hv // RET_HEADS
    ni = l // tb
    nsub = tb // tabs[0].shape[1]
    tok = lambda w: pl.BlockSpec((tb, w), lambda bi, i: (bi * ni + i, 0))
    body = functools.partial(_ret_main_body, dk=dk, dv=dv)
    return pl.pallas_call(
        body,
        grid=(b, ni),
        in_specs=[tok(d), tok(hk), pl.BlockSpec((1, hk, tb), lambda bi, i: (bi, 0, i)), tok(hv), tok(hv),
                  pl.BlockSpec((1, nsub, RET_HEADS, dk, dv), lambda bi, i: (bi, i, 0, 0, 0))]
                 + [_const_spec(a.shape) for a in tabs] + [_const_spec((1, hv)), _const_spec(wo.shape)],
        out_specs=tok(d),
        out_shape=jax.ShapeDtypeStruct((t, d), F32),
        scratch_shapes=[pltpu.VMEM((RET_HEADS, dk, dv), F32), pltpu.VMEM((tb, hv), BF16)],
        compiler_params=_params("parallel", "arbitrary"),
        name="ret_main",
    )(x2, q, kt, v, og, sb, *tabs, gn.reshape(1, hv), wo)


def _tile(l, want):
    t = min(l, want)
    while l % t:
        t //= 2
    return t


def _s5_gla_layer(x2, mix_g, w_in, lam_re, lam_im, b_re, b_im, c_re, c_im, log_dt, d_skip, w_glu,
                  w_gk, b_gk, gla_norm, w_out, ffn2, *, b, l):
    su = d_skip.shape[0]
    hk = w_gk.shape[-1]
    r = w_gk.shape[1]
    hv = gla_norm.shape[0]
    tm = _tile(l, TM_AB_IN)
    wut = w_in[:, :su].T.astype(BF16)
    wr = w_in[:, su:].astype(BF16)
    wgk = jnp.zeros((2 * r, 2 * hk), F32)
    wgk = wgk.at[:r, :hk].set(w_gk[0]).at[r:, hk:].set(w_gk[1]).astype(BF16)
    bgk = b_gk.reshape(1, 2 * hk).astype(F32)
    ut, q, k, v, og, gf, gb = _ab_in(x2, mix_g, wut, wr, wgk, bgk, b=b, l=l, tm=tm, hk=hk, hv=hv)

    nc = l // LANES
    taps, w_st, wc, pq = _s5_tables(lam_re, lam_im, b_re, b_im, c_re, c_im, log_dt, d_skip, nc)
    y4 = _s5(ut.reshape(b, su, nc, LANES), taps, w_st, wc, pq)
    yt = y4.reshape(b, su, l)

    o_f, o_b = _gla(q, k, v, gf, gb, b=b, l=l, tb=_tile(l, TB_GLA))
    return _ab_out(x2, yt, w_glu.T.astype(BF16), o_f, o_b, og, gla_norm, w_out.astype(BF16), ffn2,
                   b=b, l=l, tm=_tile(l, TM_AB_OUT))


def _retention_layer(x2, mix_g, w_in, ret_norm, w_out, *, b, l):
    hv = ret_norm.shape[0]
    hk = (w_in.shape[1] - 2 * hv) // 2
    dk = hk // RET_HEADS
    half = dk // 2
    tm = _tile(l, TM_RET_IN)
    pos = jnp.arange(l, dtype=F32)
    inv = jnp.exp(-math.log(ROPE_BASE) * jnp.arange(half, dtype=F32) / half)
    ang = pos[:, None] * inv[None, :]
    cos, sin = jnp.cos(ang), jnp.sin(ang)
    cos2 = jnp.concatenate([cos, cos], axis=1)
    sin2 = jnp.concatenate([-sin, sin], axis=1)
    w_qvg = jnp.concatenate([w_in[:, :hk], w_in[:, 2 * hk:]], axis=1).astype(BF16)
    w_kt = w_in[:, hk:2 * hk].T.astype(BF16)
    q, kt, v, og = _ret_in(x2, mix_g, w_qvg, w_kt, cos2, sin2, cos.T, sin.T, b=b, l=l, tm=tm, hk=hk, hv=hv)
    tb = _tile(l, RET_BLOCK)
    dmat, xi_f, xi_b, zeta_f, zeta_b, cd_f, cd_b = _ret_tables(tb, dk, hv // RET_HEADS)
    sb = _ret_state(kt, v, zeta_b, cd_b, b=b, l=l, tb=_tile(l, RET_STATE_BLOCKS * tb))
    return _ret_main(x2, q, kt, v, og, sb, (dmat, xi_f, xi_b, zeta_f, cd_f), ret_norm, w_out.astype(BF16),
                     b=b, l=l, tb=_tile(l, RET_MAIN_BLOCKS * tb))


def kernel(x, ffn1_norm, ffn1_w1, ffn1_w2, mix_norm, ffn2_norm, ffn2_w1, ffn2_w2, ab_w_in, s5_lambda_re, s5_lambda_im, s5_b_re, s5_b_im, s5_c_re, s5_c_im, s5_log_dt, s5_d, s5_w_glu, gla_w_gk, gla_b_gk, gla_norm, ab_w_out, ret_w_in, ret_norm, ret_w_out, final_norm):
    b, l, d = x.shape
    depth = ffn1_norm.shape[0]
    assert depth % 2 == 0, "the final norm is applied by the retention layer's second FFN"
    tm = _tile(b * l, TM_FFN)
    x2 = x.reshape(b * l, d).astype(F32)
    for i in range(depth):
        j = i // 2
        ffn1 = (ffn1_norm[i], ffn1_w1[i].astype(BF16), ffn1_w2[i].astype(BF16))
        ffn2 = (ffn2_norm[i], ffn2_w1[i].astype(BF16), ffn2_w2[i].astype(BF16))
        x2 = _ffn(x2, *ffn1, final_norm, final=False, tm=tm)
        if i % 2 == 0:
            x2 = _s5_gla_layer(x2, mix_norm[i], ab_w_in[j], s5_lambda_re[j], s5_lambda_im[j], s5_b_re[j],
                               s5_b_im[j], s5_c_re[j], s5_c_im[j], s5_log_dt[j], s5_d[j], s5_w_glu[j],
                               gla_w_gk[j], gla_b_gk[j], gla_norm[j], ab_w_out[j], ffn2, b=b, l=l)
        else:
            x2 = _retention_layer(x2, mix_norm[i], ret_w_in[j], ret_norm[j], ret_w_out[j], b=b, l=l)
            x2 = _ffn(x2, *ffn2, final_norm, final=(i == depth - 1), tm=tm)
    return x2.reshape(b, l, d)
```
